```python
import jax, jax.numpy as jnp
from jax import lax
import numpy as np

D_MODEL = 1024
BATCH = 2
SEQ = 8192
DEPTH = 1
DEC_BATCH = 16
DEC_SEQ = 64
PAST_LEN = 1024

CHUNK = 64
Q_BLOCK = 128
HEAD_DIM = 64
N_SB_HEADS = 8
N_FOX_HEADS = 8
SB_WIDTH = N_SB_HEADS * HEAD_DIM
FOX_WIDTH = N_FOX_HEADS * HEAD_DIM
MIX_WIDTH = SB_WIDTH + FOX_WIDTH
IN_COLS = 3 * SB_WIDTH + 3 * FOX_WIDTH + N_FOX_HEADS
D_FF = -(-8 * D_MODEL // (3 * 256)) * 256
EPS = 1e-6
SPLITS = [SB_WIDTH, 2 * SB_WIDTH, 3 * SB_WIDTH,
          3 * SB_WIDTH + FOX_WIDTH, 3 * SB_WIDTH + 2 * FOX_WIDTH, 3 * SB_WIDTH + 3 * FOX_WIDTH]

kernel_name = "stickbreak_forgetting_hybrid_stream_step"


def rmsnorm(x, g):
    xf = x.astype(jnp.float32)
    y = xf * lax.rsqrt(jnp.mean(xf * xf, axis=-1, keepdims=True) + EPS)
    return (y * g.astype(jnp.float32)).astype(x.dtype)


def ada_terms(c, w_ada, b_ada):
    a = jax.nn.silu(c) @ w_ada + b_ada
    return [t[:, None, :] for t in jnp.split(a, 6, axis=-1)]


def sb_attend(q, k, v, q_pos, k_pos):
    z = jnp.einsum('bqhd,bkhd->bhqk', q.astype(jnp.float32), k.astype(jnp.float32)) * (HEAD_DIM ** -0.5)
    mask = k_pos[None, :] < q_pos[:, None]
    log_keep = jnp.where(mask, jax.nn.log_sigmoid(-z), 0.0)
    between = lax.cumsum(log_keep, axis=3, reverse=True) - log_keep
    a = jnp.where(mask, jnp.exp(jax.nn.log_sigmoid(z) + between), 0.0)
    return jnp.einsum('bhqk,bkhd->bqhd', a, v.astype(jnp.float32)).astype(v.dtype)


def fox_attend(q, k, v, fq, fk, q_pos, k_pos):
    s = jnp.einsum('bqhd,bkhd->bhqk', q.astype(jnp.float32), k.astype(jnp.float32)) * (HEAD_DIM ** -0.5)
    s = s + jnp.transpose(fq, (0, 2, 1))[:, :, :, None] - jnp.transpose(fk, (0, 2, 1))[:, :, None, :]
    mask = k_pos[None, :] <= q_pos[:, None]
    p = jax.nn.softmax(jnp.where(mask, s, -jnp.inf), axis=-1)
    return jnp.einsum('bhqk,bkhd->bqhd', p, v.astype(jnp.float32)).astype(v.dtype)


def to_blocks(a):
    b, t = a.shape[0], a.shape[1]
    return jnp.swapaxes(a.reshape((b, t // Q_BLOCK, Q_BLOCK) + a.shape[2:]), 0, 1)


def from_blocks(a):
    nb, b = a.shape[0], a.shape[1]
    return jnp.swapaxes(a, 0, 1).reshape((b, nb * Q_BLOCK) + a.shape[3:])


def sb_prompt(q, k, v):
    pos = jnp.arange(q.shape[1], dtype=jnp.int32)
    out = lax.map(lambda a: sb_attend(a[0], k, v, a[1], pos), (to_blocks(q), pos.reshape(-1, Q_BLOCK)))
    return from_blocks(out)


def fox_prompt(q, k, v, f):
    pos = jnp.arange(q.shape[1], dtype=jnp.int32)
    out = lax.map(lambda a: fox_attend(a[0], k, v, a[1], f, a[2], pos),
                  (to_blocks(q), to_blocks(f), pos.reshape(-1, Q_BLOCK)))
    return from_blocks(out)


def pre_mixer(x, shift, scale, g_mix, w_in, b_f):
    b, t, _ = x.shape
    h = rmsnorm(x, g_mix) * (1.0 + scale) + shift
    parts = jnp.split(h @ w_in, SPLITS, axis=-1)
    q_sb, k_sb, v_sb = [p.reshape(b, t, N_SB_HEADS, HEAD_DIM) for p in parts[0:3]]
    q_fx, k_fx, v_fx = [p.reshape(b, t, N_FOX_HEADS, HEAD_DIM) for p in parts[3:6]]
    logf = jax.nn.log_sigmoid((parts[6] + b_f).astype(jnp.float32))
    return q_sb, k_sb, v_sb, q_fx, k_fx, v_fx, logf


def post_mixer(x, o_sb, o_fx, ada, g_sb_out, g_fox_out, w_o, g_ffn, w_gate, w_up, w_down):
    b, t, _ = x.shape
    o = jnp.concatenate([rmsnorm(o_sb.reshape(b, t, SB_WIDTH), g_sb_out),
                         rmsnorm(o_fx.reshape(b, t, FOX_WIDTH), g_fox_out)], axis=-1)
    x = x + (1.0 + ada[2]) * (o @ w_o)
    h = rmsnorm(x, g_ffn) * (1.0 + ada[4]) + ada[3]
    f = (jax.nn.silu(h @ w_gate) * (h @ w_up)) @ w_down
    return x + (1.0 + ada[5]) * f


def setup_inputs(seed: int = 0) -> dict:
    key = jax.random.key(seed)
    ks = jax.random.split(key, 24)
    f32 = jnp.float32
    nrm = lambda k, shape, s=1.0: (jax.random.normal(k, shape, f32) * s)
    return {
        "x_prompt": nrm(ks[0], (BATCH, SEQ, D_MODEL)),
        "x_sample": nrm(ks[1], (DEC_BATCH, DEC_SEQ, D_MODEL)),
        "c_prompt": nrm(ks[2], (BATCH, D_MODEL)),
        "c_sample": nrm(ks[3], (DEC_BATCH, D_MODEL)),
        "cache_sb_k": nrm(ks[4], (DEPTH, DEC_BATCH, PAST_LEN, N_SB_HEADS, HEAD_DIM)),
        "cache_sb_v": nrm(ks[5], (DEPTH, DEC_BATCH, PAST_LEN, N_SB_HEADS, HEAD_DIM)),
        "cache_fox_k": nrm(ks[6], (DEPTH, DEC_BATCH, PAST_LEN, N_FOX_HEADS, HEAD_DIM)),
        "cache_fox_v": nrm(ks[7], (DEPTH, DEC_BATCH, PAST_LEN, N_FOX_HEADS, HEAD_DIM)),
        "cache_fox_logf": jax.nn.log_sigmoid(nrm(ks[8], (DEPTH, DEC_BATCH, PAST_LEN, N_FOX_HEADS)) + 3.0),
        "w_ada": nrm(ks[9], (DEPTH, D_MODEL, 6 * D_MODEL), 0.1 * D_MODEL ** -0.5),
        "b_ada": nrm(ks[10], (DEPTH, 6 * D_MODEL), 0.01),
        "g_mix": 1.0 + nrm(ks[11], (DEPTH, D_MODEL), 0.05),
        "w_in": nrm(ks[12], (DEPTH, D_MODEL, IN_COLS), D_MODEL ** -0.5),
        "b_f": 1.0 + 3.0 * jax.random.uniform(ks[13], (DEPTH, N_FOX_HEADS), f32),
        "g_sb_out": 1.0 + nrm(ks[14], (DEPTH, SB_WIDTH), 0.05),
        "g_fox_out": 1.0 + nrm(ks[15], (DEPTH, FOX_WIDTH), 0.05),
        "w_o": nrm(ks[16], (DEPTH, MIX_WIDTH, D_MODEL), MIX_WIDTH ** -0.5),
        "g_ffn": 1.0 + nrm(ks[17], (DEPTH, D_MODEL), 0.05),
        "w_gate": nrm(ks[18], (DEPTH, D_MODEL, D_FF), D_MODEL ** -0.5),
        "w_up": nrm(ks[19], (DEPTH, D_MODEL, D_FF), D_MODEL ** -0.5),
        "w_down": nrm(ks[20], (DEPTH, D_FF, D_MODEL), D_FF ** -0.5),
        "g_final": 1.0 + nrm(ks[21], (D_MODEL,), 0.05),
    }


def reference(x_prompt, x_sample, c_prompt, c_sample, cache_sb_k, cache_sb_v, cache_fox_k,
              cache_fox_v, cache_fox_logf, w_ada, b_ada, g_mix, w_in, b_f, g_sb_out, g_fox_out,
              w_o, g_ffn, w_gate, w_up, w_down, g_final):
    xp, xs = x_prompt, x_sample
    n_new = xs.shape[1]
    past = cache_sb_k.shape[2]
    q_pos_s = past + jnp.arange(n_new, dtype=jnp.int32)
    k_pos_s = jnp.arange(past + n_new, dtype=jnp.int32)
    sbk_p, sbv_p, fxk_p, fxv_p, lf_p = [], [], [], [], []
    sbk_s, sbv_s, fxk_s, fxv_s, lf_s = [], [], [], [], []
    for l in range(DEPTH):
        ada_p = ada_terms(c_prompt, w_ada[l], b_ada[l])
        q_sb, k_sb, v_sb, q_fx, k_fx, v_fx, logf = pre_mixer(xp, ada_p[0], ada_p[1], g_mix[l], w_in[l], b_f[l])
        o_sb = sb_prompt(q_sb, k_sb, v_sb)
        f_cum = jnp.cumsum(logf, axis=1)
        o_fx = fox_prompt(q_fx, k_fx, v_fx, f_cum)
        xp = post_mixer(xp, o_sb, o_fx, ada_p, g_sb_out[l], g_fox_out[l], w_o[l], g_ffn[l],
                        w_gate[l], w_up[l], w_down[l])
        sbk_p.append(k_sb); sbv_p.append(v_sb); fxk_p.append(k_fx); fxv_p.append(v_fx)
        lf_p.append(logf.astype(x_prompt.dtype))
        ada_s = ada_terms(c_sample, w_ada[l], b_ada[l])
        q_sb, k_sb, v_sb, q_fx, k_fx, v_fx, logf = pre_mixer(xs, ada_s[0], ada_s[1], g_mix[l], w_in[l], b_f[l])
        k_all = jnp.concatenate([cache_sb_k[l], k_sb], axis=1)
        v_all = jnp.concatenate([cache_sb_v[l], v_sb], axis=1)
        o_sb = sb_attend(q_sb, k_all, v_all, q_pos_s, k_pos_s)
        f_all = jnp.cumsum(jnp.concatenate([cache_fox_logf[l].astype(jnp.float32), logf], axis=1), axis=1)
        kf_all = jnp.concatenate([cache_fox_k[l], k_fx], axis=1)
        vf_all = jnp.concatenate([cache_fox_v[l], v_fx], axis=1)
        o_fx = fox_attend(q_fx, kf_all, vf_all, f_all[:, past:], f_all, q_pos_s, k_pos_s)
        xs = post_mixer(xs, o_sb, o_fx, ada_s, g_sb_out[l], g_fox_out[l], w_o[l], g_ffn[l],
                        w_gate[l], w_up[l], w_down[l])
        sbk_s.append(k_sb); sbv_s.append(v_sb); fxk_s.append(k_fx); fxv_s.append(v_fx)
        lf_s.append(logf.astype(x_sample.dtype))
    y_prompt = rmsnorm(xp, g_final)
    y_sample = rmsnorm(xs, g_final)
    return (y_prompt, y_sample,
            jnp.stack(sbk_p), jnp.stack(sbv_p), jnp.stack(fxk_p), jnp.stack(fxv_p), jnp.stack(lf_p),
            jnp.stack(sbk_s), jnp.stack(sbv_s), jnp.stack(fxk_s), jnp.stack(fxv_s), jnp.stack(lf_s))
```

```python
import functools

import jax
import jax.numpy as jnp
from jax import lax
from jax.experimental import pallas as pl
from jax.experimental.pallas import tpu as pltpu

F32 = jnp.float32
BF16 = jnp.bfloat16
EPS = 1e-6
HEAD_DIM = 64
LANES = 128
ROW_BLOCK = 512
ATT_BLOCK = 256
NEG_BIG = -1e30
VMEM_LIMIT = 56 * 1024 * 1024


def _params(*semantics):
    return pltpu.CompilerParams(dimension_semantics=semantics, vmem_limit_bytes=VMEM_LIMIT)


def _dot(a, b):
    return jnp.dot(a, b, preferred_element_type=F32)


def _dot_nt(a, b):
    return lax.dot_general(a, b, (((1,), (1,)), ((), ())), preferred_element_type=F32)


def _rms(x, g):
    return x * lax.rsqrt(jnp.mean(x * x, axis=-1, keepdims=True) + EPS) * g


def _split_bf16(x, parts):
    out = []
    for _ in range(parts - 1):
        p = x.astype(BF16)
        out.append(p)
        x = x - p.astype(F32)
    out.append(x.astype(BF16))
    return out


def _dot_split(x, m, parts):
    acc = None
    for p in _split_bf16(x, parts):
        d = _dot(p, m)
        acc = d if acc is None else acc + d
    return acc


def _keep_head(x, first):
    lane = lax.broadcasted_iota(jnp.int32, x.shape, x.ndim - 1)
    keep = (lane < HEAD_DIM) if first else (lane >= HEAD_DIM)
    return jnp.where(keep, x, jnp.zeros_like(x))


def _per_head(col0, col1, shape):
    lane = lax.broadcasted_iota(jnp.int32, shape, 1)
    return jnp.where(lane < HEAD_DIM, col0, col1)


def _lanes_to(x, n):
    if n <= LANES:
        return x[:, :n]
    return jnp.concatenate([x] * (n // LANES), axis=1)


def _ada_kernel(c_ref, w_ref, b_ref, o_ref):
    c = c_ref[...]
    s = (c * (1.0 / (1.0 + jnp.exp(-c)))).astype(BF16)
    o_ref[...] = _dot(s, w_ref[...]) + b_ref[...]


def _ada(c, w16, b):
    rows, d = c.shape
    n = w16.shape[1]
    tn = 1024
    return pl.pallas_call(
        _ada_kernel,
        grid=(n // tn,),
        in_specs=[pl.BlockSpec((rows, d), lambda j: (0, 0)),
                  pl.BlockSpec((d, tn), lambda j: (0, j)),
                  pl.BlockSpec((1, tn), lambda j: (0, j))],
        out_specs=pl.BlockSpec((rows, tn), lambda j: (0, j)),
        out_shape=jax.ShapeDtypeStruct((rows, n), F32),
        compiler_params=_params("arbitrary"),
        name="ada",
    )(c, w16, b)


def _pre_kernel(x_ref, ada_ref, g_ref, w_ref, wf_ref, bf_ref,
                qsb_ref, ksb_ref, vsb_ref, ksb16_ref, vsb16_ref,
                qfx_ref, kfx_ref, vfx_ref, kfx16_ref, vfx16_ref, lf_ref, *, sb_w, fx_w, n_f):
    x = x_ref[...]
    nb, tt, d = x.shape
    ada = ada_ref[...]
    h = _rms(x, g_ref[...]) * (1.0 + ada[:, 1:2, :]) + ada[:, 0:1, :]
    h = h.reshape(nb * tt, d).astype(BF16)
    q_scale = HEAD_DIM ** -0.5

    def proj(lo, width):
        return _dot(h, w_ref[:, lo:lo + width]).reshape(nb, tt, width)

    qsb_ref[...] = (proj(0, sb_w) * q_scale).astype(BF16)
    k = proj(sb_w, sb_w)
    ksb_ref[...] = k
    ksb16_ref[...] = k.astype(BF16)
    v = proj(2 * sb_w, sb_w)
    vsb_ref[...] = v
    vsb16_ref[...] = v.astype(BF16)
    base = 3 * sb_w
    qfx_ref[...] = (proj(base, fx_w) * q_scale).astype(BF16)
    k = proj(base + fx_w, fx_w)
    kfx_ref[...] = k
    kfx16_ref[...] = k.astype(BF16)
    v = proj(base + 2 * fx_w, fx_w)
    vfx_ref[...] = v
    vfx16_ref[...] = v.astype(BF16)
    u = _dot(h, wf_ref[...]) + bf_ref[...]
    lf = jnp.minimum(u, 0.0) - jnp.log(1.0 + jnp.exp(-jnp.abs(u)))
    lf_ref[...] = lf[:, :n_f].reshape(nb, tt, n_f)


def _pre(x, ada, g, w16, wf16, bfp, sb_w, fx_w, n_f):
    b, t, d = x.shape
    tt = min(t, ROW_BLOCK)
    nb = max(1, ROW_BLOCK // tt)
    grid = (b // nb, t // tt)
    tok = lambda w: pl.BlockSpec((nb, tt, w), lambda i, j: (i, j, 0))
    const = lambda a: pl.BlockSpec(a.shape, lambda i, j: (0, 0))
    shape = lambda w, dt: jax.ShapeDtypeStruct((b, t, w), dt)
    return pl.pallas_call(
        functools.partial(_pre_kernel, sb_w=sb_w, fx_w=fx_w, n_f=n_f),
        grid=grid,
        in_specs=[tok(d), pl.BlockSpec((nb, 6, d), lambda i, j: (i, 0, 0)),
                  const(g), const(w16), const(wf16), const(bfp)],
        out_specs=[tok(sb_w)] * 5 + [tok(fx_w)] * 5 + [tok(n_f)],
        out_shape=[shape(sb_w, BF16), shape(sb_w, F32), shape(sb_w, F32), shape(sb_w, BF16),
                   shape(sb_w, BF16),
                   shape(fx_w, BF16), shape(fx_w, F32), shape(fx_w, F32), shape(fx_w, BF16),
                   shape(fx_w, BF16),
                   shape(n_f, F32)],
        compiler_params=_params("arbitrary", "arbitrary"),
        name="pre_mixer",
    )(x, ada, g, w16, wf16, bfp)


def _cumsum_kernel(x_ref, o_ref, carry_ref):
    @pl.when(pl.program_id(0) == 0)
    def _():
        carry_ref[...] = jnp.zeros_like(carry_ref)

    x = x_ref[...]
    tc = x.shape[1]
    r = lax.broadcasted_iota(jnp.int32, (tc, tc), 0)
    c = lax.broadcasted_iota(jnp.int32, (tc, tc), 1)
    upper = jnp.where(r <= c, 1.0, 0.0).astype(BF16)
    carry = carry_ref[...]
    o_ref[...] = _dot_split(x, upper, 3) + _lanes_to(carry, tc)
    carry_ref[...] = carry + _dot_split(x, jnp.ones((tc, LANES), BF16), 3)


def _cumsum_rows(x, tc):
    rows, t = x.shape
    return pl.pallas_call(
        _cumsum_kernel,
        grid=(t // tc,),
        in_specs=[pl.BlockSpec((rows, tc), lambda j: (0, j))],
        out_specs=pl.BlockSpec((rows, tc), lambda j: (0, j)),
        out_shape=jax.ShapeDtypeStruct((rows, t), F32),
        scratch_shapes=[pltpu.VMEM((rows, LANES), F32)],
        compiler_params=_params("arbitrary"),
        name="cumsum_time",
    )(x)


def _tri_strict(n):
    r = lax.broadcasted_iota(jnp.int32, (n, n), 0)
    c = lax.broadcasted_iota(jnp.int32, (n, n), 1)
    return jnp.where(r > c, 1.0, 0.0).astype(BF16)


def _causal(tq, tk, strict):
    r = lax.broadcasted_iota(jnp.int32, (tq, tk), 0)
    c = lax.broadcasted_iota(jnp.int32, (tq, tk), 1)
    return (c < r) if strict else (c <= r)


def _sb_tile(qh, k, vh, carry, mask):
    tk = k.shape[0]
    z = _dot_nt(qh, k)
    log_beta = jnp.minimum(z, 0.0) - jnp.log(1.0 + jnp.exp(-jnp.abs(z)))
    log_keep = log_beta - z
    if mask is not None:
        log_keep = jnp.where(mask, log_keep, 0.0)
    parts = _split_bf16(log_keep, 2)
    tri = _tri_strict(tk)
    ones = jnp.ones((tk, LANES), BF16)
    between = _dot(parts[0], tri) + _dot(parts[1], tri)
    total = _dot(parts[0], ones) + _dot(parts[1], ones)
    a = jnp.exp(log_beta + between + _lanes_to(carry, tk))
    if mask is not None:
        a = jnp.where(mask, a, 0.0)
    return _dot(a.astype(BF16), vh), carry + total


def _fox_tile(qh, k, vh, fk_row, fq_col, m, l, mask):
    u = _dot_nt(qh, k) - fk_row
    if mask is not None:
        u = jnp.where(mask, u, NEG_BIG)
    m_new = jnp.maximum(m, jnp.max(u, axis=1, keepdims=True) + fq_col)
    alpha = jnp.exp(m - m_new)
    p = jnp.exp(u - (m_new - fq_col))
    l_new = alpha * l + jnp.sum(p, axis=1, keepdims=True)
    return _dot(p.astype(BF16), vh), m_new, l_new, alpha


def _head_column(f, h):
    lane = lax.broadcasted_iota(jnp.int32, f.shape, 1)
    return jnp.sum(jnp.where(lane == h, f, 0.0), axis=1, keepdims=True)


def _sb_prompt_kernel(q_ref, k_ref, v_ref, o_ref, *, tile):
    qi = pl.program_id(2)
    q = q_ref[0]
    q0, q1 = _keep_head(q, True), _keep_head(q, False)

    def pair(kb, c0, c1, acc, mask):
        start = pl.multiple_of(kb * tile, tile)
        k = k_ref[0, pl.ds(start, tile), :]
        v = v_ref[0, pl.ds(start, tile), :]
        pv0, c0 = _sb_tile(q0, k, _keep_head(v, True), c0, mask)
        pv1, c1 = _sb_tile(q1, k, _keep_head(v, False), c1, mask)
        return c0, c1, acc + pv0 + pv1

    zero = jnp.zeros((tile, LANES), F32)
    state = pair(qi, zero, zero, zero, _causal(tile, tile, True))
    state = lax.fori_loop(0, qi, lambda i, s: pair(qi - 1 - i, s[0], s[1], s[2], None), state)
    o_ref[0] = state[2]


def _fox_prompt_kernel(q_ref, k_ref, v_ref, fq_ref, fk_ref, o_ref, *, tile):
    hp = pl.program_id(1)
    qi = pl.program_id(2)
    q = q_ref[0]
    q0, q1 = _keep_head(q, True), _keep_head(q, False)
    fq = fq_ref[0]
    fq0, fq1 = _head_column(fq, 2 * hp), _head_column(fq, 2 * hp + 1)
    shape = (tile, LANES)

    def pair(kb, m0, l0, m1, l1, acc, mask):
        start = pl.multiple_of(kb * tile, tile)
        k = k_ref[0, pl.ds(start, tile), :]
        v = v_ref[0, pl.ds(start, tile), :]
        fk = fk_ref[0, 0, kb]
        pv0, m0, l0, a0 = _fox_tile(q0, k, _keep_head(v, True), fk[0:1, :], fq0, m0, l0, mask)
        pv1, m1, l1, a1 = _fox_tile(q1, k, _keep_head(v, False), fk[1:2, :], fq1, m1, l1, mask)
        return m0, l0, m1, l1, acc * _per_head(a0, a1, shape) + pv0 + pv1

    neg = jnp.full((tile, 1), NEG_BIG, F32)
    zero = jnp.zeros((tile, 1), F32)
    state = pair(qi, neg, zero, neg, zero, jnp.zeros(shape, F32), _causal(tile, tile, False))
    state = lax.fori_loop(0, qi, lambda i, s: pair(qi - 1 - i, *s, None), state)
    _, l0, _, l1, acc = state
    o_ref[0] = acc / _per_head(l0, l1, shape)


def _prompt_specs(t, tile):
    qspec = pl.BlockSpec((1, tile, LANES), lambda b, h, i: (b, i, h))
    kvspec = pl.BlockSpec((1, t, LANES), lambda b, h, i: (b, 0, h))
    return qspec, kvspec


def _sb_prompt(q, k, v):
    b, t, w = q.shape
    tile = ATT_BLOCK
    qspec, kvspec = _prompt_specs(t, tile)
    return pl.pallas_call(
        functools.partial(_sb_prompt_kernel, tile=tile),
        grid=(b, w // LANES, t // tile),
        in_specs=[qspec, kvspec, kvspec],
        out_specs=qspec,
        out_shape=jax.ShapeDtypeStruct((b, t, w), F32),
        compiler_params=_params("arbitrary", "arbitrary", "arbitrary"),
        name="sb_prompt",
    )(q, k, v)


def _fox_prompt(q, k, v, fq, fk):
    b, t, w = q.shape
    tile = ATT_BLOCK
    qspec, kvspec = _prompt_specs(t, tile)
    n_f = fq.shape[-1]
    return pl.pallas_call(
        functools.partial(_fox_prompt_kernel, tile=tile),
        grid=(b, w // LANES, t // tile),
        in_specs=[qspec, kvspec, kvspec,
                  pl.BlockSpec((1, tile, n_f), lambda b, h, i: (b, i, 0)),
                  pl.BlockSpec((1, 1, t // tile, 2, tile), lambda b, h, i: (b, h, 0, 0, 0))],
        out_specs=qspec,
        out_shape=jax.ShapeDtypeStruct((b, t, w), F32),
        compiler_params=_params("arbitrary", "arbitrary", "arbitrary"),
        name="fox_prompt",
    )(q, k, v, fq, fk)


def _cache_tiles(past, tile):
    return [(s, min(tile, past - s)) for s in range(0, past, tile)][::-1]


def _sb_sample_kernel(q_ref, kn_ref, vn_ref, kc_ref, vc_ref, o_ref, *, tile):
    q = q_ref[0]
    n = q.shape[0]
    past = kc_ref.shape[1]
    q0, q1 = _keep_head(q, True), _keep_head(q, False)
    c0 = c1 = acc = jnp.zeros((n, LANES), F32)
    tiles = [(None, n)] + _cache_tiles(past, tile)
    for start, size in tiles:
        if start is None:
            k, v, mask = kn_ref[0], vn_ref[0], _causal(n, n, True)
        else:
            k = kc_ref[0, start:start + size, :].astype(BF16)
            v = vc_ref[0, start:start + size, :].astype(BF16)
            mask = None
        pv0, c0 = _sb_tile(q0, k, _keep_head(v, True), c0, mask)
        pv1, c1 = _sb_tile(q1, k, _keep_head(v, False), c1, mask)
        acc = acc + pv0 + pv1
    o_ref[0] = acc


def _fox_sample_kernel(q_ref, kn_ref, vn_ref, kc_ref, vc_ref, fq_ref, fk_ref, o_ref, *, tile):
    hp = pl.program_id(1)
    q = q_ref[0]
    n = q.shape[0]
    past = kc_ref.shape[1]
    q0, q1 = _keep_head(q, True), _keep_head(q, False)
    fq = fq_ref[0]
    fq0, fq1 = _head_column(fq, 2 * hp), _head_column(fq, 2 * hp + 1)
    shape = (n, LANES)
    m0 = m1 = jnp.full((n, 1), NEG_BIG, F32)
    l0 = l1 = jnp.zeros((n, 1), F32)
    acc = jnp.zeros(shape, F32)
    tiles = [(None, n)] + _cache_tiles(past, tile)
    for start, size in tiles:
        if start is None:
            k, v, mask, f0 = kn_ref[0], vn_ref[0], _causal(n, n, False), past
        else:
            k = kc_ref[0, start:start + size, :].astype(BF16)
            v = vc_ref[0, start:start + size, :].astype(BF16)
            mask, f0 = None, start
        fk = fk_ref[0, 0, :, f0:f0 + size]
        pv0, m0, l0, a0 = _fox_tile(q0, k, _keep_head(v, True), fk[0:1, :], fq0, m0, l0, mask)
        pv1, m1, l1, a1 = _fox_tile(q1, k, _keep_head(v, False), fk[1:2, :], fq1, m1, l1, mask)
        acc = acc * _per_head(a0, a1, shape) + pv0 + pv1
    o_ref[0] = acc / _per_head(l0, l1, shape)


def _sample_specs(n, past):
    new = pl.BlockSpec((1, n, LANES), lambda b, h: (b, 0, h))
    cache = pl.BlockSpec((1, past, LANES), lambda b, h: (b, 0, h))
    return new, cache


def _sb_sample(q, kn, vn, kc, vc):
    b, n, w = q.shape
    new, cache = _sample_specs(n, kc.shape[1])
    return pl.pallas_call(
        functools.partial(_sb_sample_kernel, tile=ATT_BLOCK),
        grid=(b, w // LANES),
        in_specs=[new, new, new, cache, cache],
        out_specs=new,
        out_shape=jax.ShapeDtypeStruct((b, n, w), F32),
        compiler_params=_params("arbitrary", "arbitrary"),
        name="sb_sample",
    )(q, kn, vn, kc, vc)


def _fox_sample(q, kn, vn, kc, vc, fq, fk):
    b, n, w = q.shape
    new, cache = _sample_specs(n, kc.shape[1])
    return pl.pallas_call(
        functools.partial(_fox_sample_kernel, tile=ATT_BLOCK),
        grid=(b, w // LANES),
        in_specs=[new, new, new, cache, cache,
                  pl.BlockSpec((1, n, fq.shape[-1]), lambda b, h: (b, 0, 0)),
                  pl.BlockSpec((1, 1, 2, fk.shape[-1]), lambda b, h: (b, h, 0, 0))],
        out_specs=new,
        out_shape=jax.ShapeDtypeStruct((b, n, w), F32),
        compiler_params=_params("arbitrary", "arbitrary"),
        name="fox_sample",
    )(q, kn, vn, kc, vc, fq, fk)


def _post_kernel(x_ref, osb_ref, ofx_ref, ada_ref, gsb_ref, gfx_ref, wo_ref, gffn_ref,
                 x2_ref, h_ref):
    x = x_ref[...]
    nb, tt, d = x.shape
    rows = nb * tt
    ada = ada_ref[...]
    o_sb = _rms(osb_ref[...], gsb_ref[...])
    o_fx = _rms(ofx_ref[...], gfx_ref[...])
    sb_w = o_sb.shape[-1]
    fx_w = o_fx.shape[-1]
    proj = (_dot(o_sb.reshape(rows, sb_w).astype(BF16), wo_ref[0:sb_w, :])
            + _dot(o_fx.reshape(rows, fx_w).astype(BF16), wo_ref[sb_w:sb_w + fx_w, :]))
    x2 = x + (1.0 + ada[:, 2:3, :]) * proj.reshape(nb, tt, d)
    x2_ref[...] = x2
    h = _rms(x2, gffn_ref[...]) * (1.0 + ada[:, 4:5, :]) + ada[:, 3:4, :]
    h_ref[...] = h.astype(BF16)


def _ffn_kernel(h_ref, x2_ref, ada_ref, wg_ref, wu_ref, wd_ref, gfin_ref, y_ref, acc_ref, *, final_norm):
    j = pl.program_id(2)
    nb, tt, d = h_ref.shape
    h = h_ref[...].reshape(nb * tt, d)
    g = _dot(h, wg_ref[...])
    u = _dot(h, wu_ref[...])
    act = (g * (1.0 / (1.0 + jnp.exp(-g))) * u).astype(BF16)
    part = _dot(act, wd_ref[...])

    @pl.when(j == 0)
    def _():
        acc_ref[...] = part

    @pl.when(j > 0)
    def _():
        acc_ref[...] += part

    @pl.when(j == pl.num_programs(2) - 1)
    def _():
        ada = ada_ref[...]
        x3 = x2_ref[...] + (1.0 + ada[:, 5:6, :]) * acc_ref[...].reshape(nb, tt, d)
        y_ref[...] = _rms(x3, gfin_ref[...]) if final_norm else x3


def _post(x, o_sb, o_fx, ada, g_sb, g_fx, wo16, g_ffn, wg16, wu16, wd16, g_fin, final_norm):
    b, t, d = x.shape
    tt = min(t, ROW_BLOCK)
    nb = max(1, ROW_BLOCK // tt)
    tok = lambda w: pl.BlockSpec((nb, tt, w), lambda i, j: (i, j, 0))
    const = lambda a: pl.BlockSpec(a.shape, lambda i, j: (0, 0))
    x2, h = pl.pallas_call(
        _post_kernel,
        grid=(b // nb, t // tt),
        in_specs=[tok(d), tok(o_sb.shape[-1]), tok(o_fx.shape[-1]),
                  pl.BlockSpec((nb, 6, d), lambda i, j: (i, 0, 0)),
                  const(g_sb), const(g_fx), const(wo16), const(g_ffn)],
        out_specs=[tok(d), tok(d)],
        out_shape=[jax.ShapeDtypeStruct((b, t, d), F32), jax.ShapeDtypeStruct((b, t, d), BF16)],
        compiler_params=_params("arbitrary", "arbitrary"),
        name="post_mixer",
    )(x, o_sb, o_fx, ada, g_sb, g_fx, wo16, g_ffn)

    d_ff = wg16.shape[1]
    n_ff = 2 if (d_ff // 2) % LANES == 0 else 1
    tf = d_ff // n_ff
    tok3 = lambda w: pl.BlockSpec((nb, tt, w), lambda i, j, f: (i, j, 0))
    return pl.pallas_call(
        functools.partial(_ffn_kernel, final_norm=final_norm),
        grid=(b // nb, t // tt, n_ff),
        in_specs=[tok3(d), tok3(d), pl.BlockSpec((nb, 6, d), lambda i, j, f: (i, 0, 0)),
                  pl.BlockSpec((d, tf), lambda i, j, f: (0, f)),
                  pl.BlockSpec((d, tf), lambda i, j, f: (0, f)),
                  pl.BlockSpec((tf, d), lambda i, j, f: (f, 0)),
                  pl.BlockSpec(g_fin.shape, lambda i, j, f: (0, 0))],
        out_specs=tok3(d),
        out_shape=jax.ShapeDtypeStruct((b, t, d), F32),
        scratch_shapes=[pltpu.VMEM((nb * tt, d), F32)],
        compiler_params=_params("arbitrary", "arbitrary", "arbitrary"),
        name="ffn",
    )(h, x2, ada, wg16, wu16, wd16, g_fin)


def _pad_lanes(a, n):
    return jnp.pad(a, ((0, 0), (0, n - a.shape[1])))


def kernel(x_prompt, x_sample, c_prompt, c_sample, cache_sb_k, cache_sb_v, cache_fox_k, cache_fox_v, cache_fox_logf, w_ada, b_ada, g_mix, w_in, b_f, g_sb_out, g_fox_out, w_o, g_ffn, w_gate, w_up, w_down, g_final):
    depth = w_ada.shape[0]
    bp, tp, d = x_prompt.shape
    bs, ts, _ = x_sample.shape
    past, n_sb, hd = cache_sb_k.shape[2:]
    n_fx = cache_fox_k.shape[3]
    assert hd == HEAD_DIM and n_sb % 2 == 0 and n_fx % 2 == 0
    assert tp % ROW_BLOCK == 0 and tp % ATT_BLOCK == 0 and ROW_BLOCK % ts == 0 and bs % (ROW_BLOCK // ts) == 0
    sb_w, fx_w = n_sb * hd, n_fx * hd
    qkv_cols = 3 * sb_w + 3 * fx_w
    att = ATT_BLOCK
    past_pad = -(-(past + ts) // (3 * LANES)) * (3 * LANES)
    row = lambda a: a.reshape(1, -1)

    xp, xs = x_prompt, x_sample
    outs = [[] for _ in range(10)]
    for l in range(depth):
        w16 = w_in[l][:, :qkv_cols].astype(BF16)
        wf16 = _pad_lanes(w_in[l][:, qkv_cols:], LANES).astype(BF16)
        bfp = _pad_lanes(row(b_f[l]), LANES)
        wo16, wg16 = w_o[l].astype(BF16), w_gate[l].astype(BF16)
        wu16, wd16 = w_up[l].astype(BF16), w_down[l].astype(BF16)

        c_all = jnp.concatenate([c_prompt, c_sample], axis=0)
        ada = _ada(c_all, w_ada[l].astype(BF16), row(b_ada[l])).reshape(bp + bs, 6, d)
        ada_p, ada_s = ada[:bp], ada[bp:]

        (q_sb, k_sb, v_sb, k_sb16, v_sb16, q_fx, k_fx, v_fx, k_fx16, v_fx16, lf) = _pre(
            xp, ada_p, row(g_mix[l]), w16, wf16, bfp, sb_w, fx_w, n_fx)
        o_sb = _sb_prompt(q_sb, k_sb16, v_sb16)
        f_cum = _cumsum_rows(jnp.swapaxes(lf, 1, 2).reshape(bp * n_fx, tp), ROW_BLOCK)
        fq = jnp.swapaxes(f_cum.reshape(bp, n_fx, tp), 1, 2)
        fk = jnp.swapaxes(f_cum.reshape(bp, n_fx // 2, 2, tp // att, att), 2, 3)
        o_fx = _fox_prompt(q_fx, k_fx16, v_fx16, fq, fk)
        xp = _post(xp, o_sb, o_fx, ada_p, row(g_sb_out[l]), row(g_fox_out[l]), wo16, row(g_ffn[l]),
                   wg16, wu16, wd16, row(g_final), l + 1 == depth)
        for dst, a in zip(outs[0:4], (k_sb, v_sb, k_fx, v_fx)):
            dst.append(a.reshape(bp, tp, -1, hd))
        outs[4].append(lf)

        (q_sb, k_sb, v_sb, k_sb16, v_sb16, q_fx, k_fx, v_fx, k_fx16, v_fx16, lf) = _pre(
            xs, ada_s, row(g_mix[l]), w16, wf16, bfp, sb_w, fx_w, n_fx)
        o_sb = _sb_sample(q_sb, k_sb16, v_sb16,
                          cache_sb_k[l].reshape(bs, past, sb_w), cache_sb_v[l].reshape(bs, past, sb_w))
        lf_all = jnp.concatenate([cache_fox_logf[l].astype(F32), lf], axis=1)
        lf_rows = _pad_lanes(jnp.swapaxes(lf_all, 1, 2).reshape(bs * n_fx, past + ts), past_pad)
        f_all = _cumsum_rows(lf_rows, past_pad // 3)
        fq = jnp.swapaxes(f_all.reshape(bs, n_fx, past_pad)[:, :, past:past + ts], 1, 2)
        fk = f_all.reshape(bs, n_fx // 2, 2, past_pad)
        o_fx = _fox_sample(q_fx, k_fx16, v_fx16,
                           cache_fox_k[l].reshape(bs, past, fx_w), cache_fox_v[l].reshape(bs, past, fx_w),
                           fq, fk)
        xs = _post(xs, o_sb, o_fx, ada_s, row(g_sb_out[l]), row(g_fox_out[l]), wo16, row(g_ffn[l]),
                   wg16, wu16, wd16, row(g_final), l + 1 == depth)
        for dst, a in zip(outs[5:9], (k_sb, v_sb, k_fx, v_fx)):
            dst.append(a.reshape(bs, ts, -1, hd))
        outs[9].append(lf)

    return (xp, xs) + tuple(jnp.stack(o) for o in outs)
```

```python
import functools

import jax
import jax.numpy as jnp
from jax import lax
from jax.experimental import pallas as pl
from jax.experimental.pallas import tpu as pltpu

F32 = jnp.float32
BF16 = jnp.bfloat16
EPS = 1e-6
HEAD_DIM = 64
LANES = 128
ROW_BLOCK = 512
ATT_BLOCK = 256
NEG_BIG = -1e30
EXP_ZERO = -110.0
NORM_SLACK = 1.0 + 2.0 ** -8
VMEM_LIMIT = 56 * 1024 * 1024


def _params(*semantics):
    return pltpu.CompilerParams(dimension_semantics=semantics, vmem_limit_bytes=VMEM_LIMIT)


def _dot(a, b):
    return jnp.dot(a, b, preferred_element_type=F32)


def _dot_nt(a, b):
    return lax.dot_general(a, b, (((1,), (1,)), ((), ())), preferred_element_type=F32)


def _rms(x, g):
    return x * lax.rsqrt(jnp.mean(x * x, axis=-1, keepdims=True) + EPS) * g


def _split_bf16(x, parts):
    out = []
    for _ in range(parts - 1):
        p = x.astype(BF16)
        out.append(p)
        x = x - p.astype(F32)
    out.append(x.astype(BF16))
    return out


def _dot_split(x, m, parts):
    acc = None
    for p in _split_bf16(x, parts):
        d = _dot(p, m)
        acc = d if acc is None else acc + d
    return acc


def _keep_head(x, first):
    lane = lax.broadcasted_iota(jnp.int32, x.shape, x.ndim - 1)
    keep = (lane < HEAD_DIM) if first else (lane >= HEAD_DIM)
    return jnp.where(keep, x, jnp.zeros_like(x))


def _per_head(col0, col1, shape):
    lane = lax.broadcasted_iota(jnp.int32, shape, 1)
    return jnp.where(lane < HEAD_DIM, col0, col1)


def _lanes_to(x, n):
    if n <= LANES:
        return x[:, :n]
    return jnp.concatenate([x] * (n // LANES), axis=1)


def _ada_kernel(c_ref, w_ref, b_ref, o_ref):
    c = c_ref[...]
    s = (c * (1.0 / (1.0 + jnp.exp(-c)))).astype(BF16)
    o_ref[...] = _dot(s, w_ref[...]) + b_ref[...]


def _ada(c, w16, b):
    rows, d = c.shape
    n = w16.shape[1]
    tn = 1024
    return pl.pallas_call(
        _ada_kernel,
        grid=(n // tn,),
        in_specs=[pl.BlockSpec((rows, d), lambda j: (0, 0)),
                  pl.BlockSpec((d, tn), lambda j: (0, j)),
                  pl.BlockSpec((1, tn), lambda j: (0, j))],
        out_specs=pl.BlockSpec((rows, tn), lambda j: (0, j)),
        out_shape=jax.ShapeDtypeStruct((rows, n), F32),
        compiler_params=_params("arbitrary"),
        name="ada",
    )(c, w16, b)


def _pre_kernel(x_ref, ada_ref, g_ref, w_ref, wf_ref, bf_ref,
                qsb_ref, ksb_ref, vsb_ref, ksb16_ref, vsb16_ref,
                qfx_ref, kfx_ref, vfx_ref, kfx16_ref, vfx16_ref, lf_ref, *, sb_w, fx_w, n_f):
    x = x_ref[...]
    nb, tt, d = x.shape
    ada = ada_ref[...]
    h = _rms(x, g_ref[...]) * (1.0 + ada[:, 1:2, :]) + ada[:, 0:1, :]
    h = h.reshape(nb * tt, d).astype(BF16)
    q_scale = HEAD_DIM ** -0.5

    def proj(lo, width):
        return _dot(h, w_ref[:, lo:lo + width]).reshape(nb, tt, width)

    qsb_ref[...] = (proj(0, sb_w) * q_scale).astype(BF16)
    k = proj(sb_w, sb_w)
    ksb_ref[...] = k
    ksb16_ref[...] = k.astype(BF16)
    v = proj(2 * sb_w, sb_w)
    vsb_ref[...] = v
    vsb16_ref[...] = v.astype(BF16)
    base = 3 * sb_w
    qfx_ref[...] = (proj(base, fx_w) * q_scale).astype(BF16)
    k = proj(base + fx_w, fx_w)
    kfx_ref[...] = k
    kfx16_ref[...] = k.astype(BF16)
    v = proj(base + 2 * fx_w, fx_w)
    vfx_ref[...] = v
    vfx16_ref[...] = v.astype(BF16)
    u = _dot(h, wf_ref[...]) + bf_ref[...]
    lf = jnp.minimum(u, 0.0) - jnp.log(1.0 + jnp.exp(-jnp.abs(u)))
    lf_ref[...] = lf[:, :n_f].reshape(nb, tt, n_f)


def _pre(x, ada, g, w16, wf16, bfp, sb_w, fx_w, n_f):
    b, t, d = x.shape
    tt = min(t, ROW_BLOCK)
    nb = max(1, ROW_BLOCK // tt)
    grid = (b // nb, t // tt)
    tok = lambda w: pl.BlockSpec((nb, tt, w), lambda i, j: (i, j, 0))
    const = lambda a: pl.BlockSpec(a.shape, lambda i, j: (0, 0))
    shape = lambda w, dt: jax.ShapeDtypeStruct((b, t, w), dt)
    return pl.pallas_call(
        functools.partial(_pre_kernel, sb_w=sb_w, fx_w=fx_w, n_f=n_f),
        grid=grid,
        in_specs=[tok(d), pl.BlockSpec((nb, 6, d), lambda i, j: (i, 0, 0)),
                  const(g), const(w16), const(wf16), const(bfp)],
        out_specs=[tok(sb_w)] * 5 + [tok(fx_w)] * 5 + [tok(n_f)],
        out_shape=[shape(sb_w, BF16), shape(sb_w, F32), shape(sb_w, F32), shape(sb_w, BF16),
                   shape(sb_w, BF16),
                   shape(fx_w, BF16), shape(fx_w, F32), shape(fx_w, F32), shape(fx_w, BF16),
                   shape(fx_w, BF16),
                   shape(n_f, F32)],
        compiler_params=_params("arbitrary", "arbitrary"),
        name="pre_mixer",
    )(x, ada, g, w16, wf16, bfp)


def _cumsum_kernel(x_ref, o_ref, carry_ref):
    @pl.when(pl.program_id(0) == 0)
    def _():
        carry_ref[...] = jnp.zeros_like(carry_ref)

    x = x_ref[...]
    tc = x.shape[1]
    r = lax.broadcasted_iota(jnp.int32, (tc, tc), 0)
    c = lax.broadcasted_iota(jnp.int32, (tc, tc), 1)
    upper = jnp.where(r <= c, 1.0, 0.0).astype(BF16)
    carry = carry_ref[...]
    o_ref[...] = _dot_split(x, upper, 3) + _lanes_to(carry, tc)
    carry_ref[...] = carry + _dot_split(x, jnp.ones((tc, LANES), BF16), 3)


def _cumsum_rows(x, tc):
    rows, t = x.shape
    return pl.pallas_call(
        _cumsum_kernel,
        grid=(t // tc,),
        in_specs=[pl.BlockSpec((rows, tc), lambda j: (0, j))],
        out_specs=pl.BlockSpec((rows, tc), lambda j: (0, j)),
        out_shape=jax.ShapeDtypeStruct((rows, t), F32),
        scratch_shapes=[pltpu.VMEM((rows, LANES), F32)],
        compiler_params=_params("arbitrary"),
        name="cumsum_time",
    )(x)


def _tri_strict(n):
    r = lax.broadcasted_iota(jnp.int32, (n, n), 0)
    c = lax.broadcasted_iota(jnp.int32, (n, n), 1)
    return jnp.where(r > c, 1.0, 0.0).astype(BF16)


def _causal(tq, tk, strict):
    r = lax.broadcasted_iota(jnp.int32, (tq, tk), 0)
    c = lax.broadcasted_iota(jnp.int32, (tq, tk), 1)
    return (c < r) if strict else (c <= r)


def _sb_tile(qh, k, vh, carry, mask):
    tk = k.shape[0]
    z = _dot_nt(qh, k)
    log_beta = jnp.minimum(z, 0.0) - jnp.log(1.0 + jnp.exp(-jnp.abs(z)))
    log_keep = log_beta - z
    if mask is not None:
        log_keep = jnp.where(mask, log_keep, 0.0)
    parts = _split_bf16(log_keep, 2)
    tri = _tri_strict(tk)
    ones = jnp.ones((tk, LANES), BF16)
    between = _dot(parts[0], tri) + _dot(parts[1], tri)
    total = _dot(parts[0], ones) + _dot(parts[1], ones)
    a = jnp.exp(log_beta + between + _lanes_to(carry, tk))
    if mask is not None:
        a = jnp.where(mask, a, 0.0)
    return _dot(a.astype(BF16), vh), carry + total


def _fox_tile(qh, k, vh, fk_row, fq_col, m, l, mask):
    u = _dot_nt(qh, k) - fk_row
    if mask is not None:
        u = jnp.where(mask, u, NEG_BIG)
    m_new = jnp.maximum(m, jnp.max(u, axis=1, keepdims=True) + fq_col)
    alpha = jnp.exp(m - m_new)
    p = jnp.exp(u - (m_new - fq_col))
    l_new = alpha * l + jnp.sum(p, axis=1, keepdims=True)
    return _dot(p.astype(BF16), vh), m_new, l_new, alpha


def _sb_sweep(pair, n_left, state):
    def live(s):
        i, c0, c1, _ = s
        return jnp.logical_and(i < n_left, jnp.max(jnp.maximum(c0, c1)) > EXP_ZERO)

    def step(s):
        i, c0, c1, acc = s
        return (i + 1,) + pair(n_left - 1 - i, c0, c1, acc, None)

    return lax.while_loop(live, step, (jnp.int32(0),) + tuple(state))[3]


def _head_column(f, h):
    lane = lax.broadcasted_iota(jnp.int32, f.shape, 1)
    return jnp.sum(jnp.where(lane == h, f, 0.0), axis=1, keepdims=True)


def _sb_prompt_kernel(q_ref, k_ref, v_ref, o_ref, *, tile):
    qi = pl.program_id(2)
    q = q_ref[0]
    q0, q1 = _keep_head(q, True), _keep_head(q, False)

    def pair(kb, c0, c1, acc, mask):
        start = pl.multiple_of(kb * tile, tile)
        k = k_ref[0, pl.ds(start, tile), :]
        v = v_ref[0, pl.ds(start, tile), :]
        pv0, c0 = _sb_tile(q0, k, _keep_head(v, True), c0, mask)
        pv1, c1 = _sb_tile(q1, k, _keep_head(v, False), c1, mask)
        return c0, c1, acc + pv0 + pv1

    zero = jnp.zeros((tile, LANES), F32)
    state = pair(qi, zero, zero, zero, _causal(tile, tile, True))
    o_ref[0] = _sb_sweep(pair, qi, state)


def _head_norms_sq(x16):
    x = x16.astype(F32)
    r = lax.broadcasted_iota(jnp.int32, (LANES, LANES), 0)
    c = lax.broadcasted_iota(jnp.int32, (LANES, LANES), 1)
    same_head = jnp.where((r < HEAD_DIM) == (c < HEAD_DIM), 1.0, 0.0).astype(BF16)
    return _dot_split(x * x, same_head, 2)


def _fox_prompt_kernel(q_ref, k_ref, v_ref, fq_ref, fk_ref, o_ref, kmax_ref, *, tile):
    hp = pl.program_id(1)
    qi = pl.program_id(2)
    q = q_ref[0]
    q0, q1 = _keep_head(q, True), _keep_head(q, False)
    fq = fq_ref[0]
    fq0, fq1 = _head_column(fq, 2 * hp), _head_column(fq, 2 * hp + 1)
    shape = (tile, LANES)

    @pl.when(qi == 0)
    def _():
        rows = min(ROW_BLOCK, k_ref.shape[1])

        def chunk(i, best):
            kc = k_ref[0, pl.ds(pl.multiple_of(i * rows, rows), rows), :]
            return jnp.maximum(best, jnp.max(_head_norms_sq(kc), axis=0, keepdims=True))

        kmax_ref[...] = lax.fori_loop(0, k_ref.shape[1] // rows, chunk, jnp.zeros((1, LANES), F32))

    reach = (jnp.sqrt(_head_norms_sq(q)) * jnp.sqrt(kmax_ref[...]) * NORM_SLACK
             + _per_head(fq0, fq1, shape))

    def best_gap(kb, m0, m1):
        fk = fk_ref[0, 0, jnp.maximum(kb, 0)]
        last = _per_head(fk[0:1, tile - 1:tile], fk[1:2, tile - 1:tile], (1, LANES))
        return jnp.max(reach - _per_head(m0, m1, shape) - last)

    def pair(kb, m0, l0, m1, l1, acc, mask):
        start = pl.multiple_of(kb * tile, tile)
        k = k_ref[0, pl.ds(start, tile), :]
        v = v_ref[0, pl.ds(start, tile), :]
        fk = fk_ref[0, 0, kb]
        pv0, m0, l0, a0 = _fox_tile(q0, k, _keep_head(v, True), fk[0:1, :], fq0, m0, l0, mask)
        pv1, m1, l1, a1 = _fox_tile(q1, k, _keep_head(v, False), fk[1:2, :], fq1, m1, l1, mask)
        return m0, l0, m1, l1, acc * _per_head(a0, a1, shape) + pv0 + pv1

    neg = jnp.full((tile, 1), NEG_BIG, F32)
    zero = jnp.zeros((tile, 1), F32)
    state = pair(qi, neg, zero, neg, zero, jnp.zeros(shape, F32), _causal(tile, tile, False))

    def live(s):
        return jnp.logical_and(s[0] < qi, s[1] > EXP_ZERO)

    def step(s):
        i = s[0]
        new = pair(qi - 1 - i, *s[2:], None)
        return (i + 1, best_gap(qi - 2 - i, new[0], new[2])) + new

    state = lax.while_loop(live, step, (jnp.int32(0), best_gap(qi - 1, state[0], state[2])) + state)
    _, _, _, l0, _, l1, acc = state
    o_ref[0] = acc / _per_head(l0, l1, shape)


def _prompt_specs(t, tile):
    qspec = pl.BlockSpec((1, tile, LANES), lambda b, h, i: (b, i, h))
    kvspec = pl.BlockSpec((1, t, LANES), lambda b, h, i: (b, 0, h))
    return qspec, kvspec


def _sb_prompt(q, k, v):
    b, t, w = q.shape
    tile = ATT_BLOCK
    qspec, kvspec = _prompt_specs(t, tile)
    return pl.pallas_call(
        functools.partial(_sb_prompt_kernel, tile=tile),
        grid=(b, w // LANES, t // tile),
        in_specs=[qspec, kvspec, kvspec],
        out_specs=qspec,
        out_shape=jax.ShapeDtypeStruct((b, t, w), F32),
        compiler_params=_params("arbitrary", "arbitrary", "arbitrary"),
        name="sb_prompt",
    )(q, k, v)


def _fox_prompt(q, k, v, fq, fk):
    b, t, w = q.shape
    tile = ATT_BLOCK
    qspec, kvspec = _prompt_specs(t, tile)
    n_f = fq.shape[-1]
    return pl.pallas_call(
        functools.partial(_fox_prompt_kernel, tile=tile),
        grid=(b, w // LANES, t // tile),
        in_specs=[qspec, kvspec, kvspec,
                  pl.BlockSpec((1, tile, n_f), lambda b, h, i: (b, i, 0)),
                  pl.BlockSpec((1, 1, t // tile, 2, tile), lambda b, h, i: (b, h, 0, 0, 0))],
        out_specs=qspec,
        out_shape=jax.ShapeDtypeStruct((b, t, w), F32),
        scratch_shapes=[pltpu.VMEM((1, LANES), F32)],
        compiler_params=_params("arbitrary", "arbitrary", "arbitrary"),
        name="fox_prompt",
    )(q, k, v, fq, fk)


def _cache_tiles(past, tile):
    return [(s, min(tile, past - s)) for s in range(0, past, tile)][::-1]


def _sb_sample_kernel(q_ref, kn_ref, vn_ref, kc_ref, vc_ref, o_ref, *, tile):
    q = q_ref[0]
    n = q.shape[0]
    past = kc_ref.shape[1]
    q0, q1 = _keep_head(q, True), _keep_head(q, False)

    def pair(kb, c0, c1, acc, mask):
        if mask is not None:
            k, v = kn_ref[0], vn_ref[0]
        else:
            start = pl.multiple_of(kb * tile, tile)
            k = kc_ref[0, pl.ds(start, tile), :].astype(BF16)
            v = vc_ref[0, pl.ds(start, tile), :].astype(BF16)
        pv0, c0 = _sb_tile(q0, k, _keep_head(v, True), c0, mask)
        pv1, c1 = _sb_tile(q1, k, _keep_head(v, False), c1, mask)
        return c0, c1, acc + pv0 + pv1

    zero = jnp.zeros((n, LANES), F32)
    state = pair(None, zero, zero, zero, _causal(n, n, True))
    o_ref[0] = _sb_sweep(pair, past // tile, state)


def _fox_sample_kernel(q_ref, kn_ref, vn_ref, kc_ref, vc_ref, fq_ref, fk_ref, o_ref, *, tile):
    hp = pl.program_id(1)
    q = q_ref[0]
    n = q.shape[0]
    past = kc_ref.shape[1]
    q0, q1 = _keep_head(q, True), _keep_head(q, False)
    fq = fq_ref[0]
    fq0, fq1 = _head_column(fq, 2 * hp), _head_column(fq, 2 * hp + 1)
    shape = (n, LANES)
    m0 = m1 = jnp.full((n, 1), NEG_BIG, F32)
    l0 = l1 = jnp.zeros((n, 1), F32)
    acc = jnp.zeros(shape, F32)
    tiles = [(None, n)] + _cache_tiles(past, tile)
    for start, size in tiles:
        if start is None:
            k, v, mask, f0 = kn_ref[0], vn_ref[0], _causal(n, n, False), past
        else:
            k = kc_ref[0, start:start + size, :].astype(BF16)
            v = vc_ref[0, start:start + size, :].astype(BF16)
            mask, f0 = None, start
        fk = fk_ref[0, 0, :, f0:f0 + size]
        pv0, m0, l0, a0 = _fox_tile(q0, k, _keep_head(v, True), fk[0:1, :], fq0, m0, l0, mask)
        pv1, m1, l1, a1 = _fox_tile(q1, k, _keep_head(v, False), fk[1:2, :], fq1, m1, l1, mask)
        acc = acc * _per_head(a0, a1, shape) + pv0 + pv1
    o_ref[0] = acc / _per_head(l0, l1, shape)


def _sample_specs(n, past):
    new = pl.BlockSpec((1, n, LANES), lambda b, h: (b, 0, h))
    cache = pl.BlockSpec((1, past, LANES), lambda b, h: (b, 0, h))
    return new, cache


def _sb_sample(q, kn, vn, kc, vc):
    b, n, w = q.shape
    new, cache = _sample_specs(n, kc.shape[1])
    return pl.pallas_call(
        functools.partial(_sb_sample_kernel, tile=ATT_BLOCK),
        grid=(b, w // LANES),
        in_specs=[new, new, new, cache, cache],
        out_specs=new,
        out_shape=jax.ShapeDtypeStruct((b, n, w), F32),
        compiler_params=_params("arbitrary", "arbitrary"),
        name="sb_sample",
    )(q, kn, vn, kc, vc)


def _fox_sample(q, kn, vn, kc, vc, fq, fk):
    b, n, w = q.shape
    new, cache = _sample_specs(n, kc.shape[1])
    return pl.pallas_call(
        functools.partial(_fox_sample_kernel, tile=ATT_BLOCK),
        grid=(b, w // LANES),
        in_specs=[new, new, new, cache, cache,
                  pl.BlockSpec((1, n, fq.shape[-1]), lambda b, h: (b, 0, 0)),
                  pl.BlockSpec((1, 1, 2, fk.shape[-1]), lambda b, h: (b, h, 0, 0))],
        out_specs=new,
        out_shape=jax.ShapeDtypeStruct((b, n, w), F32),
        compiler_params=_params("arbitrary", "arbitrary"),
        name="fox_sample",
    )(q, kn, vn, kc, vc, fq, fk)


def _post_kernel(x_ref, osb_ref, ofx_ref, ada_ref, gsb_ref, gfx_ref, wo_ref, gffn_ref,
                 x2_ref, h_ref):
    x = x_ref[...]
    nb, tt, d = x.shape
    rows = nb * tt
    ada = ada_ref[...]
    o_sb = _rms(osb_ref[...], gsb_ref[...])
    o_fx = _rms(ofx_ref[...], gfx_ref[...])
    sb_w = o_sb.shape[-1]
    fx_w = o_fx.shape[-1]
    proj = (_dot(o_sb.reshape(rows, sb_w).astype(BF16), wo_ref[0:sb_w, :])
            + _dot(o_fx.reshape(rows, fx_w).astype(BF16), wo_ref[sb_w:sb_w + fx_w, :]))
    x2 = x + (1.0 + ada[:, 2:3, :]) * proj.reshape(nb, tt, d)
    x2_ref[...] = x2
    h = _rms(x2, gffn_ref[...]) * (1.0 + ada[:, 4:5, :]) + ada[:, 3:4, :]
    h_ref[...] = h.astype(BF16)


def _ffn_kernel(h_ref, x2_ref, ada_ref, wg_ref, wu_ref, wd_ref, gfin_ref, y_ref, acc_ref, *, final_norm):
    j = pl.program_id(2)
    nb, tt, d = h_ref.shape
    h = h_ref[...].reshape(nb * tt, d)
    g = _dot(h, wg_ref[...])
    u = _dot(h, wu_ref[...])
    act = (g * (1.0 / (1.0 + jnp.exp(-g))) * u).astype(BF16)
    part = _dot(act, wd_ref[...])

    @pl.when(j == 0)
    def _():
        acc_ref[...] = part

    @pl.when(j > 0)
    def _():
        acc_ref[...] += part

    @pl.when(j == pl.num_programs(2) - 1)
    def _():
        ada = ada_ref[...]
        x3 = x2_ref[...] + (1.0 + ada[:, 5:6, :]) * acc_ref[...].reshape(nb, tt, d)
        y_ref[...] = _rms(x3, gfin_ref[...]) if final_norm else x3


def _post(x, o_sb, o_fx, ada, g_sb, g_fx, wo16, g_ffn, wg16, wu16, wd16, g_fin, final_norm):
    b, t, d = x.shape
    tt = min(t, ROW_BLOCK)
    nb = max(1, ROW_BLOCK // tt)
    tok = lambda w: pl.BlockSpec((nb, tt, w), lambda i, j: (i, j, 0))
    const = lambda a: pl.BlockSpec(a.shape, lambda i, j: (0, 0))
    x2, h = pl.pallas_call(
        _post_kernel,
        grid=(b // nb, t // tt),
        in_specs=[tok(d), tok(o_sb.shape[-1]), tok(o_fx.shape[-1]),
                  pl.BlockSpec((nb, 6, d), lambda i, j: (i, 0, 0)),
                  const(g_sb), const(g_fx), const(wo16), const(g_ffn)],
        out_specs=[tok(d), tok(d)],
        out_shape=[jax.ShapeDtypeStruct((b, t, d), F32), jax.ShapeDtypeStruct((b, t, d), BF16)],
        compiler_params=_params("arbitrary", "arbitrary"),
        name="post_mixer",
    )(x, o_sb, o_fx, ada, g_sb, g_fx, wo16, g_ffn)

    d_ff = wg16.shape[1]
    n_ff = 2 if (d_ff // 2) % LANES == 0 else 1
    tf = d_ff // n_ff
    tok3 = lambda w: pl.BlockSpec((nb, tt, w), lambda i, j, f: (i, j, 0))
    return pl.pallas_call(
        functools.partial(_ffn_kernel, final_norm=final_norm),
        grid=(b // nb, t // tt, n_ff),
        in_specs=[tok3(d), tok3(d), pl.BlockSpec((nb, 6, d), lambda i, j, f: (i, 0, 0)),
                  pl.BlockSpec((d, tf), lambda i, j, f: (0, f)),
                  pl.BlockSpec((d, tf), lambda i, j, f: (0, f)),
                  pl.BlockSpec((tf, d), lambda i, j, f: (f, 0)),
                  pl.BlockSpec(g_fin.shape, lambda i, j, f: (0, 0))],
        out_specs=tok3(d),
        out_shape=jax.ShapeDtypeStruct((b, t, d), F32),
        scratch_shapes=[pltpu.VMEM((nb * tt, d), F32)],
        compiler_params=_params("arbitrary", "arbitrary", "arbitrary"),
        name="ffn",
    )(h, x2, ada, wg16, wu16, wd16, g_fin)


def _pad_lanes(a, n):
    return jnp.pad(a, ((0, 0), (0, n - a.shape[1])))


def kernel(x_prompt, x_sample, c_prompt, c_sample, cache_sb_k, cache_sb_v, cache_fox_k, cache_fox_v, cache_fox_logf, w_ada, b_ada, g_mix, w_in, b_f, g_sb_out, g_fox_out, w_o, g_ffn, w_gate, w_up, w_down, g_final):
    depth = w_ada.shape[0]
    bp, tp, d = x_prompt.shape
    bs, ts, _ = x_sample.shape
    past, n_sb, hd = cache_sb_k.shape[2:]
    n_fx = cache_fox_k.shape[3]
    assert hd == HEAD_DIM and n_sb % 2 == 0 and n_fx % 2 == 0
    assert tp % ROW_BLOCK == 0 and tp % ATT_BLOCK == 0 and ROW_BLOCK % ts == 0 and bs % (ROW_BLOCK // ts) == 0
    assert past % ATT_BLOCK == 0
    sb_w, fx_w = n_sb * hd, n_fx * hd
    qkv_cols = 3 * sb_w + 3 * fx_w
    att = ATT_BLOCK
    past_pad = -(-(past + ts) // (3 * LANES)) * (3 * LANES)
    row = lambda a: a.reshape(1, -1)

    xp, xs = x_prompt, x_sample
    outs = [[] for _ in range(10)]
    for l in range(depth):
        w16 = w_in[l][:, :qkv_cols].astype(BF16)
        wf16 = _pad_lanes(w_in[l][:, qkv_cols:], LANES).astype(BF16)
        bfp = _pad_lanes(row(b_f[l]), LANES)
        wo16, wg16 = w_o[l].astype(BF16), w_gate[l].astype(BF16)
        wu16, wd16 = w_up[l].astype(BF16), w_down[l].astype(BF16)

        c_all = jnp.concatenate([c_prompt, c_sample], axis=0)
        ada = _ada(c_all, w_ada[l].astype(BF16), row(b_ada[l])).reshape(bp + bs, 6, d)
        ada_p, ada_s = ada[:bp], ada[bp:]

        (q_sb, k_sb, v_sb, k_sb16, v_sb16, q_fx, k_fx, v_fx, k_fx16, v_fx16, lf) = _pre(
            xp, ada_p, row(g_mix[l]), w16, wf16, bfp, sb_w, fx_w, n_fx)
        o_sb = _sb_prompt(q_sb, k_sb16, v_sb16)
        f_cum = _cumsum_rows(jnp.swapaxes(lf, 1, 2).reshape(bp * n_fx, tp), ROW_BLOCK)
        fq = jnp.swapaxes(f_cum.reshape(bp, n_fx, tp), 1, 2)
        fk = jnp.swapaxes(f_cum.reshape(bp, n_fx // 2, 2, tp // att, att), 2, 3)
        o_fx = _fox_prompt(q_fx, k_fx16, v_fx16, fq, fk)
        xp = _post(xp, o_sb, o_fx, ada_p, row(g_sb_out[l]), row(g_fox_out[l]), wo16, row(g_ffn[l]),
                   wg16, wu16, wd16, row(g_final), l + 1 == depth)
        for dst, a in zip(outs[0:4], (k_sb, v_sb, k_fx, v_fx)):
            dst.append(a.reshape(bp, tp, -1, hd))
        outs[4].append(lf)

        (q_sb, k_sb, v_sb, k_sb16, v_sb16, q_fx, k_fx, v_fx, k_fx16, v_fx16, lf) = _pre(
            xs, ada_s, row(g_mix[l]), w16, wf16, bfp, sb_w, fx_w, n_fx)
        o_sb = _sb_sample(q_sb, k_sb16, v_sb16,
                          cache_sb_k[l].reshape(bs, past, sb_w), cache_sb_v[l].reshape(bs, past, sb_w))
        lf_all = jnp.concatenate([cache_fox_logf[l].astype(F32), lf], axis=1)
        lf_rows = _pad_lanes(jnp.swapaxes(lf_all, 1, 2).reshape(bs * n_fx, past + ts), past_pad)
        f_all = _cumsum_rows(lf_rows, past_pad // 3)
        fq = jnp.swapaxes(f_all.reshape(bs, n_fx, past_pad)[:, :, past:past + ts], 1, 2)
        fk = f_all.reshape(bs, n_fx // 2, 2, past_pad)
        o_fx = _fox_sample(q_fx, k_fx16, v_fx16,
                           cache_fox_k[l].reshape(bs, past, fx_w), cache_fox_v[l].reshape(bs, past, fx_w),
                           fq, fk)
        xs = _post(xs, o_sb, o_fx, ada_s, row(g_sb_out[l]), row(g_fox_out[l]), wo16, row(g_ffn[l]),
                   wg16, wu16, wd16, row(g_final), l + 1 == depth)
        for dst, a in zip(outs[5:9], (k_sb, v_sb, k_fx, v_fx)):
            dst.append(a.reshape(bs, ts, -1, hd))
        outs[9].append(lf)

    return (xp, xs) + tuple(jnp.stack(o) for o in outs)
```

```python
import functools

import jax
import jax.numpy as jnp
from jax import lax
from jax.experimental import pallas as pl
from jax.experimental.pallas import tpu as pltpu

F32 = jnp.float32
BF16 = jnp.bfloat16
EPS = 1e-6
HEAD_DIM = 64
LANES = 128
ROW_BLOCK = 512
ATT_BLOCK = 256
SB_BLOCKS_PER_STEP = 2
FOX_BLOCK = 512
NEG_BIG = -1e30
EXP_ZERO = -110.0
NORM_SLACK = 1.0 + 2.0 ** -8
VMEM_LIMIT = 56 * 1024 * 1024


def _params(*semantics):
    return pltpu.CompilerParams(dimension_semantics=semantics, vmem_limit_bytes=VMEM_LIMIT)


def _dot(a, b):
    return jnp.dot(a, b, preferred_element_type=F32)


def _dot_nt(a, b):
    return lax.dot_general(a, b, (((1,), (1,)), ((), ())), preferred_element_type=F32)


def _rms(x, g):
    return x * lax.rsqrt(jnp.mean(x * x, axis=-1, keepdims=True) + EPS) * g


def _split_bf16(x, parts):
    out = []
    for _ in range(parts - 1):
        p = x.astype(BF16)
        out.append(p)
        x = x - p.astype(F32)
    out.append(x.astype(BF16))
    return out


def _dot_split(x, m, parts):
    acc = None
    for p in _split_bf16(x, parts):
        d = _dot(p, m)
        acc = d if acc is None else acc + d
    return acc


def _keep_head(x, first):
    lane = lax.broadcasted_iota(jnp.int32, x.shape, x.ndim - 1)
    keep = (lane < HEAD_DIM) if first else (lane >= HEAD_DIM)
    return jnp.where(keep, x, jnp.zeros_like(x))


def _per_head(col0, col1, shape):
    lane = lax.broadcasted_iota(jnp.int32, shape, 1)
    return jnp.where(lane < HEAD_DIM, col0, col1)


def _lanes_to(x, n):
    if n <= LANES:
        return x[:, :n]
    return jnp.concatenate([x] * (n // LANES), axis=1)


def _ada_kernel(c_ref, w_ref, b_ref, o_ref):
    c = c_ref[...]
    s = (c * (1.0 / (1.0 + jnp.exp(-c)))).astype(BF16)
    o_ref[...] = _dot(s, w_ref[...]) + b_ref[...]


def _ada(c, w16, b):
    rows, d = c.shape
    n = w16.shape[1]
    tn = 1024
    return pl.pallas_call(
        _ada_kernel,
        grid=(n // tn,),
        in_specs=[pl.BlockSpec((rows, d), lambda j: (0, 0)),
                  pl.BlockSpec((d, tn), lambda j: (0, j)),
                  pl.BlockSpec((1, tn), lambda j: (0, j))],
        out_specs=pl.BlockSpec((rows, tn), lambda j: (0, j)),
        out_shape=jax.ShapeDtypeStruct((rows, n), F32),
        compiler_params=_params("arbitrary"),
        name="ada",
    )(c, w16, b)


def _pre_kernel(x_ref, ada_ref, g_ref, w_ref, wf_ref, bf_ref,
                qsb_ref, ksb_ref, vsb_ref, ksb16_ref, vsb16_ref,
                qfx_ref, kfx_ref, vfx_ref, kfx16_ref, vfx16_ref, lf_ref, *, sb_w, fx_w, n_f):
    x = x_ref[...]
    nb, tt, d = x.shape
    ada = ada_ref[...]
    h = _rms(x, g_ref[...]) * (1.0 + ada[:, 1:2, :]) + ada[:, 0:1, :]
    h = h.reshape(nb * tt, d).astype(BF16)
    q_scale = HEAD_DIM ** -0.5

    def proj(lo, width):
        return _dot(h, w_ref[:, lo:lo + width]).reshape(nb, tt, width)

    qsb_ref[...] = (proj(0, sb_w) * q_scale).astype(BF16)
    k = proj(sb_w, sb_w)
    ksb_ref[...] = k
    ksb16_ref[...] = k.astype(BF16)
    v = proj(2 * sb_w, sb_w)
    vsb_ref[...] = v
    vsb16_ref[...] = v.astype(BF16)
    base = 3 * sb_w
    qfx_ref[...] = (proj(base, fx_w) * q_scale).astype(BF16)
    k = proj(base + fx_w, fx_w)
    kfx_ref[...] = k
    kfx16_ref[...] = k.astype(BF16)
    v = proj(base + 2 * fx_w, fx_w)
    vfx_ref[...] = v
    vfx16_ref[...] = v.astype(BF16)
    u = _dot(h, wf_ref[...]) + bf_ref[...]
    lf = jnp.minimum(u, 0.0) - jnp.log(1.0 + jnp.exp(-jnp.abs(u)))
    lf_ref[...] = lf[:, :n_f].reshape(nb, tt, n_f)


def _pre(x, ada, g, w16, wf16, bfp, sb_w, fx_w, n_f):
    b, t, d = x.shape
    tt = min(t, ROW_BLOCK)
    nb = max(1, ROW_BLOCK // tt)
    grid = (b // nb, t // tt)
    tok = lambda w: pl.BlockSpec((nb, tt, w), lambda i, j: (i, j, 0))
    const = lambda a: pl.BlockSpec(a.shape, lambda i, j: (0, 0))
    shape = lambda w, dt: jax.ShapeDtypeStruct((b, t, w), dt)
    return pl.pallas_call(
        functools.partial(_pre_kernel, sb_w=sb_w, fx_w=fx_w, n_f=n_f),
        grid=grid,
        in_specs=[tok(d), pl.BlockSpec((nb, 6, d), lambda i, j: (i, 0, 0)),
                  const(g), const(w16), const(wf16), const(bfp)],
        out_specs=[tok(sb_w)] * 5 + [tok(fx_w)] * 5 + [tok(n_f)],
        out_shape=[shape(sb_w, BF16), shape(sb_w, F32), shape(sb_w, F32), shape(sb_w, BF16),
                   shape(sb_w, BF16),
                   shape(fx_w, BF16), shape(fx_w, F32), shape(fx_w, F32), shape(fx_w, BF16),
                   shape(fx_w, BF16),
                   shape(n_f, F32)],
        compiler_params=_params("arbitrary", "arbitrary"),
        name="pre_mixer",
    )(x, ada, g, w16, wf16, bfp)


def _cumsum_kernel(x_ref, o_ref, carry_ref):
    @pl.when(pl.program_id(0) == 0)
    def _():
        carry_ref[...] = jnp.zeros_like(carry_ref)

    x = x_ref[...]
    tc = x.shape[1]
    r = lax.broadcasted_iota(jnp.int32, (tc, tc), 0)
    c = lax.broadcasted_iota(jnp.int32, (tc, tc), 1)
    upper = jnp.where(r <= c, 1.0, 0.0).astype(BF16)
    carry = carry_ref[...]
    o_ref[...] = _dot_split(x, upper, 3) + _lanes_to(carry, tc)
    carry_ref[...] = carry + _dot_split(x, jnp.ones((tc, LANES), BF16), 3)


def _cumsum_rows(x, tc):
    rows, t = x.shape
    return pl.pallas_call(
        _cumsum_kernel,
        grid=(t // tc,),
        in_specs=[pl.BlockSpec((rows, tc), lambda j: (0, j))],
        out_specs=pl.BlockSpec((rows, tc), lambda j: (0, j)),
        out_shape=jax.ShapeDtypeStruct((rows, t), F32),
        scratch_shapes=[pltpu.VMEM((rows, LANES), F32)],
        compiler_params=_params("arbitrary"),
        name="cumsum_time",
    )(x)


def _tri_strict(n):
    r = lax.broadcasted_iota(jnp.int32, (n, n), 0)
    c = lax.broadcasted_iota(jnp.int32, (n, n), 1)
    return jnp.where(r > c, 1.0, 0.0).astype(BF16)


def _causal(tq, tk, strict):
    r = lax.broadcasted_iota(jnp.int32, (tq, tk), 0)
    c = lax.broadcasted_iota(jnp.int32, (tq, tk), 1)
    return (c < r) if strict else (c <= r)


def _sb_tiles(chains, mask):
    tk = chains[0][1].shape[0]
    tri = _tri_strict(tk)
    wide = tk % LANES == 0
    if wide:
        tri2 = jnp.concatenate([tri, tri], axis=0)
    zs = [_dot_nt(q, k) for q, k, _, _ in chains]
    staged = []
    for z in zs:
        log_beta = jnp.minimum(z, 0.0) - jnp.log(1.0 + jnp.exp(-jnp.abs(z)))
        log_keep = log_beta - z
        if mask is not None:
            log_keep = jnp.where(mask, log_keep, 0.0)
        hi, lo = _split_bf16(log_keep, 2)
        staged.append((log_beta, log_keep[:, 0:1], hi, lo))
    if wide:
        betweens = [_dot(jnp.concatenate([hi, lo], axis=1), tri2) for _, _, hi, lo in staged]
    else:
        betweens = [_dot(hi, tri) + _dot(lo, tri) for _, _, hi, lo in staged]
    weights = []
    for (log_beta, _, _, _), between, (_, _, _, carry) in zip(staged, betweens, chains):
        a = jnp.exp(log_beta + between + carry)
        if mask is not None:
            a = jnp.where(mask, a, 0.0)
        weights.append(a.astype(BF16))
    pvs = [_dot(a, v) for a, (_, _, v, _) in zip(weights, chains)]
    return [(pv, carry + (between[:, 0:1] + first))
            for pv, between, (_, first, _, _), (_, _, _, carry) in zip(pvs, betweens, staged, chains)]


def _fox_tile(qh, k, vh, fk_row, fq_col, m, l, mask):
    u = _dot_nt(qh, k) - fk_row
    if mask is not None:
        u = jnp.where(mask, u, NEG_BIG)
    m_new = jnp.maximum(m, jnp.max(u, axis=1, keepdims=True) + fq_col)
    alpha = jnp.exp(m - m_new)
    p = jnp.exp(u - (m_new - fq_col))
    l_new = alpha * l + jnp.sum(p, axis=1, keepdims=True)
    return _dot(p.astype(BF16), vh), m_new, l_new, alpha


def _sb_sweep(pair, n_left, state):
    def live(s):
        i, c0, c1, _ = s
        return jnp.logical_and(i < n_left, jnp.max(jnp.maximum(c0, c1)) > EXP_ZERO)

    def step(s):
        i, c0, c1, acc = s
        return (i + 1,) + pair(n_left - 1 - i, c0, c1, acc, None)

    return lax.while_loop(live, step, (jnp.int32(0),) + tuple(state))[3]


def _head_column(f, h):
    lane = lax.broadcasted_iota(jnp.int32, f.shape, 1)
    return jnp.sum(jnp.where(lane == h, f, 0.0), axis=1, keepdims=True)


def _sb_prompt_kernel(q_ref, k_ref, v_ref, o_ref, *, tile, blocks):
    first = pl.program_id(2) * blocks
    qs = []
    for j in range(blocks):
        q = q_ref[0, j * tile:(j + 1) * tile, :]
        qs.append((_keep_head(q, True), _keep_head(q, False)))

    def visit(kbs, state, mask):
        chains = []
        for j in range(blocks):
            start = pl.multiple_of(kbs[j] * tile, tile)
            k = k_ref[0, pl.ds(start, tile), :]
            v = v_ref[0, pl.ds(start, tile), :]
            chains.append((qs[j][0], k, _keep_head(v, True), state[3 * j]))
            chains.append((qs[j][1], k, _keep_head(v, False), state[3 * j + 1]))
        res = _sb_tiles(chains, mask)
        out = ()
        for j in range(blocks):
            (pv0, c0), (pv1, c1) = res[2 * j], res[2 * j + 1]
            out += (c0, c1, state[3 * j + 2] + pv0 + pv1)
        return out

    col = jnp.zeros((tile, 1), F32)
    state = visit([first + j for j in range(blocks)],
                  (col, col, jnp.zeros((tile, LANES), F32)) * blocks, _causal(tile, tile, True))

    def live(s):
        i = s[0]
        alive = jnp.bool_(False)
        for j in range(blocks):
            c0, c1 = s[1 + 3 * j], s[2 + 3 * j]
            alive = jnp.logical_or(alive, jnp.logical_and(i < first + j,
                                                          jnp.max(jnp.maximum(c0, c1)) > EXP_ZERO))
        return alive

    def step(s):
        i = s[0]
        old = s[1:]
        new = visit([jnp.maximum(first + j - 1 - i, 0) for j in range(blocks)], old, None)
        out = (i + 1,)
        for j in range(blocks):
            inside = i < first + j
            out += tuple(jnp.where(inside, n, o) for n, o in zip(new[3 * j:3 * j + 3], old[3 * j:3 * j + 3]))
        return out

    state = lax.while_loop(live, step, (jnp.int32(0),) + state)
    for j in range(blocks):
        o_ref[0, j * tile:(j + 1) * tile, :] = state[3 + 3 * j]


def _head_norms_sq(x16):
    x = x16.astype(F32)
    r = lax.broadcasted_iota(jnp.int32, (LANES, LANES), 0)
    c = lax.broadcasted_iota(jnp.int32, (LANES, LANES), 1)
    same_head = jnp.where((r < HEAD_DIM) == (c < HEAD_DIM), 1.0, 0.0).astype(BF16)
    return _dot_split(x * x, same_head, 2)


def _fox_prompt_kernel(q_ref, k_ref, v_ref, fq_ref, fk_ref, o_ref, kmax_ref, *, tile):
    hp = pl.program_id(1)
    qi = pl.program_id(2)
    q = q_ref[0]
    q0, q1 = _keep_head(q, True), _keep_head(q, False)
    fq = fq_ref[0]
    fq0, fq1 = _head_column(fq, 2 * hp), _head_column(fq, 2 * hp + 1)
    shape = (tile, LANES)

    @pl.when(qi == 0)
    def _():
        rows = min(ROW_BLOCK, k_ref.shape[1])

        def chunk(i, best):
            kc = k_ref[0, pl.ds(pl.multiple_of(i * rows, rows), rows), :]
            return jnp.maximum(best, jnp.max(_head_norms_sq(kc), axis=0, keepdims=True))

        kmax_ref[...] = lax.fori_loop(0, k_ref.shape[1] // rows, chunk, jnp.zeros((1, LANES), F32))

    reach = (jnp.sqrt(_head_norms_sq(q)) * jnp.sqrt(kmax_ref[...]) * NORM_SLACK
             + _per_head(fq0, fq1, shape))

    def last_fk(kb):
        fk = fk_ref[0, 0, jnp.maximum(kb, 0)]
        return _per_head(fk[0:1, tile - 1:tile], fk[1:2, tile - 1:tile], (1, LANES))

    def scores(kb, mask):
        k = k_ref[0, pl.ds(pl.multiple_of(kb * tile, tile), tile), :]
        fk = fk_ref[0, 0, kb]
        u0 = _dot_nt(q0, k) - fk[0:1, :]
        u1 = _dot_nt(q1, k) - fk[1:2, :]
        if mask is not None:
            u0, u1 = jnp.where(mask, u0, NEG_BIG), jnp.where(mask, u1, NEG_BIG)
        return u0, u1

    causal = _causal(tile, tile, False)

    def row_max(kb, m0, m1, mask):
        u0, u1 = scores(kb, mask)
        return (jnp.maximum(m0, jnp.max(u0, axis=1, keepdims=True) + fq0),
                jnp.maximum(m1, jnp.max(u1, axis=1, keepdims=True) + fq1))

    def best_gap(kb, m0, m1):
        return jnp.max(reach - _per_head(m0, m1, shape) - last_fk(kb))

    def max_step(s):
        i = s[0]
        m0, m1 = row_max(qi - 1 - i, s[2], s[3], None)
        return i + 1, best_gap(qi - 2 - i, m0, m1), m0, m1

    neg = jnp.full((tile, 1), NEG_BIG, F32)
    m0, m1 = row_max(qi, neg, neg, causal)
    _, _, m0, m1 = lax.while_loop(lambda s: jnp.logical_and(s[0] < qi, s[1] > 0.0), max_step,
                                  (jnp.int32(0), best_gap(qi - 1, m0, m1), m0, m1))

    c0, c1 = m0 - fq0, m1 - fq1
    gap_rows = jnp.max(reach - _per_head(m0, m1, shape), axis=0, keepdims=True)

    def weigh(kb, acc0, acc1, mask):
        u0, u1 = scores(kb, mask)
        v = v_ref[0, pl.ds(pl.multiple_of(kb * tile, tile), tile), :]
        lane = lax.broadcasted_iota(jnp.int32, v.shape, 1)
        one = jnp.ones_like(v)
        acc0 = acc0 + _dot(jnp.exp(u0 - c0).astype(BF16), jnp.where(lane < HEAD_DIM, v, one))
        acc1 = acc1 + _dot(jnp.exp(u1 - c1).astype(BF16), jnp.where(lane < HEAD_DIM, one, v))
        return acc0, acc1

    def sum_step(s):
        i = s[0]
        acc0, acc1 = weigh(qi - 1 - i, s[2], s[3], None)
        return i + 1, jnp.max(gap_rows - last_fk(qi - 2 - i)), acc0, acc1

    zero = jnp.zeros(shape, F32)
    acc0, acc1 = weigh(qi, zero, zero, causal)
    _, _, acc0, acc1 = lax.while_loop(lambda s: jnp.logical_and(s[0] < qi, s[1] > EXP_ZERO), sum_step,
                                      (jnp.int32(0), jnp.max(gap_rows - last_fk(qi - 1)), acc0, acc1))
    lane = lax.broadcasted_iota(jnp.int32, shape, 1)
    half = LANES // 2
    o_ref[0] = jnp.where(lane < HEAD_DIM, acc0 / pltpu.roll(acc0, half, 1), acc1 / pltpu.roll(acc1, half, 1))


def _prompt_specs(t, tile):
    qspec = pl.BlockSpec((1, tile, LANES), lambda b, h, i: (b, i, h))
    kvspec = pl.BlockSpec((1, t, LANES), lambda b, h, i: (b, 0, h))
    return qspec, kvspec


def _sb_prompt(q, k, v):
    b, t, w = q.shape
    tile = ATT_BLOCK
    blocks = SB_BLOCKS_PER_STEP if t % (tile * SB_BLOCKS_PER_STEP) == 0 else 1
    qspec, kvspec = _prompt_specs(t, tile * blocks)
    return pl.pallas_call(
        functools.partial(_sb_prompt_kernel, tile=tile, blocks=blocks),
        grid=(b, w // LANES, t // (tile * blocks)),
        in_specs=[qspec, kvspec, kvspec],
        out_specs=qspec,
        out_shape=jax.ShapeDtypeStruct((b, t, w), F32),
        compiler_params=_params("arbitrary", "arbitrary", "arbitrary"),
        name="sb_prompt",
    )(q, k, v)


def _fox_prompt(q, k, v, fq, fk):
    b, t, w = q.shape
    tile = fk.shape[-1]
    qspec, kvspec = _prompt_specs(t, tile)
    n_f = fq.shape[-1]
    return pl.pallas_call(
        functools.partial(_fox_prompt_kernel, tile=tile),
        grid=(b, w // LANES, t // tile),
        in_specs=[qspec, kvspec, kvspec,
                  pl.BlockSpec((1, tile, n_f), lambda b, h, i: (b, i, 0)),
                  pl.BlockSpec((1, 1, t // tile, 2, tile), lambda b, h, i: (b, h, 0, 0, 0))],
        out_specs=qspec,
        out_shape=jax.ShapeDtypeStruct((b, t, w), F32),
        scratch_shapes=[pltpu.VMEM((1, LANES), F32)],
        compiler_params=_params("arbitrary", "arbitrary", "arbitrary"),
        name="fox_prompt",
    )(q, k, v, fq, fk)


def _cache_tiles(past, tile):
    return [(s, min(tile, past - s)) for s in range(0, past, tile)][::-1]


def _sb_sample_kernel(q_ref, kn_ref, vn_ref, kc_ref, vc_ref, o_ref, *, tile):
    q = q_ref[0]
    n = q.shape[0]
    past = kc_ref.shape[1]
    q0, q1 = _keep_head(q, True), _keep_head(q, False)

    def pair(kb, c0, c1, acc, mask):
        if mask is not None:
            k, v = kn_ref[0], vn_ref[0]
        else:
            start = pl.multiple_of(kb * tile, tile)
            k = kc_ref[0, pl.ds(start, tile), :].astype(BF16)
            v = vc_ref[0, pl.ds(start, tile), :].astype(BF16)
        (pv0, c0), (pv1, c1) = _sb_tiles([(q0, k, _keep_head(v, True), c0),
                                          (q1, k, _keep_head(v, False), c1)], mask)
        return c0, c1, acc + pv0 + pv1

    col = jnp.zeros((n, 1), F32)
    state = pair(None, col, col, jnp.zeros((n, LANES), F32), _causal(n, n, True))
    o_ref[0] = _sb_sweep(pair, past // tile, state)


def _fox_sample_kernel(q_ref, kn_ref, vn_ref, kc_ref, vc_ref, fq_ref, fk_ref, o_ref, *, tile):
    hp = pl.program_id(1)
    q = q_ref[0]
    n = q.shape[0]
    past = kc_ref.shape[1]
    q0, q1 = _keep_head(q, True), _keep_head(q, False)
    fq = fq_ref[0]
    fq0, fq1 = _head_column(fq, 2 * hp), _head_column(fq, 2 * hp + 1)
    shape = (n, LANES)
    m0 = m1 = jnp.full((n, 1), NEG_BIG, F32)
    l0 = l1 = jnp.zeros((n, 1), F32)
    acc = jnp.zeros(shape, F32)
    tiles = [(None, n)] + _cache_tiles(past, tile)
    for start, size in tiles:
        if start is None:
            k, v, mask, f0 = kn_ref[0], vn_ref[0], _causal(n, n, False), past
        else:
            k = kc_ref[0, start:start + size, :].astype(BF16)
            v = vc_ref[0, start:start + size, :].astype(BF16)
            mask, f0 = None, start
        fk = fk_ref[0, 0, :, f0:f0 + size]
        pv0, m0, l0, a0 = _fox_tile(q0, k, _keep_head(v, True), fk[0:1, :], fq0, m0, l0, mask)
        pv1, m1, l1, a1 = _fox_tile(q1, k, _keep_head(v, False), fk[1:2, :], fq1, m1, l1, mask)
        acc = acc * _per_head(a0, a1, shape) + pv0 + pv1
    o_ref[0] = acc / _per_head(l0, l1, shape)


def _sample_specs(n, past):
    new = pl.BlockSpec((1, n, LANES), lambda b, h: (b, 0, h))
    cache = pl.BlockSpec((1, past, LANES), lambda b, h: (b, 0, h))
    return new, cache


def _sb_sample(q, kn, vn, kc, vc):
    b, n, w = q.shape
    new, cache = _sample_specs(n, kc.shape[1])
    return pl.pallas_call(
        functools.partial(_sb_sample_kernel, tile=ATT_BLOCK),
        grid=(b, w // LANES),
        in_specs=[new, new, new, cache, cache],
        out_specs=new,
        out_shape=jax.ShapeDtypeStruct((b, n, w), F32),
        compiler_params=_params("arbitrary", "arbitrary"),
        name="sb_sample",
    )(q, kn, vn, kc, vc)


def _fox_sample(q, kn, vn, kc, vc, fq, fk):
    b, n, w = q.shape
    new, cache = _sample_specs(n, kc.shape[1])
    return pl.pallas_call(
        functools.partial(_fox_sample_kernel, tile=ATT_BLOCK),
        grid=(b, w // LANES),
        in_specs=[new, new, new, cache, cache,
                  pl.BlockSpec((1, n, fq.shape[-1]), lambda b, h: (b, 0, 0)),
                  pl.BlockSpec((1, 1, 2, fk.shape[-1]), lambda b, h: (b, h, 0, 0))],
        out_specs=new,
        out_shape=jax.ShapeDtypeStruct((b, n, w), F32),
        compiler_params=_params("arbitrary", "arbitrary"),
        name="fox_sample",
    )(q, kn, vn, kc, vc, fq, fk)


def _post_kernel(x_ref, osb_ref, ofx_ref, ada_ref, gsb_ref, gfx_ref, wo_ref, gffn_ref,
                 x2_ref, h_ref):
    x = x_ref[...]
    nb, tt, d = x.shape
    rows = nb * tt
    ada = ada_ref[...]
    o_sb = _rms(osb_ref[...], gsb_ref[...])
    o_fx = _rms(ofx_ref[...], gfx_ref[...])
    sb_w = o_sb.shape[-1]
    fx_w = o_fx.shape[-1]
    proj = (_dot(o_sb.reshape(rows, sb_w).astype(BF16), wo_ref[0:sb_w, :])
            + _dot(o_fx.reshape(rows, fx_w).astype(BF16), wo_ref[sb_w:sb_w + fx_w, :]))
    x2 = x + (1.0 + ada[:, 2:3, :]) * proj.reshape(nb, tt, d)
    x2_ref[...] = x2
    h = _rms(x2, gffn_ref[...]) * (1.0 + ada[:, 4:5, :]) + ada[:, 3:4, :]
    h_ref[...] = h.astype(BF16)


def _ffn_kernel(h_ref, x2_ref, ada_ref, wg_ref, wu_ref, wd_ref, gfin_ref, y_ref, acc_ref, *, final_norm):
    j = pl.program_id(2)
    nb, tt, d = h_ref.shape
    h = h_ref[...].reshape(nb * tt, d)
    g = _dot(h, wg_ref[...])
    u = _dot(h, wu_ref[...])
    act = (g * (1.0 / (1.0 + jnp.exp(-g))) * u).astype(BF16)
    part = _dot(act, wd_ref[...])

    @pl.when(j == 0)
    def _():
        acc_ref[...] = part

    @pl.when(j > 0)
    def _():
        acc_ref[...] += part

    @pl.when(j == pl.num_programs(2) - 1)
    def _():
        ada = ada_ref[...]
        x3 = x2_ref[...] + (1.0 + ada[:, 5:6, :]) * acc_ref[...].reshape(nb, tt, d)
        y_ref[...] = _rms(x3, gfin_ref[...]) if final_norm else x3


def _post(x, o_sb, o_fx, ada, g_sb, g_fx, wo16, g_ffn, wg16, wu16, wd16, g_fin, final_norm):
    b, t, d = x.shape
    tt = min(t, ROW_BLOCK)
    nb = max(1, ROW_BLOCK // tt)
    tok = lambda w: pl.BlockSpec((nb, tt, w), lambda i, j: (i, j, 0))
    const = lambda a: pl.BlockSpec(a.shape, lambda i, j: (0, 0))
    x2, h = pl.pallas_call(
        _post_kernel,
        grid=(b // nb, t // tt),
        in_specs=[tok(d), tok(o_sb.shape[-1]), tok(o_fx.shape[-1]),
                  pl.BlockSpec((nb, 6, d), lambda i, j: (i, 0, 0)),
                  const(g_sb), const(g_fx), const(wo16), const(g_ffn)],
        out_specs=[tok(d), tok(d)],
        out_shape=[jax.ShapeDtypeStruct((b, t, d), F32), jax.ShapeDtypeStruct((b, t, d), BF16)],
        compiler_params=_params("arbitrary", "arbitrary"),
        name="post_mixer",
    )(x, o_sb, o_fx, ada, g_sb, g_fx, wo16, g_ffn)

    d_ff = wg16.shape[1]
    n_ff = 2 if (d_ff // 2) % LANES == 0 else 1
    tf = d_ff // n_ff
    tok3 = lambda w: pl.BlockSpec((nb, tt, w), lambda i, j, f: (i, j, 0))
    return pl.pallas_call(
        functools.partial(_ffn_kernel, final_norm=final_norm),
        grid=(b // nb, t // tt, n_ff),
        in_specs=[tok3(d), tok3(d), pl.BlockSpec((nb, 6, d), lambda i, j, f: (i, 0, 0)),
                  pl.BlockSpec((d, tf), lambda i, j, f: (0, f)),
                  pl.BlockSpec((d, tf), lambda i, j, f: (0, f)),
                  pl.BlockSpec((tf, d), lambda i, j, f: (f, 0)),
                  pl.BlockSpec(g_fin.shape, lambda i, j, f: (0, 0))],
        out_specs=tok3(d),
        out_shape=jax.ShapeDtypeStruct((b, t, d), F32),
        scratch_shapes=[pltpu.VMEM((nb * tt, d), F32)],
        compiler_params=_params("arbitrary", "arbitrary", "arbitrary"),
        name="ffn",
    )(h, x2, ada, wg16, wu16, wd16, g_fin)


def _pad_lanes(a, n):
    return jnp.pad(a, ((0, 0), (0, n - a.shape[1])))


def kernel(x_prompt, x_sample, c_prompt, c_sample, cache_sb_k, cache_sb_v, cache_fox_k, cache_fox_v, cache_fox_logf, w_ada, b_ada, g_mix, w_in, b_f, g_sb_out, g_fox_out, w_o, g_ffn, w_gate, w_up, w_down, g_final):
    depth = w_ada.shape[0]
    bp, tp, d = x_prompt.shape
    bs, ts, _ = x_sample.shape
    past, n_sb, hd = cache_sb_k.shape[2:]
    n_fx = cache_fox_k.shape[3]
    assert hd == HEAD_DIM and n_sb % 2 == 0 and n_fx % 2 == 0
    assert tp % ROW_BLOCK == 0 and tp % ATT_BLOCK == 0 and ROW_BLOCK % ts == 0 and bs % (ROW_BLOCK // ts) == 0
    assert past % ATT_BLOCK == 0
    sb_w, fx_w = n_sb * hd, n_fx * hd
    qkv_cols = 3 * sb_w + 3 * fx_w
    att = min(FOX_BLOCK, tp)
    past_pad = -(-(past + ts) // (3 * LANES)) * (3 * LANES)
    row = lambda a: a.reshape(1, -1)

    xp, xs = x_prompt, x_sample
    outs = [[] for _ in range(10)]
    for l in range(depth):
        w16 = w_in[l][:, :qkv_cols].astype(BF16)
        wf16 = _pad_lanes(w_in[l][:, qkv_cols:], LANES).astype(BF16)
        bfp = _pad_lanes(row(b_f[l]), LANES)
        wo16, wg16 = w_o[l].astype(BF16), w_gate[l].astype(BF16)
        wu16, wd16 = w_up[l].astype(BF16), w_down[l].astype(BF16)

        c_all = jnp.concatenate([c_prompt, c_sample], axis=0)
        ada = _ada(c_all, w_ada[l].astype(BF16), row(b_ada[l])).reshape(bp + bs, 6, d)
        ada_p, ada_s = ada[:bp], ada[bp:]

        (q_sb, k_sb, v_sb, k_sb16, v_sb16, q_fx, k_fx, v_fx, k_fx16, v_fx16, lf) = _pre(
            xp, ada_p, row(g_mix[l]), w16, wf16, bfp, sb_w, fx_w, n_fx)
        o_sb = _sb_prompt(q_sb, k_sb16, v_sb16)
        f_cum = _cumsum_rows(jnp.swapaxes(lf, 1, 2).reshape(bp * n_fx, tp), ROW_BLOCK)
        fq = jnp.swapaxes(f_cum.reshape(bp, n_fx, tp), 1, 2)
        fk = jnp.swapaxes(f_cum.reshape(bp, n_fx // 2, 2, tp // att, att), 2, 3)
        o_fx = _fox_prompt(q_fx, k_fx16, v_fx16, fq, fk)
        xp = _post(xp, o_sb, o_fx, ada_p, row(g_sb_out[l]), row(g_fox_out[l]), wo16, row(g_ffn[l]),
                   wg16, wu16, wd16, row(g_final), l + 1 == depth)
        for dst, a in zip(outs[0:4], (k_sb, v_sb, k_fx, v_fx)):
            dst.append(a.reshape(bp, tp, -1, hd))
        outs[4].append(lf)

        (q_sb, k_sb, v_sb, k_sb16, v_sb16, q_fx, k_fx, v_fx, k_fx16, v_fx16, lf) = _pre(
            xs, ada_s, row(g_mix[l]), w16, wf16, bfp, sb_w, fx_w, n_fx)
        o_sb = _sb_sample(q_sb, k_sb16, v_sb16,
                          cache_sb_k[l].reshape(bs, past, sb_w), cache_sb_v[l].reshape(bs, past, sb_w))
        lf_all = jnp.concatenate([cache_fox_logf[l].astype(F32), lf], axis=1)
        lf_rows = _pad_lanes(jnp.swapaxes(lf_all, 1, 2).reshape(bs * n_fx, past + ts), past_pad)
        f_all = _cumsum_rows(lf_rows, past_pad // 3)
        fq = jnp.swapaxes(f_all.reshape(bs, n_fx, past_pad)[:, :, past:past + ts], 1, 2)
        fk = f_all.reshape(bs, n_fx // 2, 2, past_pad)
        o_fx = _fox_sample(q_fx, k_fx16, v_fx16,
                           cache_fox_k[l].reshape(bs, past, fx_w), cache_fox_v[l].reshape(bs, past, fx_w),
                           fq, fk)
        xs = _post(xs, o_sb, o_fx, ada_s, row(g_sb_out[l]), row(g_fox_out[l]), wo16, row(g_ffn[l]),
                   wg16, wu16, wd16, row(g_final), l + 1 == depth)
        for dst, a in zip(outs[5:9], (k_sb, v_sb, k_fx, v_fx)):
            dst.append(a.reshape(bs, ts, -1, hd))
        outs[9].append(lf)

    return (xp, xs) + tuple(jnp.stack(o) for o in outs)
```

```python
import functools

import jax
import jax.numpy as jnp
from jax import lax
from jax.experimental import pallas as pl
from jax.experimental.pallas import tpu as pltpu

F32 = jnp.float32
BF16 = jnp.bfloat16
EPS = 1e-6
HEAD_DIM = 64
LANES = 128
ROW_BLOCK = 512
ATT_BLOCK = 256
SB_BLOCKS_PER_STEP = 2
FOX_BLOCK = 512
NEG_BIG = -1e30
EXP_ZERO = -110.0
NORM_CHUNK = 2048
NORM_SLACK = 1.0 + 2.0 ** -8
VMEM_LIMIT = 56 * 1024 * 1024


def _params(*semantics):
    return pltpu.CompilerParams(dimension_semantics=semantics, vmem_limit_bytes=VMEM_LIMIT)


def _dot(a, b):
    return jnp.dot(a, b, preferred_element_type=F32)


def _dot_nt(a, b):
    return lax.dot_general(a, b, (((1,), (1,)), ((), ())), preferred_element_type=F32)


def _rms(x, g):
    return x * lax.rsqrt(jnp.mean(x * x, axis=-1, keepdims=True) + EPS) * g


def _split_bf16(x, parts):
    out = []
    for _ in range(parts - 1):
        p = x.astype(BF16)
        out.append(p)
        x = x - p.astype(F32)
    out.append(x.astype(BF16))
    return out


def _dot_split(x, m, parts):
    acc = None
    for p in _split_bf16(x, parts):
        d = _dot(p, m)
        acc = d if acc is None else acc + d
    return acc


def _keep_head(x, first, axis=1, fill=0.0):
    pos = lax.broadcasted_iota(jnp.int32, x.shape, axis)
    keep = (pos < HEAD_DIM) if first else (pos >= HEAD_DIM)
    return jnp.where(keep, x, jnp.full_like(x, fill))


def _per_head(col0, col1, shape):
    lane = lax.broadcasted_iota(jnp.int32, shape, 1)
    return jnp.where(lane < HEAD_DIM, col0, col1)


def _lanes_to(x, n):
    if n <= LANES:
        return x[:, :n]
    return jnp.concatenate([x] * (n // LANES), axis=1)


def _ada_kernel(c_ref, w_ref, b_ref, o_ref):
    c = c_ref[...]
    s = (c * (1.0 / (1.0 + jnp.exp(-c)))).astype(BF16)
    o_ref[...] = _dot(s, w_ref[...]) + b_ref[...]


def _ada(c, w16, b):
    rows, d = c.shape
    n = w16.shape[1]
    tn = 1024
    return pl.pallas_call(
        _ada_kernel,
        grid=(n // tn,),
        in_specs=[pl.BlockSpec((rows, d), lambda j: (0, 0)),
                  pl.BlockSpec((d, tn), lambda j: (0, j)),
                  pl.BlockSpec((1, tn), lambda j: (0, j))],
        out_specs=pl.BlockSpec((rows, tn), lambda j: (0, j)),
        out_shape=jax.ShapeDtypeStruct((rows, n), F32),
        compiler_params=_params("arbitrary"),
        name="ada",
    )(c, w16, b)


def _pre_kernel(x_ref, ada_ref, g_ref, w_ref, wt_ref, wf_ref, bf_ref,
                qsb_ref, ksb_ref, vsb_ref, ksb16_ref, vsb16_ref,
                qfx_ref, kfx_ref, vfx_ref, kfx16_ref, vfx16_ref, lf_ref, *, sb_w, fx_w, n_f, time_minor):
    x = x_ref[...]
    nb, tt, d = x.shape
    ada = ada_ref[...]
    h = _rms(x, g_ref[...]) * (1.0 + ada[:, 1:2, :]) + ada[:, 0:1, :]
    h = h.reshape(nb * tt, d).astype(BF16)
    q_scale = HEAD_DIM ** -0.5

    def proj(lo, width):
        return _dot(h, w_ref[:, lo:lo + width]).reshape(nb, tt, width)

    def proj_kv(lo, width):
        if time_minor:
            return _dot_nt(wt_ref[lo:lo + width, :], h)[None]
        return proj(lo, width)

    base = 3 * sb_w
    qsb_ref[...] = (proj(0, sb_w) * q_scale).astype(BF16)
    qfx_ref[...] = (proj(base, fx_w) * q_scale).astype(BF16)
    for lo, width, out, out16 in ((sb_w, sb_w, ksb_ref, ksb16_ref), (2 * sb_w, sb_w, vsb_ref, vsb16_ref),
                                  (base + fx_w, fx_w, kfx_ref, kfx16_ref),
                                  (base + 2 * fx_w, fx_w, vfx_ref, vfx16_ref)):
        r = proj_kv(lo, width)
        out[...] = r
        out16[...] = r.astype(BF16)
    u = _dot(h, wf_ref[...]) + bf_ref[...]
    lf = jnp.minimum(u, 0.0) - jnp.log(1.0 + jnp.exp(-jnp.abs(u)))
    lf_ref[...] = lf[:, :n_f].reshape(nb, tt, n_f)


def _pre(x, ada, g, w16, w16t, wf16, bfp, sb_w, fx_w, n_f, time_minor):
    b, t, d = x.shape
    tt = min(t, ROW_BLOCK)
    nb = max(1, ROW_BLOCK // tt)
    assert nb == 1 or not time_minor
    grid = (b // nb, t // tt)
    tok = lambda w: pl.BlockSpec((nb, tt, w), lambda i, j: (i, j, 0))
    const = lambda a: pl.BlockSpec(a.shape, lambda i, j: (0, 0))
    shape = lambda w, dt: jax.ShapeDtypeStruct((b, t, w), dt)
    if time_minor:
        kv = lambda w: pl.BlockSpec((1, w, tt), lambda i, j: (i, 0, j))
        kv_shape = lambda w, dt: jax.ShapeDtypeStruct((b, w, t), dt)
    else:
        kv, kv_shape = tok, shape
    return pl.pallas_call(
        functools.partial(_pre_kernel, sb_w=sb_w, fx_w=fx_w, n_f=n_f, time_minor=time_minor),
        grid=grid,
        in_specs=[tok(d), pl.BlockSpec((nb, 6, d), lambda i, j: (i, 0, 0)),
                  const(g), const(w16), const(w16t), const(wf16), const(bfp)],
        out_specs=[tok(sb_w)] + [kv(sb_w)] * 4 + [tok(fx_w)] + [kv(fx_w)] * 4 + [tok(n_f)],
        out_shape=[shape(sb_w, BF16), kv_shape(sb_w, F32), kv_shape(sb_w, F32), kv_shape(sb_w, BF16),
                   kv_shape(sb_w, BF16),
                   shape(fx_w, BF16), kv_shape(fx_w, F32), kv_shape(fx_w, F32), kv_shape(fx_w, BF16),
                   kv_shape(fx_w, BF16),
                   shape(n_f, F32)],
        compiler_params=_params("arbitrary", "arbitrary"),
        name="pre_mixer",
    )(x, ada, g, w16, w16t, wf16, bfp)


def _cumsum_kernel(x_ref, o_ref, carry_ref):
    @pl.when(pl.program_id(0) == 0)
    def _():
        carry_ref[...] = jnp.zeros_like(carry_ref)

    x = x_ref[...]
    tc = x.shape[1]
    r = lax.broadcasted_iota(jnp.int32, (tc, tc), 0)
    c = lax.broadcasted_iota(jnp.int32, (tc, tc), 1)
    upper = jnp.where(r <= c, 1.0, 0.0).astype(BF16)
    carry = carry_ref[...]
    o_ref[...] = _dot_split(x, upper, 3) + _lanes_to(carry, tc)
    carry_ref[...] = carry + _dot_split(x, jnp.ones((tc, LANES), BF16), 3)


def _cumsum_rows(x, tc):
    rows, t = x.shape
    return pl.pallas_call(
        _cumsum_kernel,
        grid=(t // tc,),
        in_specs=[pl.BlockSpec((rows, tc), lambda j: (0, j))],
        out_specs=pl.BlockSpec((rows, tc), lambda j: (0, j)),
        out_shape=jax.ShapeDtypeStruct((rows, t), F32),
        scratch_shapes=[pltpu.VMEM((rows, LANES), F32)],
        compiler_params=_params("arbitrary"),
        name="cumsum_time",
    )(x)


def _tri_strict(n):
    r = lax.broadcasted_iota(jnp.int32, (n, n), 0)
    c = lax.broadcasted_iota(jnp.int32, (n, n), 1)
    return jnp.where(r > c, 1.0, 0.0).astype(BF16)


def _causal(tq, tk, strict):
    r = lax.broadcasted_iota(jnp.int32, (tq, tk), 0)
    c = lax.broadcasted_iota(jnp.int32, (tq, tk), 1)
    return (c < r) if strict else (c <= r)


def _sb_tiles(chains, mask):
    tk = chains[0][1].shape[1]
    tri = _tri_strict(tk)
    wide = tk % LANES == 0
    if wide:
        tri2 = jnp.concatenate([tri, tri], axis=0)
    zs = [_dot(q, k) for q, k, _, _ in chains]
    staged = []
    for z in zs:
        log_beta = jnp.minimum(z, 0.0) - jnp.log(1.0 + jnp.exp(-jnp.abs(z)))
        log_keep = log_beta - z
        if mask is not None:
            log_keep = jnp.where(mask, log_keep, 0.0)
        hi, lo = _split_bf16(log_keep, 2)
        staged.append((log_beta, log_keep[:, 0:1], hi, lo))
    if wide:
        betweens = [_dot(jnp.concatenate([hi, lo], axis=1), tri2) for _, _, hi, lo in staged]
    else:
        betweens = [_dot(hi, tri) + _dot(lo, tri) for _, _, hi, lo in staged]
    weights = []
    for (log_beta, _, _, _), between, (_, _, _, carry) in zip(staged, betweens, chains):
        a = jnp.exp(log_beta + between + carry)
        if mask is not None:
            a = jnp.where(mask, a, 0.0)
        weights.append(a.astype(BF16))
    pvs = [_dot_nt(a, v) for a, (_, _, v, _) in zip(weights, chains)]
    return [(pv, carry + (between[:, 0:1] + first))
            for pv, between, (_, first, _, _), (_, _, _, carry) in zip(pvs, betweens, staged, chains)]


def _sb_start(rows, groups):
    col = jnp.zeros((rows, 1), F32)
    return (col, col, jnp.zeros((rows, LANES), F32)) * groups


def _sb_visit(qs, kvs, state, mask):
    chains = []
    for g, ((q0, q1), (k, v)) in enumerate(zip(qs, kvs)):
        chains.append((q0, k, _keep_head(v, True, 0), state[3 * g]))
        chains.append((q1, k, _keep_head(v, False, 0), state[3 * g + 1]))
    res = _sb_tiles(chains, mask)
    out = ()
    for g in range(len(qs)):
        (pv0, c0), (pv1, c1) = res[2 * g], res[2 * g + 1]
        out += (c0, c1, state[3 * g + 2] + pv0 + pv1)
    return out


def _sb_sweep(visit, lefts, state):
    groups = len(lefts)

    def live(s):
        alive = jnp.bool_(False)
        for g in range(groups):
            more = jnp.max(jnp.maximum(s[1 + 3 * g], s[2 + 3 * g])) > EXP_ZERO
            alive = jnp.logical_or(alive, jnp.logical_and(s[0] < lefts[g], more))
        return alive

    def step(s):
        i, old = s[0], s[1:]
        new = visit([jnp.maximum(lefts[g] - 1 - i, 0) for g in range(groups)], old, None)
        out = (i + 1,)
        for g in range(groups):
            inside = i < lefts[g]
            out += tuple(jnp.where(inside, n, o) for n, o in zip(new[3 * g:3 * g + 3], old[3 * g:3 * g + 3]))
        return out

    return lax.while_loop(live, step, (jnp.int32(0),) + tuple(state))[1:]


def _head_column(f, h):
    lane = lax.broadcasted_iota(jnp.int32, f.shape, 1)
    return jnp.sum(jnp.where(lane == h, f, 0.0), axis=1, keepdims=True)


def _sb_prompt_kernel(q_ref, k_ref, v_ref, o_ref, *, tile, blocks):
    first = pl.program_id(2) * blocks
    qs = []
    for j in range(blocks):
        q = q_ref[0, j * tile:(j + 1) * tile, :]
        qs.append((_keep_head(q, True), _keep_head(q, False)))

    def visit(kbs, state, mask):
        kvs = []
        for j in range(blocks):
            start = pl.multiple_of(kbs[j] * tile, tile)
            kvs.append((k_ref[0, :, pl.ds(start, tile)], v_ref[0, :, pl.ds(start, tile)]))
        return _sb_visit(qs, kvs, state, mask)

    lefts = [first + j for j in range(blocks)]
    state = visit(lefts, _sb_start(tile, blocks), _causal(tile, tile, True))
    state = _sb_sweep(visit, lefts, state)
    for j in range(blocks):
        o_ref[0, j * tile:(j + 1) * tile, :] = state[2 + 3 * j]


def _head_norms_sq(x16):
    x = x16.astype(F32)
    r = lax.broadcasted_iota(jnp.int32, (LANES, LANES), 0)
    c = lax.broadcasted_iota(jnp.int32, (LANES, LANES), 1)
    same_head = jnp.where((r < HEAD_DIM) == (c < HEAD_DIM), 1.0, 0.0).astype(BF16)
    return _dot_split(x * x, same_head, 2)


def _fox_prompt_kernel(q_ref, k_ref, v_ref, fq_ref, fk_ref, o_ref, kmax_ref, *, tile):
    hp = pl.program_id(1)
    qi = pl.program_id(2)
    q = q_ref[0]
    q0, q1 = _keep_head(q, True), _keep_head(q, False)
    fq = fq_ref[0]
    fq0, fq1 = _head_column(fq, 2 * hp), _head_column(fq, 2 * hp + 1)
    shape = (tile, LANES)

    @pl.when(qi == 0)
    def _():
        width = min(NORM_CHUNK, k_ref.shape[2])

        def chunk(i, best):
            kc = k_ref[0, :, pl.ds(pl.multiple_of(i * width, width), width)].astype(F32)
            sq = kc * kc
            return (jnp.maximum(best[0], jnp.sum(sq[:HEAD_DIM], axis=0, keepdims=True)),
                    jnp.maximum(best[1], jnp.sum(sq[HEAD_DIM:], axis=0, keepdims=True)))

        zero = jnp.zeros((1, width), F32)
        best = lax.fori_loop(0, k_ref.shape[2] // width, chunk, (zero, zero))
        kmax_ref[...] = _per_head(jnp.max(best[0], axis=1, keepdims=True),
                                  jnp.max(best[1], axis=1, keepdims=True), (1, LANES))

    reach = (jnp.sqrt(_head_norms_sq(q)) * jnp.sqrt(kmax_ref[...]) * NORM_SLACK
             + _per_head(fq0, fq1, shape))

    def last_fk(kb):
        fk = fk_ref[0, 0, jnp.maximum(kb, 0)]
        return _per_head(fk[0:1, tile - 1:tile], fk[1:2, tile - 1:tile], (1, LANES))

    def scores(kb, mask):
        k = k_ref[0, :, pl.ds(pl.multiple_of(kb * tile, tile), tile)]
        fk = fk_ref[0, 0, kb]
        u0 = _dot(q0, k) - fk[0:1, :]
        u1 = _dot(q1, k) - fk[1:2, :]
        if mask is not None:
            u0, u1 = jnp.where(mask, u0, NEG_BIG), jnp.where(mask, u1, NEG_BIG)
        return u0, u1

    causal = _causal(tile, tile, False)

    def row_max(kb, m0, m1, mask):
        u0, u1 = scores(kb, mask)
        return (jnp.maximum(m0, jnp.max(u0, axis=1, keepdims=True) + fq0),
                jnp.maximum(m1, jnp.max(u1, axis=1, keepdims=True) + fq1))

    def best_gap(kb, m0, m1):
        return jnp.max(reach - _per_head(m0, m1, shape) - last_fk(kb))

    def max_step(s):
        i = s[0]
        m0, m1 = row_max(qi - 1 - i, s[2], s[3], None)
        return i + 1, best_gap(qi - 2 - i, m0, m1), m0, m1

    neg = jnp.full((tile, 1), NEG_BIG, F32)
    m0, m1 = row_max(qi, neg, neg, causal)
    _, _, m0, m1 = lax.while_loop(lambda s: jnp.logical_and(s[0] < qi, s[1] > 0.0), max_step,
                                  (jnp.int32(0), best_gap(qi - 1, m0, m1), m0, m1))

    c0, c1 = m0 - fq0, m1 - fq1
    gap_rows = jnp.max(reach - _per_head(m0, m1, shape), axis=0, keepdims=True)

    def weigh(kb, acc0, acc1, mask):
        u0, u1 = scores(kb, mask)
        v = v_ref[0, :, pl.ds(pl.multiple_of(kb * tile, tile), tile)]
        acc0 = acc0 + _dot_nt(jnp.exp(u0 - c0).astype(BF16), _keep_head(v, True, 0, 1.0))
        acc1 = acc1 + _dot_nt(jnp.exp(u1 - c1).astype(BF16), _keep_head(v, False, 0, 1.0))
        return acc0, acc1

    def sum_step(s):
        i = s[0]
        acc0, acc1 = weigh(qi - 1 - i, s[2], s[3], None)
        return i + 1, jnp.max(gap_rows - last_fk(qi - 2 - i)), acc0, acc1

    zero = jnp.zeros(shape, F32)
    acc0, acc1 = weigh(qi, zero, zero, causal)
    _, _, acc0, acc1 = lax.while_loop(lambda s: jnp.logical_and(s[0] < qi, s[1] > EXP_ZERO), sum_step,
                                      (jnp.int32(0), jnp.max(gap_rows - last_fk(qi - 1)), acc0, acc1))
    lane = lax.broadcasted_iota(jnp.int32, shape, 1)
    half = LANES // 2
    o_ref[0] = jnp.where(lane < HEAD_DIM, acc0 / pltpu.roll(acc0, half, 1), acc1 / pltpu.roll(acc1, half, 1))


def _prompt_specs(t, tile):
    qspec = pl.BlockSpec((1, tile, LANES), lambda b, h, i: (b, i, h))
    kvspec = pl.BlockSpec((1, LANES, t), lambda b, h, i: (b, h, 0))
    return qspec, kvspec


def _sb_prompt(q, k, v):
    b, t, w = q.shape
    tile = ATT_BLOCK
    blocks = SB_BLOCKS_PER_STEP if t % (tile * SB_BLOCKS_PER_STEP) == 0 else 1
    qspec, kvspec = _prompt_specs(t, tile * blocks)
    return pl.pallas_call(
        functools.partial(_sb_prompt_kernel, tile=tile, blocks=blocks),
        grid=(b, w // LANES, t // (tile * blocks)),
        in_specs=[qspec, kvspec, kvspec],
        out_specs=qspec,
        out_shape=jax.ShapeDtypeStruct((b, t, w), F32),
        compiler_params=_params("arbitrary", "arbitrary", "arbitrary"),
        name="sb_prompt",
    )(q, k, v)


def _fox_prompt(q, k, v, fq, fk):
    b, t, w = q.shape
    tile = fk.shape[-1]
    qspec, kvspec = _prompt_specs(t, tile)
    n_f = fq.shape[-1]
    return pl.pallas_call(
        functools.partial(_fox_prompt_kernel, tile=tile),
        grid=(b, w // LANES, t // tile),
        in_specs=[qspec, kvspec, kvspec,
                  pl.BlockSpec((1, tile, n_f), lambda b, h, i: (b, i, 0)),
                  pl.BlockSpec((1, 1, t // tile, 2, tile), lambda b, h, i: (b, h, 0, 0, 0))],
        out_specs=qspec,
        out_shape=jax.ShapeDtypeStruct((b, t, w), F32),
        scratch_shapes=[pltpu.VMEM((1, LANES), F32)],
        compiler_params=_params("arbitrary", "arbitrary", "arbitrary"),
        name="fox_prompt",
    )(q, k, v, fq, fk)


def _group_rows(g):
    return slice(g * LANES, (g + 1) * LANES)


def _sb_sample_kernel(q_ref, kn_ref, vn_ref, kc_ref, vc_ref, o_ref, *, tile):
    n, width = q_ref.shape[1:]
    groups = width // LANES
    past = kc_ref.shape[2]
    qs = []
    for g in range(groups):
        q = q_ref[0, :, _group_rows(g)]
        qs.append((_keep_head(q, True), _keep_head(q, False)))

    def visit(kbs, state, mask):
        kvs = []
        for g in range(groups):
            if mask is not None:
                kvs.append((kn_ref[0, _group_rows(g), :], vn_ref[0, _group_rows(g), :]))
            else:
                cols = pl.ds(pl.multiple_of(kbs[g] * tile, tile), tile)
                kvs.append((kc_ref[0, _group_rows(g), cols].astype(BF16),
                            vc_ref[0, _group_rows(g), cols].astype(BF16)))
        return _sb_visit(qs, kvs, state, mask)

    state = visit(None, _sb_start(n, groups), _causal(n, n, True))
    state = _sb_sweep(visit, [past // tile] * groups, state)
    for g in range(groups):
        o_ref[0, :, _group_rows(g)] = state[2 + 3 * g]


def _fox_sample_kernel(q_ref, kn_ref, vn_ref, kc_ref, vc_ref, fq_ref, fk_ref, o_ref):
    n, width = q_ref.shape[1:]
    groups = width // LANES
    past = kc_ref.shape[2]
    fq = fq_ref[0]
    causal = _causal(n, n, False)
    scored = []
    for g in range(groups):
        q = q_ref[0, :, _group_rows(g)]
        kn = kn_ref[0, _group_rows(g), :]
        kc = kc_ref[0, _group_rows(g), :].astype(BF16)
        for h in range(2):
            qh = _keep_head(q, h == 0)
            fk = fk_ref[0, g, h:h + 1, :]
            u_new = jnp.where(causal, _dot(qh, kn) - fk[:, past:past + n], NEG_BIG)
            u_old = _dot(qh, kc) - fk[:, 0:past]
            scored.append((u_new, u_old, fq[:, 2 * g + h:2 * g + h + 1]))
    weights = []
    for u_new, u_old, fq_col in scored:
        m = jnp.maximum(jnp.max(u_new, axis=1, keepdims=True), jnp.max(u_old, axis=1, keepdims=True)) + fq_col
        c = m - fq_col
        weights.append((jnp.exp(u_new - c).astype(BF16), jnp.exp(u_old - c).astype(BF16)))
    lane = lax.broadcasted_iota(jnp.int32, (n, LANES), 1)
    half = LANES // 2
    for g in range(groups):
        vn = vn_ref[0, _group_rows(g), :]
        vc = vc_ref[0, _group_rows(g), :].astype(BF16)
        accs = []
        for h in range(2):
            p_new, p_old = weights[2 * g + h]
            accs.append(_dot_nt(p_new, _keep_head(vn, h == 0, 0, 1.0))
                        + _dot_nt(p_old, _keep_head(vc, h == 0, 0, 1.0)))
        o_ref[0, :, _group_rows(g)] = jnp.where(lane < HEAD_DIM, accs[0] / pltpu.roll(accs[0], half, 1),
                                                accs[1] / pltpu.roll(accs[1], half, 1))


def _sample_specs(n, width, past):
    qspec = pl.BlockSpec((1, n, width), lambda b: (b, 0, 0))
    new = pl.BlockSpec((1, width, n), lambda b: (b, 0, 0))
    cache = pl.BlockSpec((1, width, past), lambda b: (b, 0, 0))
    return qspec, new, cache


def _sb_sample(q, kn, vn, kc, vc):
    b, n, w = q.shape
    qspec, new, cache = _sample_specs(n, w, kc.shape[2])
    return pl.pallas_call(
        functools.partial(_sb_sample_kernel, tile=ATT_BLOCK),
        grid=(b,),
        in_specs=[qspec, new, new, cache, cache],
        out_specs=qspec,
        out_shape=jax.ShapeDtypeStruct((b, n, w), F32),
        compiler_params=_params("arbitrary"),
        name="sb_sample",
    )(q, kn, vn, kc, vc)


def _fox_sample(q, kn, vn, kc, vc, fq, fk):
    b, n, w = q.shape
    qspec, new, cache = _sample_specs(n, w, kc.shape[2])
    return pl.pallas_call(
        _fox_sample_kernel,
        grid=(b,),
        in_specs=[qspec, new, new, cache, cache,
                  pl.BlockSpec((1, n, fq.shape[-1]), lambda b: (b, 0, 0)),
                  pl.BlockSpec((1,) + fk.shape[1:], lambda b: (b, 0, 0, 0))],
        out_specs=qspec,
        out_shape=jax.ShapeDtypeStruct((b, n, w), F32),
        compiler_params=_params("arbitrary"),
        name="fox_sample",
    )(q, kn, vn, kc, vc, fq, fk)


def _post_kernel(x_ref, osb_ref, ofx_ref, ada_ref, gsb_ref, gfx_ref, wo_ref, gffn_ref,
                 x2_ref, h_ref):
    x = x_ref[...]
    nb, tt, d = x.shape
    rows = nb * tt
    ada = ada_ref[...]
    o_sb = _rms(osb_ref[...], gsb_ref[...])
    o_fx = _rms(ofx_ref[...], gfx_ref[...])
    sb_w = o_sb.shape[-1]
    fx_w = o_fx.shape[-1]
    proj = (_dot(o_sb.reshape(rows, sb_w).astype(BF16), wo_ref[0:sb_w, :])
            + _dot(o_fx.reshape(rows, fx_w).astype(BF16), wo_ref[sb_w:sb_w + fx_w, :]))
    x2 = x + (1.0 + ada[:, 2:3, :]) * proj.reshape(nb, tt, d)
    x2_ref[...] = x2
    h = _rms(x2, gffn_ref[...]) * (1.0 + ada[:, 4:5, :]) + ada[:, 3:4, :]
    h_ref[...] = h.astype(BF16)


def _ffn_kernel(h_ref, x2_ref, ada_ref, wg_ref, wu_ref, wd_ref, gfin_ref, y_ref, acc_ref, *, final_norm):
    j = pl.program_id(2)
    nb, tt, d = h_ref.shape
    h = h_ref[...].reshape(nb * tt, d)
    g = _dot(h, wg_ref[...])
    u = _dot(h, wu_ref[...])
    act = (g * (1.0 / (1.0 + jnp.exp(-g))) * u).astype(BF16)
    part = _dot(act, wd_ref[...])

    @pl.when(j == 0)
    def _():
        acc_ref[...] = part

    @pl.when(j > 0)
    def _():
        acc_ref[...] += part

    @pl.when(j == pl.num_programs(2) - 1)
    def _():
        ada = ada_ref[...]
        x3 = x2_ref[...] + (1.0 + ada[:, 5:6, :]) * acc_ref[...].reshape(nb, tt, d)
        y_ref[...] = _rms(x3, gfin_ref[...]) if final_norm else x3


def _post(x, o_sb, o_fx, ada, g_sb, g_fx, wo16, g_ffn, wg16, wu16, wd16, g_fin, final_norm):
    b, t, d = x.shape
    tt = min(t, ROW_BLOCK)
    nb = max(1, ROW_BLOCK // tt)
    tok = lambda w: pl.BlockSpec((nb, tt, w), lambda i, j: (i, j, 0))
    const = lambda a: pl.BlockSpec(a.shape, lambda i, j: (0, 0))
    x2, h = pl.pallas_call(
        _post_kernel,
        grid=(b // nb, t // tt),
        in_specs=[tok(d), tok(o_sb.shape[-1]), tok(o_fx.shape[-1]),
                  pl.BlockSpec((nb, 6, d), lambda i, j: (i, 0, 0)),
                  const(g_sb), const(g_fx), const(wo16), const(g_ffn)],
        out_specs=[tok(d), tok(d)],
        out_shape=[jax.ShapeDtypeStruct((b, t, d), F32), jax.ShapeDtypeStruct((b, t, d), BF16)],
        compiler_params=_params("arbitrary", "arbitrary"),
        name="post_mixer",
    )(x, o_sb, o_fx, ada, g_sb, g_fx, wo16, g_ffn)

    d_ff = wg16.shape[1]
    n_ff = 2 if (d_ff // 2) % LANES == 0 else 1
    tf = d_ff // n_ff
    tok3 = lambda w: pl.BlockSpec((nb, tt, w), lambda i, j, f: (i, j, 0))
    return pl.pallas_call(
        functools.partial(_ffn_kernel, final_norm=final_norm),
        grid=(b // nb, t // tt, n_ff),
        in_specs=[tok3(d), tok3(d), pl.BlockSpec((nb, 6, d), lambda i, j, f: (i, 0, 0)),
                  pl.BlockSpec((d, tf), lambda i, j, f: (0, f)),
                  pl.BlockSpec((d, tf), lambda i, j, f: (0, f)),
                  pl.BlockSpec((tf, d), lambda i, j, f: (f, 0)),
                  pl.BlockSpec(g_fin.shape, lambda i, j, f: (0, 0))],
        out_specs=tok3(d),
        out_shape=jax.ShapeDtypeStruct((b, t, d), F32),
        scratch_shapes=[pltpu.VMEM((nb * tt, d), F32)],
        compiler_params=_params("arbitrary", "arbitrary", "arbitrary"),
        name="ffn",
    )(h, x2, ada, wg16, wu16, wd16, g_fin)


def _pad_lanes(a, n):
    return jnp.pad(a, ((0, 0), (0, n - a.shape[1])))


def kernel(x_prompt, x_sample, c_prompt, c_sample, cache_sb_k, cache_sb_v, cache_fox_k, cache_fox_v, cache_fox_logf, w_ada, b_ada, g_mix, w_in, b_f, g_sb_out, g_fox_out, w_o, g_ffn, w_gate, w_up, w_down, g_final):
    depth = w_ada.shape[0]
    bp, tp, d = x_prompt.shape
    bs, ts, _ = x_sample.shape
    past, n_sb, hd = cache_sb_k.shape[2:]
    n_fx = cache_fox_k.shape[3]
    assert hd == HEAD_DIM and n_sb % 2 == 0 and n_fx % 2 == 0
    assert tp % ROW_BLOCK == 0 and tp % ATT_BLOCK == 0 and ROW_BLOCK % ts == 0 and bs % (ROW_BLOCK // ts) == 0
    assert past % ATT_BLOCK == 0
    sb_w, fx_w = n_sb * hd, n_fx * hd
    qkv_cols = 3 * sb_w + 3 * fx_w
    att = min(FOX_BLOCK, tp)
    past_pad = -(-(past + ts) // (3 * LANES)) * (3 * LANES)
    row = lambda a: a.reshape(1, -1)

    xp, xs = x_prompt, x_sample
    outs = [[] for _ in range(10)]
    for l in range(depth):
        w16 = w_in[l][:, :qkv_cols].astype(BF16)
        w16t = jnp.swapaxes(w_in[l], 0, 1)[:qkv_cols].astype(BF16)
        wf16 = _pad_lanes(w_in[l][:, qkv_cols:], LANES).astype(BF16)
        bfp = _pad_lanes(row(b_f[l]), LANES)
        wo16, wg16 = w_o[l].astype(BF16), w_gate[l].astype(BF16)
        wu16, wd16 = w_up[l].astype(BF16), w_down[l].astype(BF16)

        c_all = jnp.concatenate([c_prompt, c_sample], axis=0)
        ada = _ada(c_all, w_ada[l].astype(BF16), row(b_ada[l])).reshape(bp + bs, 6, d)
        ada_p, ada_s = ada[:bp], ada[bp:]

        (q_sb, k_sb, v_sb, k_sb16, v_sb16, q_fx, k_fx, v_fx, k_fx16, v_fx16, lf) = _pre(
            xp, ada_p, row(g_mix[l]), w16, w16t, wf16, bfp, sb_w, fx_w, n_fx, True)
        o_sb = _sb_prompt(q_sb, k_sb16, v_sb16)
        f_cum = _cumsum_rows(jnp.swapaxes(lf, 1, 2).reshape(bp * n_fx, tp), ROW_BLOCK)
        fq = jnp.swapaxes(f_cum.reshape(bp, n_fx, tp), 1, 2)
        fk = jnp.swapaxes(f_cum.reshape(bp, n_fx // 2, 2, tp // att, att), 2, 3)
        o_fx = _fox_prompt(q_fx, k_fx16, v_fx16, fq, fk)
        xp = _post(xp, o_sb, o_fx, ada_p, row(g_sb_out[l]), row(g_fox_out[l]), wo16, row(g_ffn[l]),
                   wg16, wu16, wd16, row(g_final), l + 1 == depth)
        for dst, a in zip(outs[0:4], (k_sb, v_sb, k_fx, v_fx)):
            dst.append(jnp.swapaxes(a, 1, 2).reshape(bp, tp, -1, hd))
        outs[4].append(lf)

        (q_sb, k_sb, v_sb, k_sb16, v_sb16, q_fx, k_fx, v_fx, k_fx16, v_fx16, lf) = _pre(
            xs, ada_s, row(g_mix[l]), w16, w16t, wf16, bfp, sb_w, fx_w, n_fx, False)
        time_minor = lambda a: jnp.swapaxes(a.reshape(a.shape[0], a.shape[1], -1), 1, 2)
        o_sb = _sb_sample(q_sb, time_minor(k_sb16), time_minor(v_sb16),
                          time_minor(cache_sb_k[l]), time_minor(cache_sb_v[l]))
        lf_all = jnp.concatenate([cache_fox_logf[l].astype(F32), lf], axis=1)
        lf_rows = _pad_lanes(jnp.swapaxes(lf_all, 1, 2).reshape(bs * n_fx, past + ts), past_pad)
        f_all = _cumsum_rows(lf_rows, past_pad // 3)
        fq = jnp.swapaxes(f_all.reshape(bs, n_fx, past_pad)[:, :, past:past + ts], 1, 2)
        fk = f_all.reshape(bs, n_fx // 2, 2, past_pad)
        o_fx = _fox_sample(q_fx, time_minor(k_fx16), time_minor(v_fx16),
                           time_minor(cache_fox_k[l]), time_minor(cache_fox_v[l]), fq, fk)
        xs = _post(xs, o_sb, o_fx, ada_s, row(g_sb_out[l]), row(g_fox_out[l]), wo16, row(g_ffn[l]),
                   wg16, wu16, wd16, row(g_final), l + 1 == depth)
        for dst, a in zip(outs[5:9], (k_sb, v_sb, k_fx, v_fx)):
            dst.append(a.reshape(bs, ts, -1, hd))
        outs[9].append(lf)

    return (xp, xs) + tuple(jnp.stack(o) for o in outs)
```

```python
import functools

import jax
import jax.numpy as jnp
from jax import lax
from jax.experimental import pallas as pl
from jax.experimental.pallas import tpu as pltpu

F32 = jnp.float32
BF16 = jnp.bfloat16
EPS = 1e-6
HEAD_DIM = 64
LANES = 128
ROW_BLOCK = 512
ATT_BLOCK = 256
SB_BLOCKS_PER_STEP = 2
FOX_BLOCK = 512
NEG_BIG = -1e30
EXP_ZERO = -105.0
NORM_CHUNK = 2048
NORM_SLACK = 1.0 + 2.0 ** -8
VMEM_LIMIT = 56 * 1024 * 1024


def _params(*semantics):
    return pltpu.CompilerParams(dimension_semantics=semantics, vmem_limit_bytes=VMEM_LIMIT)


def _dot(a, b):
    return jnp.dot(a, b, preferred_element_type=F32)


def _dot_nt(a, b):
    return lax.dot_general(a, b, (((1,), (1,)), ((), ())), preferred_element_type=F32)


def _rms(x, g):
    return x * lax.rsqrt(jnp.mean(x * x, axis=-1, keepdims=True) + EPS) * g


def _split_bf16(x, parts):
    out = []
    for _ in range(parts - 1):
        p = x.astype(BF16)
        out.append(p)
        x = x - p.astype(F32)
    out.append(x.astype(BF16))
    return out


def _dot_split(x, m, parts):
    acc = None
    for p in _split_bf16(x, parts):
        d = _dot(p, m)
        acc = d if acc is None else acc + d
    return acc


def _keep_head(x, first, axis=1, fill=0.0):
    pos = lax.broadcasted_iota(jnp.int32, x.shape, axis)
    keep = (pos < HEAD_DIM) if first else (pos >= HEAD_DIM)
    return jnp.where(keep, x, jnp.full_like(x, fill))


def _per_head(col0, col1, shape):
    lane = lax.broadcasted_iota(jnp.int32, shape, 1)
    return jnp.where(lane < HEAD_DIM, col0, col1)


def _lanes_to(x, n):
    if n <= LANES:
        return x[:, :n]
    return jnp.concatenate([x] * (n // LANES), axis=1)


def _ada_kernel(c_ref, w_ref, b_ref, o_ref):
    c = c_ref[...]
    s = (c * (1.0 / (1.0 + jnp.exp(-c)))).astype(BF16)
    o_ref[...] = _dot(s, w_ref[...].astype(BF16)) + b_ref[...]


def _ada(c, w, b):
    rows, d = c.shape
    n = w.shape[1]
    tn = 1024
    return pl.pallas_call(
        _ada_kernel,
        grid=(n // tn,),
        in_specs=[pl.BlockSpec((rows, d), lambda j: (0, 0)),
                  pl.BlockSpec((d, tn), lambda j: (0, j)),
                  pl.BlockSpec((1, tn), lambda j: (0, j))],
        out_specs=pl.BlockSpec((rows, tn), lambda j: (0, j)),
        out_shape=jax.ShapeDtypeStruct((rows, n), F32),
        compiler_params=_params("arbitrary"),
        name="ada",
    )(c, w, b)


def _pre_kernel(x_ref, ada_ref, g_ref, w_ref, wt_ref, wf_ref, bf_ref,
                qsb_ref, ksb_ref, vsb_ref, ksb16_ref, vsb16_ref,
                qfx_ref, kfx_ref, vfx_ref, kfx16_ref, vfx16_ref, lf_ref, *, sb_w, fx_w, n_f, time_minor):
    x = x_ref[...]
    nb, tt, d = x.shape
    ada = ada_ref[...]
    h = _rms(x, g_ref[...]) * (1.0 + ada[:, 1:2, :]) + ada[:, 0:1, :]
    h = h.reshape(nb * tt, d).astype(BF16)
    q_scale = HEAD_DIM ** -0.5

    def proj(lo, width):
        return _dot(h, w_ref[:, lo:lo + width]).reshape(nb, tt, width)

    def proj_kv(lo, width):
        if time_minor:
            return _dot_nt(wt_ref[lo:lo + width, :], h)[None]
        return proj(lo, width)

    base = 3 * sb_w
    qsb_ref[...] = (proj(0, sb_w) * q_scale).astype(BF16)
    qfx_ref[...] = (proj(base, fx_w) * q_scale).astype(BF16)
    for lo, width, out, out16 in ((sb_w, sb_w, ksb_ref, ksb16_ref), (2 * sb_w, sb_w, vsb_ref, vsb16_ref),
                                  (base + fx_w, fx_w, kfx_ref, kfx16_ref),
                                  (base + 2 * fx_w, fx_w, vfx_ref, vfx16_ref)):
        r = proj_kv(lo, width)
        out[...] = r
        out16[...] = r.astype(BF16)
    u = _dot(h, wf_ref[...]) + bf_ref[...]
    lf = jnp.minimum(u, 0.0) - jnp.log(1.0 + jnp.exp(-jnp.abs(u)))
    lf_ref[...] = lf[:, :n_f].reshape(nb, tt, n_f)


def _pre(x, ada, g, w16, w16t, wf16, bfp, sb_w, fx_w, n_f, time_minor):
    b, t, d = x.shape
    tt = min(t, ROW_BLOCK)
    nb = max(1, ROW_BLOCK // tt)
    assert nb == 1 or not time_minor
    grid = (b // nb, t // tt)
    tok = lambda w: pl.BlockSpec((nb, tt, w), lambda i, j: (i, j, 0))
    const = lambda a: pl.BlockSpec(a.shape, lambda i, j: (0, 0))
    shape = lambda w, dt: jax.ShapeDtypeStruct((b, t, w), dt)
    if time_minor:
        kv = lambda w: pl.BlockSpec((1, w, tt), lambda i, j: (i, 0, j))
        kv_shape = lambda w, dt: jax.ShapeDtypeStruct((b, w, t), dt)
    else:
        kv, kv_shape = tok, shape
    return pl.pallas_call(
        functools.partial(_pre_kernel, sb_w=sb_w, fx_w=fx_w, n_f=n_f, time_minor=time_minor),
        grid=grid,
        in_specs=[tok(d), pl.BlockSpec((nb, 6, d), lambda i, j: (i, 0, 0)),
                  const(g), const(w16), const(w16t), const(wf16), const(bfp)],
        out_specs=[tok(sb_w)] + [kv(sb_w)] * 4 + [tok(fx_w)] + [kv(fx_w)] * 4 + [tok(n_f)],
        out_shape=[shape(sb_w, BF16), kv_shape(sb_w, F32), kv_shape(sb_w, F32), kv_shape(sb_w, BF16),
                   kv_shape(sb_w, BF16),
                   shape(fx_w, BF16), kv_shape(fx_w, F32), kv_shape(fx_w, F32), kv_shape(fx_w, BF16),
                   kv_shape(fx_w, BF16),
                   shape(n_f, F32)],
        compiler_params=_params("arbitrary", "arbitrary"),
        name="pre_mixer",
    )(x, ada, g, w16, w16t, wf16, bfp)


def _cumsum_kernel(x_ref, o_ref, carry_ref):
    @pl.when(pl.program_id(0) == 0)
    def _():
        carry_ref[...] = jnp.zeros_like(carry_ref)

    x = x_ref[...]
    tc = x.shape[1]
    r = lax.broadcasted_iota(jnp.int32, (tc, tc), 0)
    c = lax.broadcasted_iota(jnp.int32, (tc, tc), 1)
    upper = jnp.where(r <= c, 1.0, 0.0).astype(BF16)
    carry = carry_ref[...]
    o_ref[...] = _dot_split(x, upper, 3) + _lanes_to(carry, tc)
    carry_ref[...] = carry + _dot_split(x, jnp.ones((tc, LANES), BF16), 3)


def _cumsum_rows(x, tc):
    rows, t = x.shape
    return pl.pallas_call(
        _cumsum_kernel,
        grid=(t // tc,),
        in_specs=[pl.BlockSpec((rows, tc), lambda j: (0, j))],
        out_specs=pl.BlockSpec((rows, tc), lambda j: (0, j)),
        out_shape=jax.ShapeDtypeStruct((rows, t), F32),
        scratch_shapes=[pltpu.VMEM((rows, LANES), F32)],
        compiler_params=_params("arbitrary"),
        name="cumsum_time",
    )(x)


def _tri_strict(n):
    r = lax.broadcasted_iota(jnp.int32, (n, n), 0)
    c = lax.broadcasted_iota(jnp.int32, (n, n), 1)
    return jnp.where(r > c, 1.0, 0.0).astype(BF16)


def _causal(tq, tk, strict):
    r = lax.broadcasted_iota(jnp.int32, (tq, tk), 0)
    c = lax.broadcasted_iota(jnp.int32, (tq, tk), 1)
    return (c < r) if strict else (c <= r)


def _sb_tiles(chains, mask):
    tk = chains[0][1].shape[1]
    tri = _tri_strict(tk)
    wide = tk % LANES == 0
    if wide:
        tri2 = jnp.concatenate([tri, tri], axis=0)
    zs = [_dot(q, k) for q, k, _, _ in chains]
    staged = []
    for z in zs:
        log_beta = jnp.minimum(z, 0.0) - jnp.log(1.0 + jnp.exp(-jnp.abs(z)))
        log_keep = log_beta - z
        if mask is not None:
            log_keep = jnp.where(mask, log_keep, 0.0)
        hi, lo = _split_bf16(log_keep, 2)
        staged.append((log_beta, log_keep[:, 0:1], hi, lo))
    if wide:
        betweens = [_dot(jnp.concatenate([hi, lo], axis=1), tri2) for _, _, hi, lo in staged]
    else:
        betweens = [_dot(hi, tri) + _dot(lo, tri) for _, _, hi, lo in staged]
    weights = []
    for (log_beta, _, _, _), between, (_, _, _, carry) in zip(staged, betweens, chains):
        a = jnp.exp(log_beta + between + carry)
        if mask is not None:
            a = jnp.where(mask, a, 0.0)
        weights.append(a.astype(BF16))
    pvs = [_dot_nt(a, v) for a, (_, _, v, _) in zip(weights, chains)]
    return [(pv, carry + (between[:, 0:1] + first))
            for pv, between, (_, first, _, _), (_, _, _, carry) in zip(pvs, betweens, staged, chains)]


def _sb_start(rows, groups):
    col = jnp.zeros((rows, 1), F32)
    return (col, col, jnp.zeros((rows, LANES), F32)) * groups


def _sb_visit(qs, kvs, state, mask):
    chains = []
    for g, ((q0, q1), (k, v)) in enumerate(zip(qs, kvs)):
        chains.append((q0, k, _keep_head(v, True, 0), state[3 * g]))
        chains.append((q1, k, _keep_head(v, False, 0), state[3 * g + 1]))
    res = _sb_tiles(chains, mask)
    out = ()
    for g in range(len(qs)):
        (pv0, c0), (pv1, c1) = res[2 * g], res[2 * g + 1]
        out += (c0, c1, state[3 * g + 2] + pv0 + pv1)
    return out


def _sb_sweep(visit, lefts, state):
    groups = len(lefts)

    def live(s):
        alive = jnp.bool_(False)
        for g in range(groups):
            more = jnp.max(jnp.maximum(s[1 + 3 * g], s[2 + 3 * g])) > EXP_ZERO
            alive = jnp.logical_or(alive, jnp.logical_and(s[0] < lefts[g], more))
        return alive

    def step(s):
        i, old = s[0], s[1:]
        new = visit([jnp.maximum(lefts[g] - 1 - i, 0) for g in range(groups)], old, None)
        out = (i + 1,)
        for g in range(groups):
            inside = i < lefts[g]
            out += tuple(jnp.where(inside, n, o) for n, o in zip(new[3 * g:3 * g + 3], old[3 * g:3 * g + 3]))
        return out

    return lax.while_loop(live, step, (jnp.int32(0),) + tuple(state))[1:]


def _head_column(f, h):
    lane = lax.broadcasted_iota(jnp.int32, f.shape, 1)
    return jnp.sum(jnp.where(lane == h, f, 0.0), axis=1, keepdims=True)


def _sb_prompt_kernel(q_ref, k_ref, v_ref, o_ref, *, tile, blocks):
    first = pl.program_id(2) * blocks
    qs = []
    for j in range(blocks):
        q = q_ref[0, j * tile:(j + 1) * tile, :]
        qs.append((_keep_head(q, True), _keep_head(q, False)))

    def visit(kbs, state, mask):
        kvs = []
        for j in range(blocks):
            start = pl.multiple_of(kbs[j] * tile, tile)
            kvs.append((k_ref[0, :, pl.ds(start, tile)], v_ref[0, :, pl.ds(start, tile)]))
        return _sb_visit(qs, kvs, state, mask)

    lefts = [first + j for j in range(blocks)]
    state = visit(lefts, _sb_start(tile, blocks), _causal(tile, tile, True))
    state = _sb_sweep(visit, lefts, state)
    for j in range(blocks):
        o_ref[0, j * tile:(j + 1) * tile, :] = state[2 + 3 * j]


def _head_norms_sq(x16):
    x = x16.astype(F32)
    r = lax.broadcasted_iota(jnp.int32, (LANES, LANES), 0)
    c = lax.broadcasted_iota(jnp.int32, (LANES, LANES), 1)
    same_head = jnp.where((r < HEAD_DIM) == (c < HEAD_DIM), 1.0, 0.0).astype(BF16)
    return _dot_split(x * x, same_head, 2)


def _fox_prompt_kernel(q_ref, k_ref, v_ref, fq_ref, fk_ref, o_ref, kmax_ref, *, tile):
    hp = pl.program_id(1)
    qi = pl.program_id(2)
    q = q_ref[0]
    q0, q1 = _keep_head(q, True), _keep_head(q, False)
    fq = fq_ref[0]
    fq0, fq1 = _head_column(fq, 2 * hp), _head_column(fq, 2 * hp + 1)
    shape = (tile, LANES)

    @pl.when(qi == 0)
    def _():
        width = min(NORM_CHUNK, k_ref.shape[2])

        def chunk(i, best):
            kc = k_ref[0, :, pl.ds(pl.multiple_of(i * width, width), width)].astype(F32)
            sq = kc * kc
            return (jnp.maximum(best[0], jnp.sum(sq[:HEAD_DIM], axis=0, keepdims=True)),
                    jnp.maximum(best[1], jnp.sum(sq[HEAD_DIM:], axis=0, keepdims=True)))

        zero = jnp.zeros((1, width), F32)
        best = lax.fori_loop(0, k_ref.shape[2] // width, chunk, (zero, zero))
        kmax_ref[...] = _per_head(jnp.max(best[0], axis=1, keepdims=True),
                                  jnp.max(best[1], axis=1, keepdims=True), (1, LANES))

    reach = (jnp.sqrt(_head_norms_sq(q)) * jnp.sqrt(kmax_ref[...]) * NORM_SLACK
             + _per_head(fq0, fq1, shape))

    def last_fk(kb):
        fk = fk_ref[0, 0, jnp.maximum(kb, 0)]
        return _per_head(fk[0:1, tile - 1:tile], fk[1:2, tile - 1:tile], (1, LANES))

    def scores(kb, mask):
        k = k_ref[0, :, pl.ds(pl.multiple_of(kb * tile, tile), tile)]
        fk = fk_ref[0, 0, kb]
        u0 = _dot(q0, k) - fk[0:1, :]
        u1 = _dot(q1, k) - fk[1:2, :]
        if mask is not None:
            u0, u1 = jnp.where(mask, u0, NEG_BIG), jnp.where(mask, u1, NEG_BIG)
        return u0, u1

    causal = _causal(tile, tile, False)

    def row_max(kb, m0, m1, mask):
        u0, u1 = scores(kb, mask)
        return (jnp.maximum(m0, jnp.max(u0, axis=1, keepdims=True) + fq0),
                jnp.maximum(m1, jnp.max(u1, axis=1, keepdims=True) + fq1))

    def best_gap(kb, m0, m1):
        return jnp.max(reach - _per_head(m0, m1, shape) - last_fk(kb))

    def max_step(s):
        i = s[0]
        m0, m1 = row_max(qi - 1 - i, s[2], s[3], None)
        return i + 1, best_gap(qi - 2 - i, m0, m1), m0, m1

    neg = jnp.full((tile, 1), NEG_BIG, F32)
    m0, m1 = row_max(qi, neg, neg, causal)
    _, _, m0, m1 = lax.while_loop(lambda s: jnp.logical_and(s[0] < qi, s[1] > 0.0), max_step,
                                  (jnp.int32(0), best_gap(qi - 1, m0, m1), m0, m1))

    c0, c1 = m0 - fq0, m1 - fq1
    gap_rows = jnp.max(reach - _per_head(m0, m1, shape), axis=0, keepdims=True)

    def weigh(kb, acc0, acc1, mask):
        u0, u1 = scores(kb, mask)
        v = v_ref[0, :, pl.ds(pl.multiple_of(kb * tile, tile), tile)]
        acc0 = acc0 + _dot_nt(jnp.exp(u0 - c0).astype(BF16), _keep_head(v, True, 0, 1.0))
        acc1 = acc1 + _dot_nt(jnp.exp(u1 - c1).astype(BF16), _keep_head(v, False, 0, 1.0))
        return acc0, acc1

    def head_gaps(kb):
        gap = gap_rows - last_fk(kb)
        first = lax.broadcasted_iota(jnp.int32, gap.shape, 1) < HEAD_DIM
        return jnp.max(jnp.where(first, gap, NEG_BIG)), jnp.max(jnp.where(first, NEG_BIG, gap))

    def sum_step(s):
        i = s[0]
        acc0, acc1 = weigh(qi - 1 - i, s[3], s[4], None)
        return (i + 1,) + head_gaps(qi - 2 - i) + (acc0, acc1)

    zero = jnp.zeros(shape, F32)
    acc0, acc1 = weigh(qi, zero, zero, causal)
    done, gap0, gap1, acc0, acc1 = lax.while_loop(
        lambda s: jnp.logical_and(s[0] < qi, jnp.minimum(s[1], s[2]) > EXP_ZERO), sum_step,
        (jnp.int32(0),) + head_gaps(qi - 1) + (acc0, acc1))

    half_tile = tile // 2

    def lone(second, start, gap):
        q_s, c_s = jnp.where(second, q1, q0), jnp.where(second, c1, c0)

        def lone_step(s):
            i = s[0]
            cols = pl.ds(pl.multiple_of((qi - 1 - i) * tile, tile), tile)
            k, v = k_ref[0, :, cols], v_ref[0, :, cols]
            fk = fk_ref[0, 0, qi - 1 - i]
            fk_s = jnp.where(second, fk[1:2, :], fk[0:1, :])
            ones_rows = (lax.broadcasted_iota(jnp.int32, v.shape, 0) < HEAD_DIM) == second
            v_s = jnp.where(ones_rows, jnp.ones_like(v), v)
            us = [_dot(q_s, k[:, lo:lo + half_tile]) - fk_s[:, lo:lo + half_tile] for lo in (0, half_tile)]
            ps = [jnp.exp(u - c_s).astype(BF16) for u in us]
            acc = s[2] + _dot_nt(ps[0], v_s[:, :half_tile]) + _dot_nt(ps[1], v_s[:, half_tile:])
            gaps = head_gaps(qi - 2 - i)
            return i + 1, jnp.where(second, gaps[1], gaps[0]), acc

        return lax.while_loop(lambda s: jnp.logical_and(s[0] < qi, s[1] > EXP_ZERO), lone_step,
                              (start, gap, zero))[2]

    second = gap1 > gap0
    acc_s = lone(second, done, jnp.maximum(gap0, gap1))
    acc0 = acc0 + jnp.where(second, 0.0, acc_s)
    acc1 = acc1 + jnp.where(second, acc_s, 0.0)
    lane = lax.broadcasted_iota(jnp.int32, shape, 1)
    half = LANES // 2
    o_ref[0] = jnp.where(lane < HEAD_DIM, acc0 / pltpu.roll(acc0, half, 1), acc1 / pltpu.roll(acc1, half, 1))


def _prompt_specs(t, tile):
    qspec = pl.BlockSpec((1, tile, LANES), lambda b, h, i: (b, i, h))
    kvspec = pl.BlockSpec((1, LANES, t), lambda b, h, i: (b, h, 0))
    return qspec, kvspec


def _sb_prompt(q, k, v):
    b, t, w = q.shape
    tile = ATT_BLOCK
    blocks = SB_BLOCKS_PER_STEP if t % (tile * SB_BLOCKS_PER_STEP) == 0 else 1
    qspec, kvspec = _prompt_specs(t, tile * blocks)
    return pl.pallas_call(
        functools.partial(_sb_prompt_kernel, tile=tile, blocks=blocks),
        grid=(b, w // LANES, t // (tile * blocks)),
        in_specs=[qspec, kvspec, kvspec],
        out_specs=qspec,
        out_shape=jax.ShapeDtypeStruct((b, t, w), F32),
        compiler_params=_params("arbitrary", "arbitrary", "arbitrary"),
        name="sb_prompt",
    )(q, k, v)


def _fox_prompt(q, k, v, fq, fk):
    b, t, w = q.shape
    tile = fk.shape[-1]
    qspec, kvspec = _prompt_specs(t, tile)
    n_f = fq.shape[-1]
    return pl.pallas_call(
        functools.partial(_fox_prompt_kernel, tile=tile),
        grid=(b, w // LANES, t // tile),
        in_specs=[qspec, kvspec, kvspec,
                  pl.BlockSpec((1, tile, n_f), lambda b, h, i: (b, i, 0)),
                  pl.BlockSpec((1, 1, t // tile, 2, tile), lambda b, h, i: (b, h, 0, 0, 0))],
        out_specs=qspec,
        out_shape=jax.ShapeDtypeStruct((b, t, w), F32),
        scratch_shapes=[pltpu.VMEM((1, LANES), F32)],
        compiler_params=_params("arbitrary", "arbitrary", "arbitrary"),
        name="fox_prompt",
    )(q, k, v, fq, fk)


def _group_rows(g):
    return slice(g * LANES, (g + 1) * LANES)


def _sb_sample_kernel(q_ref, kn_ref, vn_ref, kc_ref, vc_ref, o_ref, *, tile):
    n, width = q_ref.shape[1:]
    groups = width // LANES
    past = kc_ref.shape[2]
    qs = []
    for g in range(groups):
        q = q_ref[0, :, _group_rows(g)]
        qs.append((_keep_head(q, True), _keep_head(q, False)))

    def visit(kbs, state, mask):
        kvs = []
        for g in range(groups):
            if mask is not None:
                kvs.append((kn_ref[0, _group_rows(g), :], vn_ref[0, _group_rows(g), :]))
            else:
                cols = pl.ds(pl.multiple_of(kbs[g] * tile, tile), tile)
                kvs.append((kc_ref[0, _group_rows(g), cols].astype(BF16),
                            vc_ref[0, _group_rows(g), cols].astype(BF16)))
        return _sb_visit(qs, kvs, state, mask)

    state = visit(None, _sb_start(n, groups), _causal(n, n, True))
    state = _sb_sweep(visit, [past // tile] * groups, state)
    for g in range(groups):
        o_ref[0, :, _group_rows(g)] = state[2 + 3 * g]


def _fox_sample_kernel(q_ref, kn_ref, vn_ref, kc_ref, vc_ref, fq_ref, fk_ref, o_ref):
    n, width = q_ref.shape[1:]
    groups = width // LANES
    past = kc_ref.shape[2]
    fq = fq_ref[0]
    causal = _causal(n, n, False)
    scored = []
    for g in range(groups):
        q = q_ref[0, :, _group_rows(g)]
        kn = kn_ref[0, _group_rows(g), :]
        kc = kc_ref[0, _group_rows(g), :].astype(BF16)
        for h in range(2):
            qh = _keep_head(q, h == 0)
            fk = fk_ref[0, g, h:h + 1, :]
            u_new = jnp.where(causal, _dot(qh, kn) - fk[:, past:past + n], NEG_BIG)
            u_old = _dot(qh, kc) - fk[:, 0:past]
            scored.append((u_new, u_old, fq[:, 2 * g + h:2 * g + h + 1]))
    weights = []
    for u_new, u_old, fq_col in scored:
        m = jnp.maximum(jnp.max(u_new, axis=1, keepdims=True), jnp.max(u_old, axis=1, keepdims=True)) + fq_col
        c = m - fq_col
        weights.append((jnp.exp(u_new - c).astype(BF16), jnp.exp(u_old - c).astype(BF16)))
    lane = lax.broadcasted_iota(jnp.int32, (n, LANES), 1)
    half = LANES // 2
    for g in range(groups):
        vn = vn_ref[0, _group_rows(g), :]
        vc = vc_ref[0, _group_rows(g), :].astype(BF16)
        accs = []
        for h in range(2):
            p_new, p_old = weights[2 * g + h]
            accs.append(_dot_nt(p_new, _keep_head(vn, h == 0, 0, 1.0))
                        + _dot_nt(p_old, _keep_head(vc, h == 0, 0, 1.0)))
        o_ref[0, :, _group_rows(g)] = jnp.where(lane < HEAD_DIM, accs[0] / pltpu.roll(accs[0], half, 1),
                                                accs[1] / pltpu.roll(accs[1], half, 1))


def _sample_specs(n, width, past):
    qspec = pl.BlockSpec((1, n, width), lambda b: (b, 0, 0))
    new = pl.BlockSpec((1, width, n), lambda b: (b, 0, 0))
    cache = pl.BlockSpec((1, width, past), lambda b: (b, 0, 0))
    return qspec, new, cache


def _sb_sample(q, kn, vn, kc, vc):
    b, n, w = q.shape
    qspec, new, cache = _sample_specs(n, w, kc.shape[2])
    return pl.pallas_call(
        functools.partial(_sb_sample_kernel, tile=ATT_BLOCK),
        grid=(b,),
        in_specs=[qspec, new, new, cache, cache],
        out_specs=qspec,
        out_shape=jax.ShapeDtypeStruct((b, n, w), F32),
        compiler_params=_params("arbitrary"),
        name="sb_sample",
    )(q, kn, vn, kc, vc)


def _fox_sample(q, kn, vn, kc, vc, fq, fk):
    b, n, w = q.shape
    qspec, new, cache = _sample_specs(n, w, kc.shape[2])
    return pl.pallas_call(
        _fox_sample_kernel,
        grid=(b,),
        in_specs=[qspec, new, new, cache, cache,
                  pl.BlockSpec((1, n, fq.shape[-1]), lambda b: (b, 0, 0)),
                  pl.BlockSpec((1,) + fk.shape[1:], lambda b: (b, 0, 0, 0))],
        out_specs=qspec,
        out_shape=jax.ShapeDtypeStruct((b, n, w), F32),
        compiler_params=_params("arbitrary"),
        name="fox_sample",
    )(q, kn, vn, kc, vc, fq, fk)


def _post_kernel(x_ref, osb_ref, ofx_ref, ada_ref, gsb_ref, gfx_ref, wo_ref, gffn_ref,
                 wg_ref, wu_ref, wd_ref, gfin_ref, y_ref, x2_ref, h_ref, acc_ref, *, final_norm):
    j = pl.program_id(2)
    nb, tt, d = x_ref.shape
    rows = nb * tt

    @pl.when(j == 0)
    def _():
        ada = ada_ref[...]
        o_sb = _rms(osb_ref[...], gsb_ref[...])
        o_fx = _rms(ofx_ref[...], gfx_ref[...])
        sb_w, fx_w = o_sb.shape[-1], o_fx.shape[-1]
        proj = (_dot(o_sb.reshape(rows, sb_w).astype(BF16), wo_ref[0:sb_w, :])
                + _dot(o_fx.reshape(rows, fx_w).astype(BF16), wo_ref[sb_w:sb_w + fx_w, :]))
        x2 = x_ref[...] + (1.0 + ada[:, 2:3, :]) * proj.reshape(nb, tt, d)
        x2_ref[...] = x2
        h = _rms(x2, gffn_ref[...]) * (1.0 + ada[:, 4:5, :]) + ada[:, 3:4, :]
        h_ref[...] = h.reshape(rows, d).astype(BF16)

    h = h_ref[...]
    g = _dot(h, wg_ref[...])
    u = _dot(h, wu_ref[...])
    act = (g * (1.0 / (1.0 + jnp.exp(-g))) * u).astype(BF16)
    part = _dot(act, wd_ref[...])

    @pl.when(j == 0)
    def _():
        acc_ref[...] = part

    @pl.when(j > 0)
    def _():
        acc_ref[...] += part

    @pl.when(j == pl.num_programs(2) - 1)
    def _():
        ada = ada_ref[...]
        x3 = x2_ref[...] + (1.0 + ada[:, 5:6, :]) * acc_ref[...].reshape(nb, tt, d)
        y_ref[...] = _rms(x3, gfin_ref[...]) if final_norm else x3


def _post(x, o_sb, o_fx, ada, g_sb, g_fx, wo16, g_ffn, wg16, wu16, wd16, g_fin, final_norm):
    b, t, d = x.shape
    tt = min(t, ROW_BLOCK)
    nb = max(1, ROW_BLOCK // tt)
    d_ff = wg16.shape[1]
    n_ff = 2 if (d_ff // 2) % LANES == 0 else 1
    tf = d_ff // n_ff
    tok = lambda w: pl.BlockSpec((nb, tt, w), lambda i, j, f: (i, j, 0))
    const = lambda a: pl.BlockSpec(a.shape, lambda i, j, f: (0, 0))
    return pl.pallas_call(
        functools.partial(_post_kernel, final_norm=final_norm),
        grid=(b // nb, t // tt, n_ff),
        in_specs=[tok(d), tok(o_sb.shape[-1]), tok(o_fx.shape[-1]),
                  pl.BlockSpec((nb, 6, d), lambda i, j, f: (i, 0, 0)),
                  const(g_sb), const(g_fx), const(wo16), const(g_ffn),
                  pl.BlockSpec((d, tf), lambda i, j, f: (0, f)),
                  pl.BlockSpec((d, tf), lambda i, j, f: (0, f)),
                  pl.BlockSpec((tf, d), lambda i, j, f: (f, 0)),
                  const(g_fin)],
        out_specs=tok(d),
        out_shape=jax.ShapeDtypeStruct((b, t, d), F32),
        scratch_shapes=[pltpu.VMEM((nb, tt, d), F32), pltpu.VMEM((nb * tt, d), BF16),
                        pltpu.VMEM((nb * tt, d), F32)],
        compiler_params=_params("arbitrary", "arbitrary", "arbitrary"),
        name="post_mixer_ffn",
    )(x, o_sb, o_fx, ada, g_sb, g_fx, wo16, g_ffn, wg16, wu16, wd16, g_fin)


def _pad_lanes(a, n):
    return jnp.pad(a, ((0, 0), (0, n - a.shape[1])))


def kernel(x_prompt, x_sample, c_prompt, c_sample, cache_sb_k, cache_sb_v, cache_fox_k, cache_fox_v, cache_fox_logf, w_ada, b_ada, g_mix, w_in, b_f, g_sb_out, g_fox_out, w_o, g_ffn, w_gate, w_up, w_down, g_final):
    depth = w_ada.shape[0]
    bp, tp, d = x_prompt.shape
    bs, ts, _ = x_sample.shape
    past, n_sb, hd = cache_sb_k.shape[2:]
    n_fx = cache_fox_k.shape[3]
    assert hd == HEAD_DIM and n_sb % 2 == 0 and n_fx % 2 == 0
    assert tp % ROW_BLOCK == 0 and tp % ATT_BLOCK == 0 and ROW_BLOCK % ts == 0 and bs % (ROW_BLOCK // ts) == 0
    assert past % ATT_BLOCK == 0
    sb_w, fx_w = n_sb * hd, n_fx * hd
    qkv_cols = 3 * sb_w + 3 * fx_w
    att = min(FOX_BLOCK, tp)
    past_pad = -(-(past + ts) // (3 * LANES)) * (3 * LANES)
    row = lambda a: a.reshape(1, -1)

    xp, xs = x_prompt, x_sample
    outs = [[] for _ in range(10)]
    for l in range(depth):
        w16 = w_in[l][:, :qkv_cols].astype(BF16)
        w16t = jnp.swapaxes(w_in[l], 0, 1)[:qkv_cols].astype(BF16)
        wf16 = _pad_lanes(w_in[l][:, qkv_cols:], LANES).astype(BF16)
        bfp = _pad_lanes(row(b_f[l]), LANES)
        wo16, wg16 = w_o[l].astype(BF16), w_gate[l].astype(BF16)
        wu16, wd16 = w_up[l].astype(BF16), w_down[l].astype(BF16)

        c_all = jnp.concatenate([c_prompt, c_sample], axis=0)
        ada = _ada(c_all, w_ada[l], row(b_ada[l])).reshape(bp + bs, 6, d)
        ada_p, ada_s = ada[:bp], ada[bp:]

        (q_sb, k_sb, v_sb, k_sb16, v_sb16, q_fx, k_fx, v_fx, k_fx16, v_fx16, lf) = _pre(
            xp, ada_p, row(g_mix[l]), w16, w16t, wf16, bfp, sb_w, fx_w, n_fx, True)
        o_sb = _sb_prompt(q_sb, k_sb16, v_sb16)
        f_cum = _cumsum_rows(jnp.swapaxes(lf, 1, 2).reshape(bp * n_fx, tp), ROW_BLOCK)
        fq = jnp.swapaxes(f_cum.reshape(bp, n_fx, tp), 1, 2)
        fk = jnp.swapaxes(f_cum.reshape(bp, n_fx // 2, 2, tp // att, att), 2, 3)
        o_fx = _fox_prompt(q_fx, k_fx16, v_fx16, fq, fk)
        xp = _post(xp, o_sb, o_fx, ada_p, row(g_sb_out[l]), row(g_fox_out[l]), wo16, row(g_ffn[l]),
                   wg16, wu16, wd16, row(g_final), l + 1 == depth)
        for dst, a in zip(outs[0:4], (k_sb, v_sb, k_fx, v_fx)):
            dst.append(jnp.swapaxes(a, 1, 2).reshape(bp, tp, -1, hd))
        outs[4].append(lf)

        (q_sb, k_sb, v_sb, k_sb16, v_sb16, q_fx, k_fx, v_fx, k_fx16, v_fx16, lf) = _pre(
            xs, ada_s, row(g_mix[l]), w16, w16t, wf16, bfp, sb_w, fx_w, n_fx, False)
        time_minor = lambda a: jnp.swapaxes(a.reshape(a.shape[0], a.shape[1], -1), 1, 2)
        o_sb = _sb_sample(q_sb, time_minor(k_sb16), time_minor(v_sb16),
                          time_minor(cache_sb_k[l]), time_minor(cache_sb_v[l]))
        lf_all = jnp.concatenate([cache_fox_logf[l].astype(F32), lf], axis=1)
        lf_rows = _pad_lanes(jnp.swapaxes(lf_all, 1, 2).reshape(bs * n_fx, past + ts), past_pad)
        f_all = _cumsum_rows(lf_rows, past_pad // 3)
        fq = jnp.swapaxes(f_all.reshape(bs, n_fx, past_pad)[:, :, past:past + ts], 1, 2)
        fk = f_all.reshape(bs, n_fx // 2, 2, past_pad)
        o_fx = _fox_sample(q_fx, time_minor(k_fx16), time_minor(v_fx16),
                           time_minor(cache_fox_k[l]), time_minor(cache_fox_v[l]), fq, fk)
        xs = _post(xs, o_sb, o_fx, ada_s, row(g_sb_out[l]), row(g_fox_out[l]), wo16, row(g_ffn[l]),
                   wg16, wu16, wd16, row(g_final), l + 1 == depth)
        for dst, a in zip(outs[5:9], (k_sb, v_sb, k_fx, v_fx)):
            dst.append(a.reshape(bs, ts, -1, hd))
        outs[9].append(lf)

    return (xp, xs) + tuple(jnp.stack(o) for o in outs)
```

```python
import functools

import jax
import jax.numpy as jnp
from jax import lax
from jax.experimental import pallas as pl
from jax.experimental.pallas import tpu as pltpu

F32 = jnp.float32
BF16 = jnp.bfloat16
EPS = 1e-6
HEAD_DIM = 64
LANES = 128
ROW_BLOCK = 512
FF_CHUNK = 512
ATT_BLOCK = 256
SB_BLOCKS_PER_STEP = 2
FOX_BLOCK = 512
NEG_BIG = -1e30
EXP_ZERO = -105.0
NORM_CHUNK = 2048
NORM_SLACK = 1.0 + 2.0 ** -8
VMEM_LIMIT = 56 * 1024 * 1024


def _params(*semantics):
    return pltpu.CompilerParams(dimension_semantics=semantics, vmem_limit_bytes=VMEM_LIMIT)


def _dot(a, b):
    return jnp.dot(a, b, preferred_element_type=F32)


def _dot_nt(a, b):
    return lax.dot_general(a, b, (((1,), (1,)), ((), ())), preferred_element_type=F32)


def _rms(x, g):
    return x * lax.rsqrt(jnp.mean(x * x, axis=-1, keepdims=True) + EPS) * g


def _split_bf16(x, parts):
    out = []
    for _ in range(parts - 1):
        p = x.astype(BF16)
        out.append(p)
        x = x - p.astype(F32)
    out.append(x.astype(BF16))
    return out


def _dot_split(x, m, parts):
    acc = None
    for p in _split_bf16(x, parts):
        d = _dot(p, m)
        acc = d if acc is None else acc + d
    return acc


def _keep_head(x, first, axis=1, fill=0.0):
    pos = lax.broadcasted_iota(jnp.int32, x.shape, axis)
    keep = (pos < HEAD_DIM) if first else (pos >= HEAD_DIM)
    return jnp.where(keep, x, jnp.full_like(x, fill))


def _per_head(col0, col1, shape):
    lane = lax.broadcasted_iota(jnp.int32, shape, 1)
    return jnp.where(lane < HEAD_DIM, col0, col1)


def _lanes_to(x, n):
    if n <= LANES:
        return x[:, :n]
    return jnp.concatenate([x] * (n // LANES), axis=1)


def _ada_kernel(c_ref, w_ref, b_ref, o_ref):
    c = c_ref[...]
    s = (c * (1.0 / (1.0 + jnp.exp(-c)))).astype(BF16)
    o_ref[...] = _dot(s, w_ref[...].astype(BF16)) + b_ref[...]


def _ada(c, w, b):
    rows, d = c.shape
    n = w.shape[1]
    tn = 1024
    return pl.pallas_call(
        _ada_kernel,
        grid=(n // tn,),
        in_specs=[pl.BlockSpec((rows, d), lambda j: (0, 0)),
                  pl.BlockSpec((d, tn), lambda j: (0, j)),
                  pl.BlockSpec((1, tn), lambda j: (0, j))],
        out_specs=pl.BlockSpec((rows, tn), lambda j: (0, j)),
        out_shape=jax.ShapeDtypeStruct((rows, n), F32),
        compiler_params=_params("arbitrary"),
        name="ada",
    )(c, w, b)


def _pre_kernel(x_ref, ada_ref, g_ref, w_ref, wt_ref, wf_ref, bf_ref,
                qsb_ref, ksb_ref, vsb_ref, ksb16_ref, vsb16_ref,
                qfx_ref, kfx_ref, vfx_ref, kfx16_ref, vfx16_ref, lf_ref, *, sb_w, fx_w, n_f, time_minor):
    x = x_ref[...]
    nb, tt, d = x.shape
    ada = ada_ref[...]
    h = _rms(x, g_ref[...]) * (1.0 + ada[:, 1:2, :]) + ada[:, 0:1, :]
    h = h.reshape(nb * tt, d).astype(BF16)
    q_scale = HEAD_DIM ** -0.5

    def proj(lo, width):
        return _dot(h, w_ref[:, lo:lo + width]).reshape(nb, tt, width)

    def proj_kv(lo, width):
        if time_minor:
            return _dot_nt(wt_ref[lo:lo + width, :], h)[None]
        return proj(lo, width)

    base = 3 * sb_w
    qsb_ref[...] = (proj(0, sb_w) * q_scale).astype(BF16)
    qfx_ref[...] = (proj(base, fx_w) * q_scale).astype(BF16)
    for lo, width, out, out16 in ((sb_w, sb_w, ksb_ref, ksb16_ref), (2 * sb_w, sb_w, vsb_ref, vsb16_ref),
                                  (base + fx_w, fx_w, kfx_ref, kfx16_ref),
                                  (base + 2 * fx_w, fx_w, vfx_ref, vfx16_ref)):
        r = proj_kv(lo, width)
        out[...] = r
        out16[...] = r.astype(BF16)
    u = _dot(h, wf_ref[...]) + bf_ref[...]
    lf = jnp.minimum(u, 0.0) - jnp.log(1.0 + jnp.exp(-jnp.abs(u)))
    lf_ref[...] = lf[:, :n_f].reshape(nb, tt, n_f)


def _pre(x, ada, g, w16, w16t, wf16, bfp, sb_w, fx_w, n_f, time_minor):
    b, t, d = x.shape
    tt = min(t, ROW_BLOCK)
    nb = max(1, ROW_BLOCK // tt)
    assert nb == 1 or not time_minor
    grid = (b // nb, t // tt)
    tok = lambda w: pl.BlockSpec((nb, tt, w), lambda i, j: (i, j, 0))
    const = lambda a: pl.BlockSpec(a.shape, lambda i, j: (0, 0))
    shape = lambda w, dt: jax.ShapeDtypeStruct((b, t, w), dt)
    if time_minor:
        kv = lambda w: pl.BlockSpec((1, w, tt), lambda i, j: (i, 0, j))
        kv_shape = lambda w, dt: jax.ShapeDtypeStruct((b, w, t), dt)
    else:
        kv, kv_shape = tok, shape
    return pl.pallas_call(
        functools.partial(_pre_kernel, sb_w=sb_w, fx_w=fx_w, n_f=n_f, time_minor=time_minor),
        grid=grid,
        in_specs=[tok(d), pl.BlockSpec((nb, 6, d), lambda i, j: (i, 0, 0)),
                  const(g), const(w16), const(w16t), const(wf16), const(bfp)],
        out_specs=[tok(sb_w)] + [kv(sb_w)] * 4 + [tok(fx_w)] + [kv(fx_w)] * 4 + [tok(n_f)],
        out_shape=[shape(sb_w, BF16), kv_shape(sb_w, F32), kv_shape(sb_w, F32), kv_shape(sb_w, BF16),
                   kv_shape(sb_w, BF16),
                   shape(fx_w, BF16), kv_shape(fx_w, F32), kv_shape(fx_w, F32), kv_shape(fx_w, BF16),
                   kv_shape(fx_w, BF16),
                   shape(n_f, F32)],
        compiler_params=_params("arbitrary", "arbitrary"),
        name="pre_mixer",
    )(x, ada, g, w16, w16t, wf16, bfp)


def _cumsum_kernel(x_ref, o_ref, carry_ref):
    @pl.when(pl.program_id(0) == 0)
    def _():
        carry_ref[...] = jnp.zeros_like(carry_ref)

    x = x_ref[...]
    tc = x.shape[1]
    r = lax.broadcasted_iota(jnp.int32, (tc, tc), 0)
    c = lax.broadcasted_iota(jnp.int32, (tc, tc), 1)
    upper = jnp.where(r <= c, 1.0, 0.0).astype(BF16)
    carry = carry_ref[...]
    o_ref[...] = _dot_split(x, upper, 3) + _lanes_to(carry, tc)
    carry_ref[...] = carry + _dot_split(x, jnp.ones((tc, LANES), BF16), 3)


def _cumsum_rows(x, tc):
    rows, t = x.shape
    return pl.pallas_call(
        _cumsum_kernel,
        grid=(t // tc,),
        in_specs=[pl.BlockSpec((rows, tc), lambda j: (0, j))],
        out_specs=pl.BlockSpec((rows, tc), lambda j: (0, j)),
        out_shape=jax.ShapeDtypeStruct((rows, t), F32),
        scratch_shapes=[pltpu.VMEM((rows, LANES), F32)],
        compiler_params=_params("arbitrary"),
        name="cumsum_time",
    )(x)


def _tri_strict(n):
    r = lax.broadcasted_iota(jnp.int32, (n, n), 0)
    c = lax.broadcasted_iota(jnp.int32, (n, n), 1)
    return jnp.where(r > c, 1.0, 0.0).astype(BF16)


def _causal(tq, tk, strict):
    r = lax.broadcasted_iota(jnp.int32, (tq, tk), 0)
    c = lax.broadcasted_iota(jnp.int32, (tq, tk), 1)
    return (c < r) if strict else (c <= r)


def _sb_tiles(chains, mask):
    tk = chains[0][1].shape[1]
    tri = _tri_strict(tk)
    wide = tk % LANES == 0
    if wide:
        tri2 = jnp.concatenate([tri, tri], axis=0)
    zs = [_dot(q, k) for q, k, _, _ in chains]
    staged = []
    for z in zs:
        log_beta = jnp.minimum(z, 0.0) - jnp.log(1.0 + jnp.exp(-jnp.abs(z)))
        log_keep = log_beta - z
        if mask is not None:
            log_keep = jnp.where(mask, log_keep, 0.0)
        hi, lo = _split_bf16(log_keep, 2)
        staged.append((log_beta, log_keep[:, 0:1], hi, lo))
    if wide:
        betweens = [_dot(jnp.concatenate([hi, lo], axis=1), tri2) for _, _, hi, lo in staged]
    else:
        betweens = [_dot(hi, tri) + _dot(lo, tri) for _, _, hi, lo in staged]
    weights = []
    for (log_beta, _, _, _), between, (_, _, _, carry) in zip(staged, betweens, chains):
        a = jnp.exp(log_beta + between + carry)
        if mask is not None:
            a = jnp.where(mask, a, 0.0)
        weights.append(a.astype(BF16))
    pvs = [_dot_nt(a, v) for a, (_, _, v, _) in zip(weights, chains)]
    return [(pv, carry + (between[:, 0:1] + first))
            for pv, between, (_, first, _, _), (_, _, _, carry) in zip(pvs, betweens, staged, chains)]


def _sb_start(rows, groups):
    col = jnp.zeros((rows, 1), F32)
    return (col, col, jnp.zeros((rows, LANES), F32)) * groups


def _sb_visit(qs, kvs, state, mask):
    chains = []
    for g, ((q0, q1), (k, v)) in enumerate(zip(qs, kvs)):
        chains.append((q0, k, _keep_head(v, True, 0), state[3 * g]))
        chains.append((q1, k, _keep_head(v, False, 0), state[3 * g + 1]))
    res = _sb_tiles(chains, mask)
    out = ()
    for g in range(len(qs)):
        (pv0, c0), (pv1, c1) = res[2 * g], res[2 * g + 1]
        out += (c0, c1, state[3 * g + 2] + pv0 + pv1)
    return out


def _sb_sweep(visit, lefts, state):
    groups = len(lefts)

    def live(s):
        alive = jnp.bool_(False)
        for g in range(groups):
            more = jnp.max(jnp.maximum(s[1 + 3 * g], s[2 + 3 * g])) > EXP_ZERO
            alive = jnp.logical_or(alive, jnp.logical_and(s[0] < lefts[g], more))
        return alive

    def step(s):
        i, old = s[0], s[1:]
        new = visit([jnp.maximum(lefts[g] - 1 - i, 0) for g in range(groups)], old, None)
        out = (i + 1,)
        for g in range(groups):
            inside = i < lefts[g]
            out += tuple(jnp.where(inside, n, o) for n, o in zip(new[3 * g:3 * g + 3], old[3 * g:3 * g + 3]))
        return out

    return lax.while_loop(live, step, (jnp.int32(0),) + tuple(state))[1:]


def _head_column(f, h):
    lane = lax.broadcasted_iota(jnp.int32, f.shape, 1)
    return jnp.sum(jnp.where(lane == h, f, 0.0), axis=1, keepdims=True)


def _sb_prompt_kernel(q_ref, k_ref, v_ref, o_ref, *, tile, blocks):
    first = pl.program_id(2) * blocks
    qs = []
    for j in range(blocks):
        q = q_ref[0, j * tile:(j + 1) * tile, :]
        qs.append((_keep_head(q, True), _keep_head(q, False)))

    def visit(kbs, state, mask):
        kvs = []
        for j in range(blocks):
            start = pl.multiple_of(kbs[j] * tile, tile)
            kvs.append((k_ref[0, :, pl.ds(start, tile)], v_ref[0, :, pl.ds(start, tile)]))
        return _sb_visit(qs, kvs, state, mask)

    lefts = [first + j for j in range(blocks)]
    state = visit(lefts, _sb_start(tile, blocks), _causal(tile, tile, True))
    state = _sb_sweep(visit, lefts, state)
    for j in range(blocks):
        o_ref[0, j * tile:(j + 1) * tile, :] = state[2 + 3 * j]


def _head_norms_sq(x16):
    x = x16.astype(F32)
    r = lax.broadcasted_iota(jnp.int32, (LANES, LANES), 0)
    c = lax.broadcasted_iota(jnp.int32, (LANES, LANES), 1)
    same_head = jnp.where((r < HEAD_DIM) == (c < HEAD_DIM), 1.0, 0.0).astype(BF16)
    return _dot_split(x * x, same_head, 2)


def _fox_prompt_kernel(q_ref, k_ref, v_ref, fq_ref, fk_ref, o_ref, kmax_ref, *, tile):
    hp = pl.program_id(1)
    qi = pl.program_id(2)
    q = q_ref[0]
    q0, q1 = _keep_head(q, True), _keep_head(q, False)
    fq = fq_ref[0]
    fq0, fq1 = _head_column(fq, 2 * hp), _head_column(fq, 2 * hp + 1)
    shape = (tile, LANES)

    @pl.when(qi == 0)
    def _():
        width = min(NORM_CHUNK, k_ref.shape[2])

        def chunk(i, best):
            kc = k_ref[0, :, pl.ds(pl.multiple_of(i * width, width), width)].astype(F32)
            sq = kc * kc
            return (jnp.maximum(best[0], jnp.sum(sq[:HEAD_DIM], axis=0, keepdims=True)),
                    jnp.maximum(best[1], jnp.sum(sq[HEAD_DIM:], axis=0, keepdims=True)))

        zero = jnp.zeros((1, width), F32)
        best = lax.fori_loop(0, k_ref.shape[2] // width, chunk, (zero, zero))
        kmax_ref[...] = _per_head(jnp.max(best[0], axis=1, keepdims=True),
                                  jnp.max(best[1], axis=1, keepdims=True), (1, LANES))

    reach = (jnp.sqrt(_head_norms_sq(q)) * jnp.sqrt(kmax_ref[...]) * NORM_SLACK
             + _per_head(fq0, fq1, shape))

    def last_fk(kb):
        fk = fk_ref[0, 0, jnp.maximum(kb, 0)]
        return _per_head(fk[0:1, tile - 1:tile], fk[1:2, tile - 1:tile], (1, LANES))

    def scores(kb, mask):
        k = k_ref[0, :, pl.ds(pl.multiple_of(kb * tile, tile), tile)]
        fk = fk_ref[0, 0, kb]
        u0 = _dot(q0, k) - fk[0:1, :]
        u1 = _dot(q1, k) - fk[1:2, :]
        if mask is not None:
            u0, u1 = jnp.where(mask, u0, NEG_BIG), jnp.where(mask, u1, NEG_BIG)
        return u0, u1

    causal = _causal(tile, tile, False)

    def row_max(us, m0, m1):
        return (jnp.maximum(m0, jnp.max(us[0], axis=1, keepdims=True) + fq0),
                jnp.maximum(m1, jnp.max(us[1], axis=1, keepdims=True) + fq1))

    def best_gap(kb, m0, m1):
        return jnp.max(reach - _per_head(m0, m1, shape) - last_fk(kb))

    def max_step(s):
        i = s[0]
        m0, m1 = row_max(scores(qi - 1 - i, None), s[2], s[3])
        return i + 1, best_gap(qi - 2 - i, m0, m1), m0, m1

    neg = jnp.full((tile, 1), NEG_BIG, F32)
    diagonal = scores(qi, causal)
    m0, m1 = row_max(diagonal, neg, neg)
    _, _, m0, m1 = lax.while_loop(lambda s: jnp.logical_and(s[0] < qi, s[1] > 0.0), max_step,
                                  (jnp.int32(0), best_gap(qi - 1, m0, m1), m0, m1))

    c0, c1 = m0 - fq0, m1 - fq1
    gap_rows = jnp.max(reach - _per_head(m0, m1, shape), axis=0, keepdims=True)

    def weigh(kb, acc0, acc1, us):
        u0, u1 = us
        v = v_ref[0, :, pl.ds(pl.multiple_of(kb * tile, tile), tile)]
        acc0 = acc0 + _dot_nt(jnp.exp(u0 - c0).astype(BF16), _keep_head(v, True, 0, 1.0))
        acc1 = acc1 + _dot_nt(jnp.exp(u1 - c1).astype(BF16), _keep_head(v, False, 0, 1.0))
        return acc0, acc1

    def head_gaps(kb):
        gap = gap_rows - last_fk(kb)
        first = lax.broadcasted_iota(jnp.int32, gap.shape, 1) < HEAD_DIM
        return jnp.max(jnp.where(first, gap, NEG_BIG)), jnp.max(jnp.where(first, NEG_BIG, gap))

    def sum_step(s):
        i = s[0]
        acc0, acc1 = weigh(qi - 1 - i, s[3], s[4], scores(qi - 1 - i, None))
        return (i + 1,) + head_gaps(qi - 2 - i) + (acc0, acc1)

    zero = jnp.zeros(shape, F32)
    acc0, acc1 = weigh(qi, zero, zero, diagonal)
    done, gap0, gap1, acc0, acc1 = lax.while_loop(
        lambda s: jnp.logical_and(s[0] < qi, jnp.minimum(s[1], s[2]) > EXP_ZERO), sum_step,
        (jnp.int32(0),) + head_gaps(qi - 1) + (acc0, acc1))

    half_tile = tile // 2

    def lone(second, start, gap):
        q_s, c_s = jnp.where(second, q1, q0), jnp.where(second, c1, c0)

        def lone_step(s):
            i = s[0]
            cols = pl.ds(pl.multiple_of((qi - 1 - i) * tile, tile), tile)
            k, v = k_ref[0, :, cols], v_ref[0, :, cols]
            fk = fk_ref[0, 0, qi - 1 - i]
            fk_s = jnp.where(second, fk[1:2, :], fk[0:1, :])
            ones_rows = (lax.broadcasted_iota(jnp.int32, v.shape, 0) < HEAD_DIM) == second
            v_s = jnp.where(ones_rows, jnp.ones_like(v), v)
            us = [_dot(q_s, k[:, lo:lo + half_tile]) - fk_s[:, lo:lo + half_tile] for lo in (0, half_tile)]
            ps = [jnp.exp(u - c_s).astype(BF16) for u in us]
            acc = s[2] + _dot_nt(ps[0], v_s[:, :half_tile]) + _dot_nt(ps[1], v_s[:, half_tile:])
            gaps = head_gaps(qi - 2 - i)
            return i + 1, jnp.where(second, gaps[1], gaps[0]), acc

        return lax.while_loop(lambda s: jnp.logical_and(s[0] < qi, s[1] > EXP_ZERO), lone_step,
                              (start, gap, zero))[2]

    second = gap1 > gap0
    acc_s = lone(second, done, jnp.maximum(gap0, gap1))
    acc0 = acc0 + jnp.where(second, 0.0, acc_s)
    acc1 = acc1 + jnp.where(second, acc_s, 0.0)
    lane = lax.broadcasted_iota(jnp.int32, shape, 1)
    half = LANES // 2
    o_ref[0] = jnp.where(lane < HEAD_DIM, acc0 / pltpu.roll(acc0, half, 1), acc1 / pltpu.roll(acc1, half, 1))


def _prompt_specs(t, tile):
    qspec = pl.BlockSpec((1, tile, LANES), lambda b, h, i: (b, i, h))
    kvspec = pl.BlockSpec((1, LANES, t), lambda b, h, i: (b, h, 0))
    return qspec, kvspec


def _sb_prompt(q, k, v):
    b, t, w = q.shape
    tile = ATT_BLOCK
    blocks = SB_BLOCKS_PER_STEP if t % (tile * SB_BLOCKS_PER_STEP) == 0 else 1
    qspec, kvspec = _prompt_specs(t, tile * blocks)
    return pl.pallas_call(
        functools.partial(_sb_prompt_kernel, tile=tile, blocks=blocks),
        grid=(b, w // LANES, t // (tile * blocks)),
        in_specs=[qspec, kvspec, kvspec],
        out_specs=qspec,
        out_shape=jax.ShapeDtypeStruct((b, t, w), F32),
        compiler_params=_params("arbitrary", "arbitrary", "arbitrary"),
        name="sb_prompt",
    )(q, k, v)


def _fox_prompt(q, k, v, fq, fk):
    b, t, w = q.shape
    tile = fk.shape[-1]
    qspec, kvspec = _prompt_specs(t, tile)
    n_f = fq.shape[-1]
    return pl.pallas_call(
        functools.partial(_fox_prompt_kernel, tile=tile),
        grid=(b, w // LANES, t // tile),
        in_specs=[qspec, kvspec, kvspec,
                  pl.BlockSpec((1, tile, n_f), lambda b, h, i: (b, i, 0)),
                  pl.BlockSpec((1, 1, t // tile, 2, tile), lambda b, h, i: (b, h, 0, 0, 0))],
        out_specs=qspec,
        out_shape=jax.ShapeDtypeStruct((b, t, w), F32),
        scratch_shapes=[pltpu.VMEM((1, LANES), F32)],
        compiler_params=_params("arbitrary", "arbitrary", "arbitrary"),
        name="fox_prompt",
    )(q, k, v, fq, fk)


def _group_rows(g):
    return slice(g * LANES, (g + 1) * LANES)


def _sb_sample_kernel(q_ref, kn_ref, vn_ref, kc_ref, vc_ref, o_ref, *, tile):
    n, width = q_ref.shape[1:]
    groups = width // LANES
    past = kc_ref.shape[2]
    qs = []
    for g in range(groups):
        q = q_ref[0, :, _group_rows(g)]
        qs.append((_keep_head(q, True), _keep_head(q, False)))

    def visit(kbs, state, mask):
        kvs = []
        for g in range(groups):
            if mask is not None:
                kvs.append((kn_ref[0, _group_rows(g), :], vn_ref[0, _group_rows(g), :]))
            else:
                cols = pl.ds(pl.multiple_of(kbs[g] * tile, tile), tile)
                kvs.append((kc_ref[0, _group_rows(g), cols].astype(BF16),
                            vc_ref[0, _group_rows(g), cols].astype(BF16)))
        return _sb_visit(qs, kvs, state, mask)

    state = visit(None, _sb_start(n, groups), _causal(n, n, True))
    state = _sb_sweep(visit, [past // tile] * groups, state)
    for g in range(groups):
        o_ref[0, :, _group_rows(g)] = state[2 + 3 * g]


def _fox_sample_kernel(q_ref, kn_ref, vn_ref, kc_ref, vc_ref, fq_ref, fk_ref, o_ref):
    n, width = q_ref.shape[1:]
    groups = width // LANES
    past = kc_ref.shape[2]
    fq = fq_ref[0]
    causal = _causal(n, n, False)
    scored = []
    for g in range(groups):
        q = q_ref[0, :, _group_rows(g)]
        kn = kn_ref[0, _group_rows(g), :]
        kc = kc_ref[0, _group_rows(g), :].astype(BF16)
        for h in range(2):
            qh = _keep_head(q, h == 0)
            fk = fk_ref[0, g, h:h + 1, :]
            u_new = jnp.where(causal, _dot(qh, kn) - fk[:, past:past + n], NEG_BIG)
            u_old = _dot(qh, kc) - fk[:, 0:past]
            scored.append((u_new, u_old, fq[:, 2 * g + h:2 * g + h + 1]))
    weights = []
    for u_new, u_old, fq_col in scored:
        m = jnp.maximum(jnp.max(u_new, axis=1, keepdims=True), jnp.max(u_old, axis=1, keepdims=True)) + fq_col
        c = m - fq_col
        weights.append((jnp.exp(u_new - c).astype(BF16), jnp.exp(u_old - c).astype(BF16)))
    lane = lax.broadcasted_iota(jnp.int32, (n, LANES), 1)
    half = LANES // 2
    for g in range(groups):
        vn = vn_ref[0, _group_rows(g), :]
        vc = vc_ref[0, _group_rows(g), :].astype(BF16)
        accs = []
        for h in range(2):
            p_new, p_old = weights[2 * g + h]
            accs.append(_dot_nt(p_new, _keep_head(vn, h == 0, 0, 1.0))
                        + _dot_nt(p_old, _keep_head(vc, h == 0, 0, 1.0)))
        o_ref[0, :, _group_rows(g)] = jnp.where(lane < HEAD_DIM, accs[0] / pltpu.roll(accs[0], half, 1),
                                                accs[1] / pltpu.roll(accs[1], half, 1))


def _sample_specs(n, width, past):
    qspec = pl.BlockSpec((1, n, width), lambda b: (b, 0, 0))
    new = pl.BlockSpec((1, width, n), lambda b: (b, 0, 0))
    cache = pl.BlockSpec((1, width, past), lambda b: (b, 0, 0))
    return qspec, new, cache


def _sb_sample(q, kn, vn, kc, vc):
    b, n, w = q.shape
    qspec, new, cache = _sample_specs(n, w, kc.shape[2])
    return pl.pallas_call(
        functools.partial(_sb_sample_kernel, tile=ATT_BLOCK),
        grid=(b,),
        in_specs=[qspec, new, new, cache, cache],
        out_specs=qspec,
        out_shape=jax.ShapeDtypeStruct((b, n, w), F32),
        compiler_params=_params("arbitrary"),
        name="sb_sample",
    )(q, kn, vn, kc, vc)


def _fox_sample(q, kn, vn, kc, vc, fq, fk):
    b, n, w = q.shape
    qspec, new, cache = _sample_specs(n, w, kc.shape[2])
    return pl.pallas_call(
        _fox_sample_kernel,
        grid=(b,),
        in_specs=[qspec, new, new, cache, cache,
                  pl.BlockSpec((1, n, fq.shape[-1]), lambda b: (b, 0, 0)),
                  pl.BlockSpec((1,) + fk.shape[1:], lambda b: (b, 0, 0, 0))],
        out_specs=qspec,
        out_shape=jax.ShapeDtypeStruct((b, n, w), F32),
        compiler_params=_params("arbitrary"),
        name="fox_sample",
    )(q, kn, vn, kc, vc, fq, fk)


def _post_kernel(x_ref, osb_ref, ofx_ref, ada_ref, gsb_ref, gfx_ref, wo_ref, gffn_ref,
                 wg_ref, wu_ref, wd_ref, gfin_ref, y_ref, *, final_norm):
    nb, tt, d = x_ref.shape
    rows = nb * tt
    ada = ada_ref[...]
    o_sb = _rms(osb_ref[...], gsb_ref[...])
    o_fx = _rms(ofx_ref[...], gfx_ref[...])
    sb_w, fx_w = o_sb.shape[-1], o_fx.shape[-1]
    proj = (_dot(o_sb.reshape(rows, sb_w).astype(BF16), wo_ref[0:sb_w, :])
            + _dot(o_fx.reshape(rows, fx_w).astype(BF16), wo_ref[sb_w:sb_w + fx_w, :]))
    x2 = x_ref[...] + (1.0 + ada[:, 2:3, :]) * proj.reshape(nb, tt, d)
    h = _rms(x2, gffn_ref[...]) * (1.0 + ada[:, 4:5, :]) + ada[:, 3:4, :]
    h = h.reshape(rows, d).astype(BF16)
    d_ff = wg_ref.shape[1]
    f = None
    for lo in range(0, d_ff, FF_CHUNK):
        hi = min(lo + FF_CHUNK, d_ff)
        g = _dot(h, wg_ref[:, lo:hi])
        u = _dot(h, wu_ref[:, lo:hi])
        act = (g * (1.0 / (1.0 + jnp.exp(-g))) * u).astype(BF16)
        part = _dot(act, wd_ref[lo:hi, :])
        f = part if f is None else f + part
    x3 = x2 + (1.0 + ada[:, 5:6, :]) * f.reshape(nb, tt, d)
    y_ref[...] = _rms(x3, gfin_ref[...]) if final_norm else x3


def _post(x, o_sb, o_fx, ada, g_sb, g_fx, wo16, g_ffn, wg16, wu16, wd16, g_fin, final_norm):
    b, t, d = x.shape
    tt = min(t, ROW_BLOCK)
    nb = max(1, ROW_BLOCK // tt)
    tok = lambda w: pl.BlockSpec((nb, tt, w), lambda i, j: (i, j, 0))
    const = lambda a: pl.BlockSpec(a.shape, lambda i, j: (0, 0), pipeline_mode=pl.Buffered(1))
    return pl.pallas_call(
        functools.partial(_post_kernel, final_norm=final_norm),
        grid=(b // nb, t // tt),
        in_specs=[tok(d), tok(o_sb.shape[-1]), tok(o_fx.shape[-1]),
                  pl.BlockSpec((nb, 6, d), lambda i, j: (i, 0, 0)),
                  const(g_sb), const(g_fx), const(wo16), const(g_ffn),
                  const(wg16), const(wu16), const(wd16), const(g_fin)],
        out_specs=tok(d),
        out_shape=jax.ShapeDtypeStruct((b, t, d), F32),
        compiler_params=_params("arbitrary", "arbitrary"),
        name="post_mixer_ffn",
    )(x, o_sb, o_fx, ada, g_sb, g_fx, wo16, g_ffn, wg16, wu16, wd16, g_fin)


def _pad_lanes(a, n):
    return jnp.pad(a, ((0, 0), (0, n - a.shape[1])))


def kernel(x_prompt, x_sample, c_prompt, c_sample, cache_sb_k, cache_sb_v, cache_fox_k, cache_fox_v, cache_fox_logf, w_ada, b_ada, g_mix, w_in, b_f, g_sb_out, g_fox_out, w_o, g_ffn, w_gate, w_up, w_down, g_final):
    depth = w_ada.shape[0]
    bp, tp, d = x_prompt.shape
    bs, ts, _ = x_sample.shape
    past, n_sb, hd = cache_sb_k.shape[2:]
    n_fx = cache_fox_k.shape[3]
    assert hd == HEAD_DIM and n_sb % 2 == 0 and n_fx % 2 == 0
    assert tp % ROW_BLOCK == 0 and tp % ATT_BLOCK == 0 and ROW_BLOCK % ts == 0 and bs % (ROW_BLOCK // ts) == 0
    assert past % ATT_BLOCK == 0
    sb_w, fx_w = n_sb * hd, n_fx * hd
    qkv_cols = 3 * sb_w + 3 * fx_w
    att = min(FOX_BLOCK, tp)
    past_pad = -(-(past + ts) // (3 * LANES)) * (3 * LANES)
    row = lambda a: a.reshape(1, -1)

    xp, xs = x_prompt, x_sample
    outs = [[] for _ in range(10)]
    for l in range(depth):
        w16 = w_in[l][:, :qkv_cols].astype(BF16)
        w16t = jnp.swapaxes(w_in[l], 0, 1)[:qkv_cols].astype(BF16)
        wf16 = _pad_lanes(w_in[l][:, qkv_cols:], LANES).astype(BF16)
        bfp = _pad_lanes(row(b_f[l]), LANES)
        wo16, wg16 = w_o[l].astype(BF16), w_gate[l].astype(BF16)
        wu16, wd16 = w_up[l].astype(BF16), w_down[l].astype(BF16)

        c_all = jnp.concatenate([c_prompt, c_sample], axis=0)
        ada = _ada(c_all, w_ada[l], row(b_ada[l])).reshape(bp + bs, 6, d)
        ada_p, ada_s = ada[:bp], ada[bp:]

        (q_sb, k_sb, v_sb, k_sb16, v_sb16, q_fx, k_fx, v_fx, k_fx16, v_fx16, lf) = _pre(
            xp, ada_p, row(g_mix[l]), w16, w16t, wf16, bfp, sb_w, fx_w, n_fx, True)
        o_sb = _sb_prompt(q_sb, k_sb16, v_sb16)
        f_cum = _cumsum_rows(jnp.swapaxes(lf, 1, 2).reshape(bp * n_fx, tp), ROW_BLOCK)
        fq = jnp.swapaxes(f_cum.reshape(bp, n_fx, tp), 1, 2)
        fk = jnp.swapaxes(f_cum.reshape(bp, n_fx // 2, 2, tp // att, att), 2, 3)
        o_fx = _fox_prompt(q_fx, k_fx16, v_fx16, fq, fk)
        xp = _post(xp, o_sb, o_fx, ada_p, row(g_sb_out[l]), row(g_fox_out[l]), wo16, row(g_ffn[l]),
                   wg16, wu16, wd16, row(g_final), l + 1 == depth)
        for dst, a in zip(outs[0:4], (k_sb, v_sb, k_fx, v_fx)):
            dst.append(jnp.swapaxes(a, 1, 2).reshape(bp, tp, -1, hd))
        outs[4].append(lf)

        (q_sb, k_sb, v_sb, k_sb16, v_sb16, q_fx, k_fx, v_fx, k_fx16, v_fx16, lf) = _pre(
            xs, ada_s, row(g_mix[l]), w16, w16t, wf16, bfp, sb_w, fx_w, n_fx, False)
        time_minor = lambda a: jnp.swapaxes(a.reshape(a.shape[0], a.shape[1], -1), 1, 2)
        o_sb = _sb_sample(q_sb, time_minor(k_sb16), time_minor(v_sb16),
                          time_minor(cache_sb_k[l]), time_minor(cache_sb_v[l]))
        lf_all = jnp.concatenate([cache_fox_logf[l].astype(F32), lf], axis=1)
        lf_rows = _pad_lanes(jnp.swapaxes(lf_all, 1, 2).reshape(bs * n_fx, past + ts), past_pad)
        f_all = _cumsum_rows(lf_rows, past_pad // 3)
        fq = jnp.swapaxes(f_all.reshape(bs, n_fx, past_pad)[:, :, past:past + ts], 1, 2)
        fk = f_all.reshape(bs, n_fx // 2, 2, past_pad)
        o_fx = _fox_sample(q_fx, time_minor(k_fx16), time_minor(v_fx16),
                           time_minor(cache_fox_k[l]), time_minor(cache_fox_v[l]), fq, fk)
        xs = _post(xs, o_sb, o_fx, ada_s, row(g_sb_out[l]), row(g_fox_out[l]), wo16, row(g_ffn[l]),
                   wg16, wu16, wd16, row(g_final), l + 1 == depth)
        for dst, a in zip(outs[5:9], (k_sb, v_sb, k_fx, v_fx)):
            dst.append(a.reshape(bs, ts, -1, hd))
        outs[9].append(lf)

    return (xp, xs) + tuple(jnp.stack(o) for o in outs)
```

```python
import functools

import jax
import jax.numpy as jnp
from jax import lax
from jax.experimental import pallas as pl
from jax.experimental.pallas import tpu as pltpu

F32 = jnp.float32
BF16 = jnp.bfloat16
EPS = 1e-6
HEAD_DIM = 64
LANES = 128
BF16_ROWS = 16
ROW_BLOCK = 512
FF_CHUNK = 512
ATT_BLOCK = 256
SB_BLOCKS_PER_STEP = 4
FOX_BLOCK = 512
NEG_BIG = -1e30
EXP_ZERO = -105.0
NORM_CHUNK = 2048
NORM_SLACK = 1.0 + 2.0 ** -8
VMEM_LIMIT = 56 * 1024 * 1024


def _params(*semantics):
    return pltpu.CompilerParams(dimension_semantics=semantics, vmem_limit_bytes=VMEM_LIMIT)


def _dot(a, b):
    return jnp.dot(a, b, preferred_element_type=F32)


def _dot_nt(a, b):
    return lax.dot_general(a, b, (((1,), (1,)), ((), ())), preferred_element_type=F32)


def _rms(x, g):
    return x * lax.rsqrt(jnp.mean(x * x, axis=-1, keepdims=True) + EPS) * g


def _split_bf16(x, parts):
    out = []
    for _ in range(parts - 1):
        p = x.astype(BF16)
        out.append(p)
        x = x - p.astype(F32)
    out.append(x.astype(BF16))
    return out


def _dot_split(x, m, parts):
    acc = None
    for p in _split_bf16(x, parts):
        d = _dot(p, m)
        acc = d if acc is None else acc + d
    return acc


def _keep_head(x, first, axis=1, fill=0.0):
    pos = lax.broadcasted_iota(jnp.int32, x.shape, axis)
    keep = (pos < HEAD_DIM) if first else (pos >= HEAD_DIM)
    return jnp.where(keep, x, jnp.full_like(x, fill))


def _per_head(col0, col1, shape):
    lane = lax.broadcasted_iota(jnp.int32, shape, 1)
    return jnp.where(lane < HEAD_DIM, col0, col1)


def _lanes_to(x, n):
    if n <= LANES:
        return x[:, :n]
    return jnp.concatenate([x] * (n // LANES), axis=1)


def _ada_kernel(c_ref, w_ref, b_ref, o_ref):
    c = c_ref[...]
    s = (c * (1.0 / (1.0 + jnp.exp(-c)))).astype(BF16)
    o_ref[...] = _dot(s, w_ref[...].astype(BF16)) + b_ref[...]


def _ada(c, w, b):
    rows, d = c.shape
    n = w.shape[1]
    tn = 1024
    return pl.pallas_call(
        _ada_kernel,
        grid=(n // tn,),
        in_specs=[pl.BlockSpec((rows, d), lambda j: (0, 0)),
                  pl.BlockSpec((d, tn), lambda j: (0, j)),
                  pl.BlockSpec((1, tn), lambda j: (0, j))],
        out_specs=pl.BlockSpec((rows, tn), lambda j: (0, j)),
        out_shape=jax.ShapeDtypeStruct((rows, n), F32),
        compiler_params=_params("arbitrary"),
        name="ada",
    )(c, w, b)


def _log_sigmoid(u):
    return jnp.minimum(u, 0.0) - jnp.log(1.0 + jnp.exp(-jnp.abs(u)))


def _pre_kernel(x_ref, ada_ref, g_ref, w_ref, wt_ref, wf_ref, bf_ref, bfc_ref,
                qsb_ref, ksb_ref, vsb_ref, ksb16_ref, vsb16_ref,
                qfx_ref, kfx_ref, vfx_ref, kfx16_ref, vfx16_ref, lf_ref, *, sb_w, fx_w, n_f, time_minor):
    x = x_ref[...]
    nb, tt, d = x.shape
    ada = ada_ref[...]
    h = _rms(x, g_ref[...]) * (1.0 + ada[:, 1:2, :]) + ada[:, 0:1, :]
    h = h.reshape(nb * tt, d).astype(BF16)
    q_scale = HEAD_DIM ** -0.5

    def proj(lo, width):
        return _dot(h, w_ref[:, lo:lo + width]).reshape(nb, tt, width)

    base = 3 * sb_w
    qsb_ref[...] = (proj(0, sb_w) * q_scale).astype(BF16)
    qfx_ref[...] = (proj(base, fx_w) * q_scale).astype(BF16)
    if time_minor:
        sb = _dot_nt(wt_ref[sb_w:base, :], h)
        fx = _dot_nt(wt_ref[base + fx_w:, :], h)
        parts = (sb[:sb_w], sb[sb_w:], fx[:fx_w], fx[fx_w:2 * fx_w])
        lf_ref[...] = _log_sigmoid(fx[2 * fx_w:2 * fx_w + n_f] + bfc_ref[...])[None]
    else:
        parts = (proj(sb_w, sb_w), proj(2 * sb_w, sb_w), proj(base + fx_w, fx_w), proj(base + 2 * fx_w, fx_w))
        lf = _log_sigmoid(_dot(h, wf_ref[...]) + bf_ref[...])
        lf_ref[...] = lf[:, :n_f].reshape(nb, tt, n_f)
    for r, out, out16 in zip(parts, (ksb_ref, vsb_ref, kfx_ref, vfx_ref),
                             (ksb16_ref, vsb16_ref, kfx16_ref, vfx16_ref)):
        r = r[None] if time_minor else r
        out[...] = r
        out16[...] = r.astype(BF16)


def _pre(x, ada, g, w16, w16t, wf16, bfp, bfc, sb_w, fx_w, n_f, time_minor):
    b, t, d = x.shape
    tt = min(t, ROW_BLOCK)
    nb = max(1, ROW_BLOCK // tt)
    assert nb == 1 or not time_minor
    grid = (b // nb, t // tt)
    tok = lambda w: pl.BlockSpec((nb, tt, w), lambda i, j: (i, j, 0))
    const = lambda a: pl.BlockSpec(a.shape, lambda i, j: (0, 0))
    shape = lambda w, dt: jax.ShapeDtypeStruct((b, t, w), dt)
    if time_minor:
        kv = lambda w: pl.BlockSpec((1, w, tt), lambda i, j: (i, 0, j))
        kv_shape = lambda w, dt: jax.ShapeDtypeStruct((b, w, t), dt)
    else:
        kv, kv_shape = tok, shape
    return pl.pallas_call(
        functools.partial(_pre_kernel, sb_w=sb_w, fx_w=fx_w, n_f=n_f, time_minor=time_minor),
        grid=grid,
        in_specs=[tok(d), pl.BlockSpec((nb, 6, d), lambda i, j: (i, 0, 0)),
                  const(g), const(w16), const(w16t), const(wf16), const(bfp), const(bfc)],
        out_specs=[tok(sb_w)] + [kv(sb_w)] * 4 + [tok(fx_w)] + [kv(fx_w)] * 4 + [kv(n_f)],
        out_shape=[shape(sb_w, BF16), kv_shape(sb_w, F32), kv_shape(sb_w, F32), kv_shape(sb_w, BF16),
                   kv_shape(sb_w, BF16),
                   shape(fx_w, BF16), kv_shape(fx_w, F32), kv_shape(fx_w, F32), kv_shape(fx_w, BF16),
                   kv_shape(fx_w, BF16),
                   kv_shape(n_f, F32)],
        compiler_params=_params("arbitrary", "arbitrary"),
        name="pre_mixer",
    )(x, ada, g, w16, w16t, wf16, bfp, bfc)


def _cumsum_kernel(x_ref, o_ref, carry_ref):
    @pl.when(pl.program_id(0) == 0)
    def _():
        carry_ref[...] = jnp.zeros_like(carry_ref)

    x = x_ref[...]
    tc = x.shape[1]
    r = lax.broadcasted_iota(jnp.int32, (tc, tc), 0)
    c = lax.broadcasted_iota(jnp.int32, (tc, tc), 1)
    upper = jnp.where(r <= c, 1.0, 0.0).astype(BF16)
    carry = carry_ref[...]
    o_ref[...] = _dot_split(x, upper, 3) + _lanes_to(carry, tc)
    carry_ref[...] = carry + _dot_split(x, jnp.ones((tc, LANES), BF16), 3)


def _cumsum_rows(x, tc):
    rows, t = x.shape
    return pl.pallas_call(
        _cumsum_kernel,
        grid=(t // tc,),
        in_specs=[pl.BlockSpec((rows, tc), lambda j: (0, j))],
        out_specs=pl.BlockSpec((rows, tc), lambda j: (0, j)),
        out_shape=jax.ShapeDtypeStruct((rows, t), F32),
        scratch_shapes=[pltpu.VMEM((rows, LANES), F32)],
        compiler_params=_params("arbitrary"),
        name="cumsum_time",
    )(x)


def _tri_strict(n):
    r = lax.broadcasted_iota(jnp.int32, (n, n), 0)
    c = lax.broadcasted_iota(jnp.int32, (n, n), 1)
    return jnp.where(r > c, 1.0, 0.0).astype(BF16)


def _causal(tq, tk, strict):
    r = lax.broadcasted_iota(jnp.int32, (tq, tk), 0)
    c = lax.broadcasted_iota(jnp.int32, (tq, tk), 1)
    return (c < r) if strict else (c <= r)


def _sb_tiles(chains, mask):
    tk = chains[0][1].shape[1]
    tri = _tri_strict(tk)
    wide = tk % LANES == 0
    if wide:
        tri2 = jnp.concatenate([tri, tri], axis=0)
    zs = [_dot(q, k) for q, k, _, _ in chains]
    staged = []
    for z in zs:
        log_beta = jnp.minimum(z, 0.0) - jnp.log(1.0 + jnp.exp(-jnp.abs(z)))
        log_keep = log_beta - z
        if mask is not None:
            log_keep = jnp.where(mask, log_keep, 0.0)
        hi, lo = _split_bf16(log_keep, 2)
        staged.append((log_beta, log_keep[:, 0:1], hi, lo))
    if wide:
        betweens = [_dot(jnp.concatenate([hi, lo], axis=1), tri2) for _, _, hi, lo in staged]
    else:
        betweens = [_dot(hi, tri) + _dot(lo, tri) for _, _, hi, lo in staged]
    weights = []
    for (log_beta, _, _, _), between, (_, _, _, carry) in zip(staged, betweens, chains):
        a = jnp.exp(log_beta + between + carry)
        if mask is not None:
            a = jnp.where(mask, a, 0.0)
        weights.append(a.astype(BF16))
    pvs = [_dot_nt(a, v) for a, (_, _, v, _) in zip(weights, chains)]
    return [(pv, carry + (between[:, 0:1] + first))
            for pv, between, (_, first, _, _), (_, _, _, carry) in zip(pvs, betweens, staged, chains)]


def _sb_start(rows, groups):
    col = jnp.zeros((rows, 1), F32)
    return (col, col, jnp.zeros((rows, LANES), F32)) * groups


def _sb_visit(qs, kvs, state, mask):
    chains = []
    for g, ((q0, q1), (k, v)) in enumerate(zip(qs, kvs)):
        chains.append((q0, k, _keep_head(v, True, 0), state[3 * g]))
        chains.append((q1, k, _keep_head(v, False, 0), state[3 * g + 1]))
    res = _sb_tiles(chains, mask)
    out = ()
    for g in range(len(qs)):
        (pv0, c0), (pv1, c1) = res[2 * g], res[2 * g + 1]
        out += (c0, c1, state[3 * g + 2] + pv0 + pv1)
    return out


def _sb_sweep(visit, lefts, state):
    groups = len(lefts)

    def live(s):
        alive = jnp.bool_(False)
        for g in range(groups):
            more = jnp.max(jnp.maximum(s[1 + 3 * g], s[2 + 3 * g])) > EXP_ZERO
            alive = jnp.logical_or(alive, jnp.logical_and(s[0] < lefts[g], more))
        return alive

    def step(s):
        i, old = s[0], s[1:]
        new = visit([jnp.maximum(lefts[g] - 1 - i, 0) for g in range(groups)], old, None)
        out = (i + 1,)
        for g in range(groups):
            inside = i < lefts[g]
            out += tuple(jnp.where(inside, n, o) for n, o in zip(new[3 * g:3 * g + 3], old[3 * g:3 * g + 3]))
        return out

    return lax.while_loop(live, step, (jnp.int32(0),) + tuple(state))[1:]


def _head_column(f, h):
    lane = lax.broadcasted_iota(jnp.int32, f.shape, 1)
    return jnp.sum(jnp.where(lane == h, f, 0.0), axis=1, keepdims=True)


def _sb_prompt_kernel(q_ref, k_ref, v_ref, o_ref, *, tile, blocks):
    first = pl.program_id(2) * blocks
    qs = []
    for j in range(blocks):
        q = q_ref[0, j * tile:(j + 1) * tile, :]
        qs.append((_keep_head(q, True), _keep_head(q, False)))

    def visit(kbs, state, mask):
        kvs = []
        for j in range(blocks):
            start = pl.multiple_of(kbs[j] * tile, tile)
            kvs.append((k_ref[0, :, pl.ds(start, tile)], v_ref[0, :, pl.ds(start, tile)]))
        return _sb_visit(qs, kvs, state, mask)

    lefts = [first + j for j in range(blocks)]
    state = visit(lefts, _sb_start(tile, blocks), _causal(tile, tile, True))
    state = _sb_sweep(visit, lefts, state)
    for j in range(blocks):
        o_ref[0, j * tile:(j + 1) * tile, :] = state[2 + 3 * j]


def _head_norms_sq(x16):
    x = x16.astype(F32)
    r = lax.broadcasted_iota(jnp.int32, (LANES, LANES), 0)
    c = lax.broadcasted_iota(jnp.int32, (LANES, LANES), 1)
    same_head = jnp.where((r < HEAD_DIM) == (c < HEAD_DIM), 1.0, 0.0).astype(BF16)
    return _dot_split(x * x, same_head, 2)


def _fox_prompt_kernel(q_ref, k_ref, v_ref, fq_ref, fk_ref, o_ref, kmax_ref, *, tile):
    hp = pl.program_id(1)
    qi = pl.program_id(2)
    q = q_ref[0]
    q0, q1 = _keep_head(q, True), _keep_head(q, False)
    fq = fq_ref[0]
    fq0, fq1 = _head_column(fq, 2 * hp), _head_column(fq, 2 * hp + 1)
    shape = (tile, LANES)

    @pl.when(qi == 0)
    def _():
        width = min(NORM_CHUNK, k_ref.shape[2])

        def chunk(i, best):
            kc = k_ref[0, :, pl.ds(pl.multiple_of(i * width, width), width)].astype(F32)
            sq = kc * kc
            return (jnp.maximum(best[0], jnp.sum(sq[:HEAD_DIM], axis=0, keepdims=True)),
                    jnp.maximum(best[1], jnp.sum(sq[HEAD_DIM:], axis=0, keepdims=True)))

        zero = jnp.zeros((1, width), F32)
        best = lax.fori_loop(0, k_ref.shape[2] // width, chunk, (zero, zero))
        kmax_ref[...] = _per_head(jnp.max(best[0], axis=1, keepdims=True),
                                  jnp.max(best[1], axis=1, keepdims=True), (1, LANES))

    reach = (jnp.sqrt(_head_norms_sq(q)) * jnp.sqrt(kmax_ref[...]) * NORM_SLACK
             + _per_head(fq0, fq1, shape))

    def last_fk(kb):
        fk = fk_ref[0, 0, jnp.maximum(kb, 0)]
        return _per_head(fk[0:1, tile - 1:tile], fk[1:2, tile - 1:tile], (1, LANES))

    def scores(kb):
        k = k_ref[0, :, pl.ds(pl.multiple_of(kb * tile, tile), tile)]
        fk = fk_ref[0, 0, kb]
        return _dot(q0, k) - fk[0:1, :], _dot(q1, k) - fk[1:2, :]

    half_tile = tile // 2
    causal = _causal(tile, tile, False)

    def diagonal_scores():
        k = k_ref[0, :, pl.ds(pl.multiple_of(qi * tile, tile), tile)]
        fk = fk_ref[0, 0, qi]
        raw = [(_dot(q, k[:, :half_tile]) - fk[r:r + 1, :half_tile],
                _dot(q[half_tile:], k[:, half_tile:]) - fk[r:r + 1, half_tile:]) for r, q in enumerate((q0, q1))]
        unseen = jnp.full((half_tile, half_tile), NEG_BIG, F32)
        return [(jnp.where(causal[:, :half_tile], left, NEG_BIG),
                 jnp.concatenate([unseen, jnp.where(causal[half_tile:, half_tile:], right, NEG_BIG)], axis=0))
                for left, right in raw]

    def row_max(us, m0, m1):
        return (jnp.maximum(m0, jnp.max(us[0], axis=1, keepdims=True) + fq0),
                jnp.maximum(m1, jnp.max(us[1], axis=1, keepdims=True) + fq1))

    def best_gap(kb, m0, m1):
        return jnp.max(reach - _per_head(m0, m1, shape) - last_fk(kb))

    def max_step(s):
        i = s[0]
        m0, m1 = row_max(scores(qi - 1 - i), s[2], s[3])
        return i + 1, best_gap(qi - 2 - i, m0, m1), m0, m1

    diagonal = diagonal_scores()
    m0, m1 = [jnp.maximum(jnp.max(left, axis=1, keepdims=True), jnp.max(right, axis=1, keepdims=True)) + fq
              for (left, right), fq in zip(diagonal, (fq0, fq1))]
    _, _, m0, m1 = lax.while_loop(lambda s: jnp.logical_and(s[0] < qi, s[1] > 0.0), max_step,
                                  (jnp.int32(0), best_gap(qi - 1, m0, m1), m0, m1))

    c0, c1 = m0 - fq0, m1 - fq1
    gap_rows = jnp.max(reach - _per_head(m0, m1, shape), axis=0, keepdims=True)

    def weigh(kb, acc0, acc1, us):
        u0, u1 = us
        v = v_ref[0, :, pl.ds(pl.multiple_of(kb * tile, tile), tile)]
        acc0 = acc0 + _dot_nt(jnp.exp(u0 - c0).astype(BF16), _keep_head(v, True, 0, 1.0))
        acc1 = acc1 + _dot_nt(jnp.exp(u1 - c1).astype(BF16), _keep_head(v, False, 0, 1.0))
        return acc0, acc1

    def head_gaps(kb):
        gap = gap_rows - last_fk(kb)
        first = lax.broadcasted_iota(jnp.int32, gap.shape, 1) < HEAD_DIM
        return jnp.max(jnp.where(first, gap, NEG_BIG)), jnp.max(jnp.where(first, NEG_BIG, gap))

    def sum_step(s):
        i = s[0]
        acc0, acc1 = weigh(qi - 1 - i, s[3], s[4], scores(qi - 1 - i))
        return (i + 1,) + head_gaps(qi - 2 - i) + (acc0, acc1)

    zero = jnp.zeros(shape, F32)
    v = v_ref[0, :, pl.ds(pl.multiple_of(qi * tile, tile), tile)]
    weights = [(jnp.exp(left - c).astype(BF16), jnp.exp(right - c).astype(BF16))
               for (left, right), c in zip(diagonal, (c0, c1))]
    blank = jnp.zeros((half_tile, LANES), F32)
    acc0, acc1 = [_dot_nt(left, vh[:, :half_tile])
                  + jnp.concatenate([blank, _dot_nt(right[half_tile:], vh[:, half_tile:])], axis=0)
                  for (left, right), vh in zip(weights, (_keep_head(v, True, 0, 1.0), _keep_head(v, False, 0, 1.0)))]
    done, gap0, gap1, acc0, acc1 = lax.while_loop(
        lambda s: jnp.logical_and(s[0] < qi, jnp.minimum(s[1], s[2]) > EXP_ZERO), sum_step,
        (jnp.int32(0),) + head_gaps(qi - 1) + (acc0, acc1))

    def lone(second, start, gap):
        q_s, c_s = jnp.where(second, q1, q0), jnp.where(second, c1, c0)

        def lone_step(s):
            i = s[0]
            cols = pl.ds(pl.multiple_of((qi - 1 - i) * tile, tile), tile)
            k, v = k_ref[0, :, cols], v_ref[0, :, cols]
            fk = fk_ref[0, 0, qi - 1 - i]
            fk_s = jnp.where(second, fk[1:2, :], fk[0:1, :])
            ones_rows = (lax.broadcasted_iota(jnp.int32, v.shape, 0) < HEAD_DIM) == second
            v_s = jnp.where(ones_rows, jnp.ones_like(v), v)
            us = [_dot(q_s, k[:, lo:lo + half_tile]) - fk_s[:, lo:lo + half_tile] for lo in (0, half_tile)]
            ps = [jnp.exp(u - c_s).astype(BF16) for u in us]
            acc = s[2] + _dot_nt(ps[0], v_s[:, :half_tile]) + _dot_nt(ps[1], v_s[:, half_tile:])
            gaps = head_gaps(qi - 2 - i)
            return i + 1, jnp.where(second, gaps[1], gaps[0]), acc

        return lax.while_loop(lambda s: jnp.logical_and(s[0] < qi, s[1] > EXP_ZERO), lone_step,
                              (start, gap, zero))[2]

    second = gap1 > gap0
    acc_s = lone(second, done, jnp.maximum(gap0, gap1))
    acc0 = acc0 + jnp.where(second, 0.0, acc_s)
    acc1 = acc1 + jnp.where(second, acc_s, 0.0)
    lane = lax.broadcasted_iota(jnp.int32, shape, 1)
    half = LANES // 2
    o_ref[0] = jnp.where(lane < HEAD_DIM, acc0 / pltpu.roll(acc0, half, 1), acc1 / pltpu.roll(acc1, half, 1))


def _prompt_specs(t, tile):
    qspec = pl.BlockSpec((1, tile, LANES), lambda b, h, i: (b, i, h))
    kvspec = pl.BlockSpec((1, LANES, t), lambda b, h, i: (b, h, 0))
    return qspec, kvspec


def _sb_prompt(q, k, v):
    b, t, w = q.shape
    tile = ATT_BLOCK
    blocks = SB_BLOCKS_PER_STEP if t % (tile * SB_BLOCKS_PER_STEP) == 0 else 1
    qspec, kvspec = _prompt_specs(t, tile * blocks)
    return pl.pallas_call(
        functools.partial(_sb_prompt_kernel, tile=tile, blocks=blocks),
        grid=(b, w // LANES, t // (tile * blocks)),
        in_specs=[qspec, kvspec, kvspec],
        out_specs=qspec,
        out_shape=jax.ShapeDtypeStruct((b, t, w), F32),
        compiler_params=_params("arbitrary", "arbitrary", "arbitrary"),
        name="sb_prompt",
    )(q, k, v)


def _fox_prompt(q, k, v, fq, fk):
    b, t, w = q.shape
    tile = fk.shape[-1]
    qspec, kvspec = _prompt_specs(t, tile)
    n_f = fq.shape[-1]
    return pl.pallas_call(
        functools.partial(_fox_prompt_kernel, tile=tile),
        grid=(b, w // LANES, t // tile),
        in_specs=[qspec, kvspec, kvspec,
                  pl.BlockSpec((1, tile, n_f), lambda b, h, i: (b, i, 0)),
                  pl.BlockSpec((1, 1, t // tile, 2, tile), lambda b, h, i: (b, h, 0, 0, 0))],
        out_specs=qspec,
        out_shape=jax.ShapeDtypeStruct((b, t, w), F32),
        scratch_shapes=[pltpu.VMEM((1, LANES), F32)],
        compiler_params=_params("arbitrary", "arbitrary", "arbitrary"),
        name="fox_prompt",
    )(q, k, v, fq, fk)


def _group_rows(g):
    return slice(g * LANES, (g + 1) * LANES)


def _sb_sample_kernel(q_ref, kn_ref, vn_ref, kc_ref, vc_ref, o_ref, *, tile):
    n, width = q_ref.shape[1:]
    groups = width // LANES
    past = kc_ref.shape[2]
    qs = []
    for g in range(groups):
        q = q_ref[0, :, _group_rows(g)]
        qs.append((_keep_head(q, True), _keep_head(q, False)))

    def visit(kbs, state, mask):
        kvs = []
        for g in range(groups):
            if mask is not None:
                kvs.append((kn_ref[0, _group_rows(g), :], vn_ref[0, _group_rows(g), :]))
            else:
                cols = pl.ds(pl.multiple_of(kbs[g] * tile, tile), tile)
                kvs.append((kc_ref[0, _group_rows(g), cols].astype(BF16),
                            vc_ref[0, _group_rows(g), cols].astype(BF16)))
        return _sb_visit(qs, kvs, state, mask)

    state = visit(None, _sb_start(n, groups), _causal(n, n, True))
    state = _sb_sweep(visit, [past // tile] * groups, state)
    for g in range(groups):
        o_ref[0, :, _group_rows(g)] = state[2 + 3 * g]


def _fox_sample_kernel(q_ref, kn_ref, vn_ref, kc_ref, vc_ref, fq_ref, fk_ref, o_ref):
    n, width = q_ref.shape[1:]
    groups = width // LANES
    past = kc_ref.shape[2]
    fq = fq_ref[0]
    causal = _causal(n, n, False)
    scored = []
    for g in range(groups):
        q = q_ref[0, :, _group_rows(g)]
        kn = kn_ref[0, _group_rows(g), :]
        kc = kc_ref[0, _group_rows(g), :].astype(BF16)
        for h in range(2):
            qh = _keep_head(q, h == 0)
            fk = fk_ref[0, g, h:h + 1, :]
            u_new = jnp.where(causal, _dot(qh, kn) - fk[:, past:past + n], NEG_BIG)
            u_old = _dot(qh, kc) - fk[:, 0:past]
            scored.append((u_new, u_old, fq[:, 2 * g + h:2 * g + h + 1]))
    weights = []
    for u_new, u_old, fq_col in scored:
        m = jnp.maximum(jnp.max(u_new, axis=1, keepdims=True), jnp.max(u_old, axis=1, keepdims=True)) + fq_col
        c = m - fq_col
        weights.append((jnp.exp(u_new - c).astype(BF16), jnp.exp(u_old - c).astype(BF16)))
    lane = lax.broadcasted_iota(jnp.int32, (n, LANES), 1)
    half = LANES // 2
    for g in range(groups):
        vn = vn_ref[0, _group_rows(g), :]
        vc = vc_ref[0, _group_rows(g), :].astype(BF16)
        accs = []
        for h in range(2):
            p_new, p_old = weights[2 * g + h]
            accs.append(_dot_nt(p_new, _keep_head(vn, h == 0, 0, 1.0))
                        + _dot_nt(p_old, _keep_head(vc, h == 0, 0, 1.0)))
        o_ref[0, :, _group_rows(g)] = jnp.where(lane < HEAD_DIM, accs[0] / pltpu.roll(accs[0], half, 1),
                                                accs[1] / pltpu.roll(accs[1], half, 1))


def _sample_specs(n, width, past):
    qspec = pl.BlockSpec((1, n, width), lambda b: (b, 0, 0))
    new = pl.BlockSpec((1, width, n), lambda b: (b, 0, 0))
    cache = pl.BlockSpec((1, width, past), lambda b: (b, 0, 0))
    return qspec, new, cache


def _sb_sample(q, kn, vn, kc, vc):
    b, n, w = q.shape
    qspec, new, cache = _sample_specs(n, w, kc.shape[2])
    return pl.pallas_call(
        functools.partial(_sb_sample_kernel, tile=ATT_BLOCK),
        grid=(b,),
        in_specs=[qspec, new, new, cache, cache],
        out_specs=qspec,
        out_shape=jax.ShapeDtypeStruct((b, n, w), F32),
        compiler_params=_params("arbitrary"),
        name="sb_sample",
    )(q, kn, vn, kc, vc)


def _fox_sample(q, kn, vn, kc, vc, fq, fk):
    b, n, w = q.shape
    qspec, new, cache = _sample_specs(n, w, kc.shape[2])
    return pl.pallas_call(
        _fox_sample_kernel,
        grid=(b,),
        in_specs=[qspec, new, new, cache, cache,
                  pl.BlockSpec((1, n, fq.shape[-1]), lambda b: (b, 0, 0)),
                  pl.BlockSpec((1,) + fk.shape[1:], lambda b: (b, 0, 0, 0))],
        out_specs=qspec,
        out_shape=jax.ShapeDtypeStruct((b, n, w), F32),
        compiler_params=_params("arbitrary"),
        name="fox_sample",
    )(q, kn, vn, kc, vc, fq, fk)


def _post_kernel(x_ref, osb_ref, ofx_ref, ada_ref, gsb_ref, gfx_ref, wo_ref, gffn_ref,
                 wg_ref, wu_ref, wd_ref, gfin_ref, y_ref, *, final_norm):
    nb, tt, d = x_ref.shape
    rows = nb * tt
    ada = ada_ref[...]
    o_sb = _rms(osb_ref[...], gsb_ref[...])
    o_fx = _rms(ofx_ref[...], gfx_ref[...])
    sb_w, fx_w = o_sb.shape[-1], o_fx.shape[-1]
    proj = (_dot(o_sb.reshape(rows, sb_w).astype(BF16), wo_ref[0:sb_w, :])
            + _dot(o_fx.reshape(rows, fx_w).astype(BF16), wo_ref[sb_w:sb_w + fx_w, :]))
    x2 = x_ref[...] + (1.0 + ada[:, 2:3, :]) * proj.reshape(nb, tt, d)
    h = _rms(x2, gffn_ref[...]) * (1.0 + ada[:, 4:5, :]) + ada[:, 3:4, :]
    h = h.reshape(rows, d).astype(BF16)
    d_ff = wg_ref.shape[1]
    f = None
    for lo in range(0, d_ff, FF_CHUNK):
        hi = min(lo + FF_CHUNK, d_ff)
        g = _dot(h, wg_ref[:, lo:hi])
        u = _dot(h, wu_ref[:, lo:hi])
        act = (g * (1.0 / (1.0 + jnp.exp(-g))) * u).astype(BF16)
        part = _dot(act, wd_ref[lo:hi, :])
        f = part if f is None else f + part
    x3 = x2 + (1.0 + ada[:, 5:6, :]) * f.reshape(nb, tt, d)
    y_ref[...] = _rms(x3, gfin_ref[...]) if final_norm else x3


def _post(x, o_sb, o_fx, ada, g_sb, g_fx, wo16, g_ffn, wg16, wu16, wd16, g_fin, final_norm):
    b, t, d = x.shape
    tt = min(t, ROW_BLOCK)
    nb = max(1, ROW_BLOCK // tt)
    tok = lambda w: pl.BlockSpec((nb, tt, w), lambda i, j: (i, j, 0))
    const = lambda a: pl.BlockSpec(a.shape, lambda i, j: (0, 0), pipeline_mode=pl.Buffered(1))
    return pl.pallas_call(
        functools.partial(_post_kernel, final_norm=final_norm),
        grid=(b // nb, t // tt),
        in_specs=[tok(d), tok(o_sb.shape[-1]), tok(o_fx.shape[-1]),
                  pl.BlockSpec((nb, 6, d), lambda i, j: (i, 0, 0)),
                  const(g_sb), const(g_fx), const(wo16), const(g_ffn),
                  const(wg16), const(wu16), const(wd16), const(g_fin)],
        out_specs=tok(d),
        out_shape=jax.ShapeDtypeStruct((b, t, d), F32),
        compiler_params=_params("arbitrary", "arbitrary"),
        name="post_mixer_ffn",
    )(x, o_sb, o_fx, ada, g_sb, g_fx, wo16, g_ffn, wg16, wu16, wd16, g_fin)


def _pad_lanes(a, n):
    return jnp.pad(a, ((0, 0), (0, n - a.shape[1])))


def kernel(x_prompt, x_sample, c_prompt, c_sample, cache_sb_k, cache_sb_v, cache_fox_k, cache_fox_v, cache_fox_logf, w_ada, b_ada, g_mix, w_in, b_f, g_sb_out, g_fox_out, w_o, g_ffn, w_gate, w_up, w_down, g_final):
    depth = w_ada.shape[0]
    bp, tp, d = x_prompt.shape
    bs, ts, _ = x_sample.shape
    past, n_sb, hd = cache_sb_k.shape[2:]
    n_fx = cache_fox_k.shape[3]
    assert hd == HEAD_DIM and n_sb % 2 == 0 and n_fx % 2 == 0
    assert tp % ROW_BLOCK == 0 and tp % ATT_BLOCK == 0 and ROW_BLOCK % ts == 0 and bs % (ROW_BLOCK // ts) == 0
    assert past % ATT_BLOCK == 0
    sb_w, fx_w = n_sb * hd, n_fx * hd
    qkv_cols = 3 * sb_w + 3 * fx_w
    att = min(FOX_BLOCK, tp)
    past_pad = -(-(past + ts) // (3 * LANES)) * (3 * LANES)
    row = lambda a: a.reshape(1, -1)

    xp, xs = x_prompt, x_sample
    outs = [[] for _ in range(10)]
    for l in range(depth):
        w16 = w_in[l][:, :qkv_cols].astype(BF16)
        in_cols = w_in.shape[2]
        w16t = jnp.pad(jnp.swapaxes(w_in[l], 0, 1), ((0, -in_cols % BF16_ROWS), (0, 0))).astype(BF16)
        wf16 = _pad_lanes(w_in[l][:, qkv_cols:], LANES).astype(BF16)
        bfp = _pad_lanes(row(b_f[l]), LANES)
        bfc = b_f[l].reshape(-1, 1)
        wo16, wg16 = w_o[l].astype(BF16), w_gate[l].astype(BF16)
        wu16, wd16 = w_up[l].astype(BF16), w_down[l].astype(BF16)

        c_all = jnp.concatenate([c_prompt, c_sample], axis=0)
        ada = _ada(c_all, w_ada[l], row(b_ada[l])).reshape(bp + bs, 6, d)
        ada_p, ada_s = ada[:bp], ada[bp:]

        (q_sb, k_sb, v_sb, k_sb16, v_sb16, q_fx, k_fx, v_fx, k_fx16, v_fx16, lf) = _pre(
            xp, ada_p, row(g_mix[l]), w16, w16t, wf16, bfp, bfc, sb_w, fx_w, n_fx, True)
        o_sb = _sb_prompt(q_sb, k_sb16, v_sb16)
        f_cum = _cumsum_rows(lf.reshape(bp * n_fx, tp), ROW_BLOCK)
        fq = jnp.swapaxes(f_cum.reshape(bp, n_fx, tp), 1, 2)
        fk = jnp.swapaxes(f_cum.reshape(bp, n_fx // 2, 2, tp // att, att), 2, 3)
        o_fx = _fox_prompt(q_fx, k_fx16, v_fx16, fq, fk)
        xp = _post(xp, o_sb, o_fx, ada_p, row(g_sb_out[l]), row(g_fox_out[l]), wo16, row(g_ffn[l]),
                   wg16, wu16, wd16, row(g_final), l + 1 == depth)
        for dst, a in zip(outs[0:4], (k_sb, v_sb, k_fx, v_fx)):
            dst.append(jnp.swapaxes(a, 1, 2).reshape(bp, tp, -1, hd))
        outs[4].append(jnp.swapaxes(lf, 1, 2))

        (q_sb, k_sb, v_sb, k_sb16, v_sb16, q_fx, k_fx, v_fx, k_fx16, v_fx16, lf) = _pre(
            xs, ada_s, row(g_mix[l]), w16, w16t, wf16, bfp, bfc, sb_w, fx_w, n_fx, False)
        time_minor = lambda a: jnp.swapaxes(a.reshape(a.shape[0], a.shape[1], -1), 1, 2)
        o_sb = _sb_sample(q_sb, time_minor(k_sb16), time_minor(v_sb16),
                          time_minor(cache_sb_k[l]), time_minor(cache_sb_v[l]))
        lf_all = jnp.concatenate([cache_fox_logf[l].astype(F32), lf], axis=1)
        lf_rows = _pad_lanes(jnp.swapaxes(lf_all, 1, 2).reshape(bs * n_fx, past + ts), past_pad)
        f_all = _cumsum_rows(lf_rows, past_pad // 3)
        fq = jnp.swapaxes(f_all.reshape(bs, n_fx, past_pad)[:, :, past:past + ts], 1, 2)
        fk = f_all.reshape(bs, n_fx // 2, 2, past_pad)
        o_fx = _fox_sample(q_fx, time_minor(k_fx16), time_minor(v_fx16),
                           time_minor(cache_fox_k[l]), time_minor(cache_fox_v[l]), fq, fk)
        xs = _post(xs, o_sb, o_fx, ada_s, row(g_sb_out[l]), row(g_fox_out[l]), wo16, row(g_ffn[l]),
                   wg16, wu16, wd16, row(g_final), l + 1 == depth)
        for dst, a in zip(outs[5:9], (k_sb, v_sb, k_fx, v_fx)):
            dst.append(a.reshape(bs, ts, -1, hd))
        outs[9].append(lf)

    return (xp, xs) + tuple(jnp.stack(o) for o in outs)
```

```python
import functools

import jax
import jax.numpy as jnp
from jax import lax
from jax.experimental import pallas as pl
from jax.experimental.pallas import tpu as pltpu

F32 = jnp.float32
BF16 = jnp.bfloat16
EPS = 1e-6
HEAD_DIM = 64
LANES = 128
BF16_ROWS = 16
ROW_BLOCK = 512
FF_CHUNK = 512
ATT_BLOCK = 256
SB_BLOCKS_PER_STEP = 4
FOX_BLOCKS_PER_STEP = 2
FOX_BLOCK = 512
NEG_BIG = -1e30
EXP_ZERO = -105.0
NORM_CHUNK = 2048
NORM_SLACK = 1.0 + 2.0 ** -8
VMEM_LIMIT = 56 * 1024 * 1024


def _params(*semantics):
    return pltpu.CompilerParams(dimension_semantics=semantics, vmem_limit_bytes=VMEM_LIMIT)


def _dot(a, b):
    return jnp.dot(a, b, preferred_element_type=F32)


def _dot_nt(a, b):
    return lax.dot_general(a, b, (((1,), (1,)), ((), ())), preferred_element_type=F32)


def _rms(x, g):
    return x * lax.rsqrt(jnp.mean(x * x, axis=-1, keepdims=True) + EPS) * g


def _split_bf16(x, parts):
    out = []
    for _ in range(parts - 1):
        p = x.astype(BF16)
        out.append(p)
        x = x - p.astype(F32)
    out.append(x.astype(BF16))
    return out


def _dot_split(x, m, parts):
    acc = None
    for p in _split_bf16(x, parts):
        d = _dot(p, m)
        acc = d if acc is None else acc + d
    return acc


def _keep_head(x, first, axis=1, fill=0.0):
    pos = lax.broadcasted_iota(jnp.int32, x.shape, axis)
    keep = (pos < HEAD_DIM) if first else (pos >= HEAD_DIM)
    return jnp.where(keep, x, jnp.full_like(x, fill))


def _per_head(col0, col1, shape):
    lane = lax.broadcasted_iota(jnp.int32, shape, 1)
    return jnp.where(lane < HEAD_DIM, col0, col1)


def _lanes_to(x, n):
    if n <= LANES:
        return x[:, :n]
    return jnp.concatenate([x] * (n // LANES), axis=1)


def _ada_kernel(c_ref, w_ref, b_ref, o_ref):
    c = c_ref[...]
    s = (c * (1.0 / (1.0 + jnp.exp(-c)))).astype(BF16)
    o_ref[...] = _dot(s, w_ref[...].astype(BF16)) + b_ref[...]


def _ada(c, w, b):
    rows, d = c.shape
    n = w.shape[1]
    tn = 1024
    return pl.pallas_call(
        _ada_kernel,
        grid=(n // tn,),
        in_specs=[pl.BlockSpec((rows, d), lambda j: (0, 0)),
                  pl.BlockSpec((d, tn), lambda j: (0, j)),
                  pl.BlockSpec((1, tn), lambda j: (0, j))],
        out_specs=pl.BlockSpec((rows, tn), lambda j: (0, j)),
        out_shape=jax.ShapeDtypeStruct((rows, n), F32),
        compiler_params=_params("arbitrary"),
        name="ada",
    )(c, w, b)


def _log_sigmoid(u):
    return jnp.minimum(u, 0.0) - jnp.log(1.0 + jnp.exp(-jnp.abs(u)))


def _pre_kernel(x_ref, ada_ref, g_ref, w_ref, wt_ref, wf_ref, bf_ref, bfc_ref,
                qsb_ref, ksb_ref, vsb_ref, ksb16_ref, vsb16_ref,
                qfx_ref, kfx_ref, vfx_ref, kfx16_ref, vfx16_ref, lf_ref, *, sb_w, fx_w, n_f, time_minor):
    x = x_ref[...]
    nb, tt, d = x.shape
    ada = ada_ref[...]
    h = _rms(x, g_ref[...]) * (1.0 + ada[:, 1:2, :]) + ada[:, 0:1, :]
    h = h.reshape(nb * tt, d).astype(BF16)
    q_scale = HEAD_DIM ** -0.5

    def proj(lo, width):
        return _dot(h, w_ref[:, lo:lo + width]).reshape(nb, tt, width)

    base = 3 * sb_w
    qsb_ref[...] = (proj(0, sb_w) * q_scale).astype(BF16)
    qfx_ref[...] = (proj(base, fx_w) * q_scale).astype(BF16)
    if time_minor:
        sb = _dot_nt(wt_ref[sb_w:base, :], h)
        fx = _dot_nt(wt_ref[base + fx_w:, :], h)
        parts = (sb[:sb_w], sb[sb_w:], fx[:fx_w], fx[fx_w:2 * fx_w])
        lf_ref[...] = _log_sigmoid(fx[2 * fx_w:2 * fx_w + n_f] + bfc_ref[...])[None]
    else:
        parts = (proj(sb_w, sb_w), proj(2 * sb_w, sb_w), proj(base + fx_w, fx_w), proj(base + 2 * fx_w, fx_w))
        lf = _log_sigmoid(_dot(h, wf_ref[...]) + bf_ref[...])
        lf_ref[...] = lf[:, :n_f].reshape(nb, tt, n_f)
    for r, out, out16 in zip(parts, (ksb_ref, vsb_ref, kfx_ref, vfx_ref),
                             (ksb16_ref, vsb16_ref, kfx16_ref, vfx16_ref)):
        r = r[None] if time_minor else r
        out[...] = r
        out16[...] = r.astype(BF16)


def _pre(x, ada, g, w16, w16t, wf16, bfp, bfc, sb_w, fx_w, n_f, time_minor):
    b, t, d = x.shape
    tt = min(t, ROW_BLOCK)
    nb = max(1, ROW_BLOCK // tt)
    assert nb == 1 or not time_minor
    grid = (b // nb, t // tt)
    tok = lambda w: pl.BlockSpec((nb, tt, w), lambda i, j: (i, j, 0))
    const = lambda a: pl.BlockSpec(a.shape, lambda i, j: (0, 0))
    shape = lambda w, dt: jax.ShapeDtypeStruct((b, t, w), dt)
    if time_minor:
        kv = lambda w: pl.BlockSpec((1, w, tt), lambda i, j: (i, 0, j))
        kv_shape = lambda w, dt: jax.ShapeDtypeStruct((b, w, t), dt)
    else:
        kv, kv_shape = tok, shape
    return pl.pallas_call(
        functools.partial(_pre_kernel, sb_w=sb_w, fx_w=fx_w, n_f=n_f, time_minor=time_minor),
        grid=grid,
        in_specs=[tok(d), pl.BlockSpec((nb, 6, d), lambda i, j: (i, 0, 0)),
                  const(g), const(w16), const(w16t), const(wf16), const(bfp), const(bfc)],
        out_specs=[tok(sb_w)] + [kv(sb_w)] * 4 + [tok(fx_w)] + [kv(fx_w)] * 4 + [kv(n_f)],
        out_shape=[shape(sb_w, BF16), kv_shape(sb_w, F32), kv_shape(sb_w, F32), kv_shape(sb_w, BF16),
                   kv_shape(sb_w, BF16),
                   shape(fx_w, BF16), kv_shape(fx_w, F32), kv_shape(fx_w, F32), kv_shape(fx_w, BF16),
                   kv_shape(fx_w, BF16),
                   kv_shape(n_f, F32)],
        compiler_params=_params("arbitrary", "arbitrary"),
        name="pre_mixer",
    )(x, ada, g, w16, w16t, wf16, bfp, bfc)


def _cumsum_kernel(x_ref, o_ref, carry_ref):
    @pl.when(pl.program_id(0) == 0)
    def _():
        carry_ref[...] = jnp.zeros_like(carry_ref)

    x = x_ref[...]
    tc = x.shape[1]
    r = lax.broadcasted_iota(jnp.int32, (tc, tc), 0)
    c = lax.broadcasted_iota(jnp.int32, (tc, tc), 1)
    upper = jnp.where(r <= c, 1.0, 0.0).astype(BF16)
    carry = carry_ref[...]
    o_ref[...] = _dot_split(x, upper, 3) + _lanes_to(carry, tc)
    carry_ref[...] = carry + _dot_split(x, jnp.ones((tc, LANES), BF16), 3)


def _cumsum_rows(x, tc):
    rows, t = x.shape
    return pl.pallas_call(
        _cumsum_kernel,
        grid=(t // tc,),
        in_specs=[pl.BlockSpec((rows, tc), lambda j: (0, j))],
        out_specs=pl.BlockSpec((rows, tc), lambda j: (0, j)),
        out_shape=jax.ShapeDtypeStruct((rows, t), F32),
        scratch_shapes=[pltpu.VMEM((rows, LANES), F32)],
        compiler_params=_params("arbitrary"),
        name="cumsum_time",
    )(x)


def _tri_strict(n):
    r = lax.broadcasted_iota(jnp.int32, (n, n), 0)
    c = lax.broadcasted_iota(jnp.int32, (n, n), 1)
    return jnp.where(r > c, 1.0, 0.0).astype(BF16)


def _causal(tq, tk, strict):
    r = lax.broadcasted_iota(jnp.int32, (tq, tk), 0)
    c = lax.broadcasted_iota(jnp.int32, (tq, tk), 1)
    return (c < r) if strict else (c <= r)


def _sb_tiles(chains, mask):
    tk = chains[0][1].shape[1]
    tri = _tri_strict(tk)
    wide = tk % LANES == 0
    if wide:
        tri2 = jnp.concatenate([tri, tri], axis=0)
    zs = [_dot(q, k) for q, k, _, _ in chains]
    staged = []
    for z in zs:
        log_beta = jnp.minimum(z, 0.0) - jnp.log(1.0 + jnp.exp(-jnp.abs(z)))
        log_keep = log_beta - z
        if mask is not None:
            log_keep = jnp.where(mask, log_keep, 0.0)
        hi, lo = _split_bf16(log_keep, 2)
        staged.append((log_beta, log_keep[:, 0:1], hi, lo))
    if wide:
        betweens = [_dot(jnp.concatenate([hi, lo], axis=1), tri2) for _, _, hi, lo in staged]
    else:
        betweens = [_dot(hi, tri) + _dot(lo, tri) for _, _, hi, lo in staged]
    weights = []
    for (log_beta, _, _, _), between, (_, _, _, carry) in zip(staged, betweens, chains):
        a = jnp.exp(log_beta + between + carry)
        if mask is not None:
            a = jnp.where(mask, a, 0.0)
        weights.append(a.astype(BF16))
    pvs = [_dot_nt(a, v) for a, (_, _, v, _) in zip(weights, chains)]
    return [(pv, carry + (between[:, 0:1] + first))
            for pv, between, (_, first, _, _), (_, _, _, carry) in zip(pvs, betweens, staged, chains)]


def _sb_start(rows, groups):
    col = jnp.zeros((rows, 1), F32)
    return (col, col, jnp.zeros((rows, LANES), F32)) * groups


def _sb_visit(qs, kvs, state, mask):
    chains = []
    for g, ((q0, q1), (k, v)) in enumerate(zip(qs, kvs)):
        chains.append((q0, k, _keep_head(v, True, 0), state[3 * g]))
        chains.append((q1, k, _keep_head(v, False, 0), state[3 * g + 1]))
    res = _sb_tiles(chains, mask)
    out = ()
    for g in range(len(qs)):
        (pv0, c0), (pv1, c1) = res[2 * g], res[2 * g + 1]
        out += (c0, c1, state[3 * g + 2] + pv0 + pv1)
    return out


def _sb_sweep(visit, lefts, state):
    groups = len(lefts)

    def live(s):
        alive = jnp.bool_(False)
        for g in range(groups):
            more = jnp.max(jnp.maximum(s[1 + 3 * g], s[2 + 3 * g])) > EXP_ZERO
            alive = jnp.logical_or(alive, jnp.logical_and(s[0] < lefts[g], more))
        return alive

    def step(s):
        i, old = s[0], s[1:]
        new = visit([jnp.maximum(lefts[g] - 1 - i, 0) for g in range(groups)], old, None)
        out = (i + 1,)
        for g in range(groups):
            inside = i < lefts[g]
            out += tuple(jnp.where(inside, n, o) for n, o in zip(new[3 * g:3 * g + 3], old[3 * g:3 * g + 3]))
        return out

    return lax.while_loop(live, step, (jnp.int32(0),) + tuple(state))[1:]


def _head_column(f, h):
    lane = lax.broadcasted_iota(jnp.int32, f.shape, 1)
    return jnp.sum(jnp.where(lane == h, f, 0.0), axis=1, keepdims=True)


def _sb_prompt_kernel(q_ref, k_ref, v_ref, o_ref, *, tile, blocks):
    first = pl.program_id(2) * blocks
    qs = []
    for j in range(blocks):
        q = q_ref[0, j * tile:(j + 1) * tile, :]
        qs.append((_keep_head(q, True), _keep_head(q, False)))

    def visit(kbs, state, mask):
        kvs = []
        for j in range(blocks):
            start = pl.multiple_of(kbs[j] * tile, tile)
            kvs.append((k_ref[0, :, pl.ds(start, tile)], v_ref[0, :, pl.ds(start, tile)]))
        return _sb_visit(qs, kvs, state, mask)

    lefts = [first + j for j in range(blocks)]
    state = visit(lefts, _sb_start(tile, blocks), _causal(tile, tile, True))
    state = _sb_sweep(visit, lefts, state)
    for j in range(blocks):
        o_ref[0, j * tile:(j + 1) * tile, :] = state[2 + 3 * j]


def _head_norms_sq(x16):
    x = x16.astype(F32)
    r = lax.broadcasted_iota(jnp.int32, (LANES, LANES), 0)
    c = lax.broadcasted_iota(jnp.int32, (LANES, LANES), 1)
    same_head = jnp.where((r < HEAD_DIM) == (c < HEAD_DIM), 1.0, 0.0).astype(BF16)
    return _dot_split(x * x, same_head, 2)


def _fox_prompt_kernel(q_ref, k_ref, v_ref, fq_ref, fk_ref, o_ref, kmax_ref, *, tile, blocks):
    hp = pl.program_id(1)
    first = pl.program_id(2) * blocks
    shape = (tile, LANES)
    half_tile = tile // 2
    causal = _causal(tile, tile, False)
    zero = jnp.zeros(shape, F32)
    every = range(blocks)

    @pl.when(first == 0)
    def _():
        width = min(NORM_CHUNK, k_ref.shape[2])

        def chunk(i, best):
            kc = k_ref[0, :, pl.ds(pl.multiple_of(i * width, width), width)].astype(F32)
            sq = kc * kc
            return (jnp.maximum(best[0], jnp.sum(sq[:HEAD_DIM], axis=0, keepdims=True)),
                    jnp.maximum(best[1], jnp.sum(sq[HEAD_DIM:], axis=0, keepdims=True)))

        none = jnp.zeros((1, width), F32)
        best = lax.fori_loop(0, k_ref.shape[2] // width, chunk, (none, none))
        kmax_ref[...] = _per_head(jnp.max(best[0], axis=1, keepdims=True),
                                  jnp.max(best[1], axis=1, keepdims=True), (1, LANES))

    kmax = jnp.sqrt(kmax_ref[...]) * NORM_SLACK
    qis = [first + j for j in every]
    qs, fqs, reach = [], [], []
    for j in every:
        rows = slice(j * tile, (j + 1) * tile)
        q = q_ref[0, rows, :]
        qs.append((_keep_head(q, True), _keep_head(q, False)))
        fq = fq_ref[0, rows, :]
        fqs.append((_head_column(fq, 2 * hp), _head_column(fq, 2 * hp + 1)))
        reach.append(jnp.sqrt(_head_norms_sq(q)) * kmax + _per_head(fqs[j][0], fqs[j][1], shape))

    def tile_left(j, i):
        return jnp.maximum(qis[j] - 1 - i, 0)

    def key_cols(kb):
        return pl.ds(pl.multiple_of(kb * tile, tile), tile)

    def last_fk(kb):
        fk = fk_ref[0, 0, kb]
        return _per_head(fk[0:1, tile - 1:tile], fk[1:2, tile - 1:tile], (1, LANES))

    def scores(j, kb):
        k = k_ref[0, :, key_cols(kb)]
        fk = fk_ref[0, 0, kb]
        return _dot(qs[j][0], k) - fk[0:1, :], _dot(qs[j][1], k) - fk[1:2, :]

    def diagonal_scores(j):
        k = k_ref[0, :, key_cols(qis[j])]
        fk = fk_ref[0, 0, qis[j]]
        raw = [(_dot(q, k[:, :half_tile]) - fk[r:r + 1, :half_tile],
                _dot(q[half_tile:], k[:, half_tile:]) - fk[r:r + 1, half_tile:]) for r, q in enumerate(qs[j])]
        unseen = jnp.full((half_tile, half_tile), NEG_BIG, F32)
        return [(jnp.where(causal[:, :half_tile], left, NEG_BIG),
                 jnp.concatenate([unseen, jnp.where(causal[half_tile:, half_tile:], right, NEG_BIG)], axis=0))
                for left, right in raw]

    def best_gap(j, kb, m0, m1):
        return jnp.max(reach[j] - _per_head(m0, m1, shape) - last_fk(kb))

    def max_live(s):
        alive = jnp.bool_(False)
        for j in every:
            alive = jnp.logical_or(alive, jnp.logical_and(s[0] < qis[j], s[1 + j] > 0.0))
        return alive

    def max_step(s):
        i, ms = s[0], s[1 + blocks:]
        us = [scores(j, tile_left(j, i)) for j in every]
        gaps, new = (), ()
        for j in every:
            inside = i < qis[j]
            both = [jnp.where(inside, jnp.maximum(ms[2 * j + r], jnp.max(us[j][r], axis=1, keepdims=True) + fqs[j][r]),
                              ms[2 * j + r]) for r in range(2)]
            gaps += (best_gap(j, tile_left(j, i + 1), both[0], both[1]),)
            new += tuple(both)
        return (i + 1,) + gaps + new

    diagonal = [diagonal_scores(j) for j in every]
    ms = ()
    for j in every:
        ms += tuple(jnp.maximum(jnp.max(left, axis=1, keepdims=True), jnp.max(right, axis=1, keepdims=True)) + fq
                    for (left, right), fq in zip(diagonal[j], fqs[j]))
    gaps = tuple(best_gap(j, tile_left(j, 0), ms[2 * j], ms[2 * j + 1]) for j in every)
    ms = lax.while_loop(max_live, max_step, (jnp.int32(0),) + gaps + ms)[1 + blocks:]

    cs = [(ms[2 * j] - fqs[j][0], ms[2 * j + 1] - fqs[j][1]) for j in every]
    gap_rows = [jnp.max(reach[j] - _per_head(ms[2 * j], ms[2 * j + 1], shape), axis=0, keepdims=True) for j in every]

    def head_gaps(j, kb):
        gap = gap_rows[j] - last_fk(kb)
        head0 = lax.broadcasted_iota(jnp.int32, gap.shape, 1) < HEAD_DIM
        return jnp.max(jnp.where(head0, gap, NEG_BIG)), jnp.max(jnp.where(head0, NEG_BIG, gap))

    def with_ones(v, head1):
        ones_rows = (lax.broadcasted_iota(jnp.int32, v.shape, 0) < HEAD_DIM) == head1
        return jnp.where(ones_rows, jnp.ones_like(v), v)

    blank = jnp.zeros((half_tile, LANES), F32)
    state = ()
    for j in every:
        v = v_ref[0, :, key_cols(qis[j])]
        weights = [(jnp.exp(left - c).astype(BF16), jnp.exp(right - c).astype(BF16))
                   for (left, right), c in zip(diagonal[j], cs[j])]
        accs = [_dot_nt(left, vh[:, :half_tile])
                + jnp.concatenate([blank, _dot_nt(right[half_tile:], vh[:, half_tile:])], axis=0)
                for (left, right), vh in zip(weights, (with_ones(v, False), with_ones(v, True)))]
        state += (jnp.int32(0),) + head_gaps(j, tile_left(j, 0)) + tuple(accs)

    def pair_active(s, j):
        done, gap0, gap1 = s[1 + 5 * j:4 + 5 * j]
        return jnp.logical_and(s[0] < qis[j], jnp.minimum(gap0, gap1) > EXP_ZERO)

    def pair_live(s):
        alive = jnp.bool_(False)
        for j in every:
            alive = jnp.logical_or(alive, pair_active(s, j))
        return alive

    def pair_step(s):
        i = s[0]
        kbs = [tile_left(j, i) for j in every]
        us = [scores(j, kbs[j]) for j in every]
        ps = [[jnp.exp(u - c).astype(BF16) for u, c in zip(us[j], cs[j])] for j in every]
        out = (i + 1,)
        for j in every:
            done, gap0, gap1, acc0, acc1 = s[1 + 5 * j:6 + 5 * j]
            active = pair_active(s, j)
            v = v_ref[0, :, key_cols(kbs[j])]
            new0, new1 = head_gaps(j, tile_left(j, i + 1))
            out += (jnp.where(active, i + 1, done), jnp.where(active, new0, gap0), jnp.where(active, new1, gap1),
                    acc0 + jnp.where(active, _dot_nt(ps[j][0], with_ones(v, False)), 0.0),
                    acc1 + jnp.where(active, _dot_nt(ps[j][1], with_ones(v, True)), 0.0))
        return out

    state = lax.while_loop(pair_live, pair_step, (jnp.int32(0),) + state)[1:]

    seconds = [state[5 * j + 2] > state[5 * j + 1] for j in every]
    starts = [state[5 * j] for j in every]
    q_lone = [jnp.where(seconds[j], qs[j][1], qs[j][0]) for j in every]
    c_lone = [jnp.where(seconds[j], cs[j][1], cs[j][0]) for j in every]

    def lone_active(s, j):
        return jnp.logical_and(starts[j] + s[0] < qis[j], s[1 + 2 * j] > EXP_ZERO)

    def lone_live(s):
        alive = jnp.bool_(False)
        for j in every:
            alive = jnp.logical_or(alive, lone_active(s, j))
        return alive

    def lone_step(s):
        t = s[0]
        kbs = [tile_left(j, starts[j] + t) for j in every]
        us = []
        for j in every:
            k = k_ref[0, :, key_cols(kbs[j])]
            fk = fk_ref[0, 0, kbs[j]]
            fk_s = jnp.where(seconds[j], fk[1:2, :], fk[0:1, :])
            us.append([_dot(q_lone[j], k[:, lo:lo + half_tile]) - fk_s[:, lo:lo + half_tile] for lo in (0, half_tile)])
        ps = [[jnp.exp(u - c_lone[j]).astype(BF16) for u in us[j]] for j in every]
        out = (t + 1,)
        for j in every:
            gap, acc = s[1 + 2 * j:3 + 2 * j]
            active = lone_active(s, j)
            v_s = with_ones(v_ref[0, :, key_cols(kbs[j])], seconds[j])
            gaps = head_gaps(j, tile_left(j, starts[j] + t + 1))
            more = _dot_nt(ps[j][0], v_s[:, :half_tile]) + _dot_nt(ps[j][1], v_s[:, half_tile:])
            out += (jnp.where(active, jnp.where(seconds[j], gaps[1], gaps[0]), gap), acc + jnp.where(active, more, 0.0))
        return out

    lone = (jnp.int32(0),)
    for j in every:
        lone += (jnp.maximum(state[5 * j + 1], state[5 * j + 2]), zero)
    lone = lax.while_loop(lone_live, lone_step, lone)[1:]

    lane = lax.broadcasted_iota(jnp.int32, shape, 1)
    half = LANES // 2
    for j in every:
        acc0 = state[5 * j + 3] + jnp.where(seconds[j], 0.0, lone[2 * j + 1])
        acc1 = state[5 * j + 4] + jnp.where(seconds[j], lone[2 * j + 1], 0.0)
        o_ref[0, j * tile:(j + 1) * tile, :] = jnp.where(lane < HEAD_DIM, acc0 / pltpu.roll(acc0, half, 1),
                                                         acc1 / pltpu.roll(acc1, half, 1))


def _prompt_specs(t, tile):
    qspec = pl.BlockSpec((1, tile, LANES), lambda b, h, i: (b, i, h))
    kvspec = pl.BlockSpec((1, LANES, t), lambda b, h, i: (b, h, 0))
    return qspec, kvspec


def _sb_prompt(q, k, v):
    b, t, w = q.shape
    tile = ATT_BLOCK
    blocks = SB_BLOCKS_PER_STEP if t % (tile * SB_BLOCKS_PER_STEP) == 0 else 1
    qspec, kvspec = _prompt_specs(t, tile * blocks)
    return pl.pallas_call(
        functools.partial(_sb_prompt_kernel, tile=tile, blocks=blocks),
        grid=(b, w // LANES, t // (tile * blocks)),
        in_specs=[qspec, kvspec, kvspec],
        out_specs=qspec,
        out_shape=jax.ShapeDtypeStruct((b, t, w), F32),
        compiler_params=_params("arbitrary", "arbitrary", "arbitrary"),
        name="sb_prompt",
    )(q, k, v)


def _fox_prompt(q, k, v, fq, fk):
    b, t, w = q.shape
    tile = fk.shape[-1]
    blocks = FOX_BLOCKS_PER_STEP if t % (tile * FOX_BLOCKS_PER_STEP) == 0 else 1
    qspec, kvspec = _prompt_specs(t, tile * blocks)
    n_f = fq.shape[-1]
    return pl.pallas_call(
        functools.partial(_fox_prompt_kernel, tile=tile, blocks=blocks),
        grid=(b, w // LANES, t // (tile * blocks)),
        in_specs=[qspec, kvspec, kvspec,
                  pl.BlockSpec((1, tile * blocks, n_f), lambda b, h, i: (b, i, 0)),
                  pl.BlockSpec((1, 1, t // tile, 2, tile), lambda b, h, i: (b, h, 0, 0, 0))],
        out_specs=qspec,
        out_shape=jax.ShapeDtypeStruct((b, t, w), F32),
        scratch_shapes=[pltpu.VMEM((1, LANES), F32)],
        compiler_params=_params("arbitrary", "arbitrary", "arbitrary"),
        name="fox_prompt",
    )(q, k, v, fq, fk)


def _group_rows(g):
    return slice(g * LANES, (g + 1) * LANES)


def _sb_sample_kernel(q_ref, kn_ref, vn_ref, kc_ref, vc_ref, o_ref, *, tile):
    n, width = q_ref.shape[1:]
    groups = width // LANES
    past = kc_ref.shape[2]
    qs = []
    for g in range(groups):
        q = q_ref[0, :, _group_rows(g)]
        qs.append((_keep_head(q, True), _keep_head(q, False)))

    def visit(kbs, state, mask):
        kvs = []
        for g in range(groups):
            if mask is not None:
                kvs.append((kn_ref[0, _group_rows(g), :], vn_ref[0, _group_rows(g), :]))
            else:
                cols = pl.ds(pl.multiple_of(kbs[g] * tile, tile), tile)
                kvs.append((kc_ref[0, _group_rows(g), cols].astype(BF16),
                            vc_ref[0, _group_rows(g), cols].astype(BF16)))
        return _sb_visit(qs, kvs, state, mask)

    state = visit(None, _sb_start(n, groups), _causal(n, n, True))
    state = _sb_sweep(visit, [past // tile] * groups, state)
    for g in range(groups):
        o_ref[0, :, _group_rows(g)] = state[2 + 3 * g]


def _fox_sample_kernel(q_ref, kn_ref, vn_ref, kc_ref, vc_ref, fq_ref, fk_ref, o_ref):
    n, width = q_ref.shape[1:]
    groups = width // LANES
    past = kc_ref.shape[2]
    fq = fq_ref[0]
    causal = _causal(n, n, False)
    scored = []
    for g in range(groups):
        q = q_ref[0, :, _group_rows(g)]
        kn = kn_ref[0, _group_rows(g), :]
        kc = kc_ref[0, _group_rows(g), :].astype(BF16)
        for h in range(2):
            qh = _keep_head(q, h == 0)
            fk = fk_ref[0, g, h:h + 1, :]
            u_new = jnp.where(causal, _dot(qh, kn) - fk[:, past:past + n], NEG_BIG)
            u_old = _dot(qh, kc) - fk[:, 0:past]
            scored.append((u_new, u_old, fq[:, 2 * g + h:2 * g + h + 1]))
    weights = []
    for u_new, u_old, fq_col in scored:
        m = jnp.maximum(jnp.max(u_new, axis=1, keepdims=True), jnp.max(u_old, axis=1, keepdims=True)) + fq_col
        c = m - fq_col
        weights.append((jnp.exp(u_new - c).astype(BF16), jnp.exp(u_old - c).astype(BF16)))
    lane = lax.broadcasted_iota(jnp.int32, (n, LANES), 1)
    half = LANES // 2
    for g in range(groups):
        vn = vn_ref[0, _group_rows(g), :]
        vc = vc_ref[0, _group_rows(g), :].astype(BF16)
        accs = []
        for h in range(2):
            p_new, p_old = weights[2 * g + h]
            accs.append(_dot_nt(p_new, _keep_head(vn, h == 0, 0, 1.0))
                        + _dot_nt(p_old, _keep_head(vc, h == 0, 0, 1.0)))
        o_ref[0, :, _group_rows(g)] = jnp.where(lane < HEAD_DIM, accs[0] / pltpu.roll(accs[0], half, 1),
                                                accs[1] / pltpu.roll(accs[1], half, 1))


def _sample_specs(n, width, past):
    qspec = pl.BlockSpec((1, n, width), lambda b: (b, 0, 0))
    new = pl.BlockSpec((1, width, n), lambda b: (b, 0, 0))
    cache = pl.BlockSpec((1, width, past), lambda b: (b, 0, 0))
    return qspec, new, cache


def _sb_sample(q, kn, vn, kc, vc):
    b, n, w = q.shape
    qspec, new, cache = _sample_specs(n, w, kc.shape[2])
    return pl.pallas_call(
        functools.partial(_sb_sample_kernel, tile=ATT_BLOCK),
        grid=(b,),
        in_specs=[qspec, new, new, cache, cache],
        out_specs=qspec,
        out_shape=jax.ShapeDtypeStruct((b, n, w), F32),
        compiler_params=_params("arbitrary"),
        name="sb_sample",
    )(q, kn, vn, kc, vc)


def _fox_sample(q, kn, vn, kc, vc, fq, fk):
    b, n, w = q.shape
    qspec, new, cache = _sample_specs(n, w, kc.shape[2])
    return pl.pallas_call(
        _fox_sample_kernel,
        grid=(b,),
        in_specs=[qspec, new, new, cache, cache,
                  pl.BlockSpec((1, n, fq.shape[-1]), lambda b: (b, 0, 0)),
                  pl.BlockSpec((1,) + fk.shape[1:], lambda b: (b, 0, 0, 0))],
        out_specs=qspec,
        out_shape=jax.ShapeDtypeStruct((b, n, w), F32),
        compiler_params=_params("arbitrary"),
        name="fox_sample",
    )(q, kn, vn, kc, vc, fq, fk)


def _post_kernel(x_ref, osb_ref, ofx_ref, ada_ref, gsb_ref, gfx_ref, wo_ref, gffn_ref,
                 wg_ref, wu_ref, wd_ref, gfin_ref, y_ref, *, final_norm):
    nb, tt, d = x_ref.shape
    rows = nb * tt
    ada = ada_ref[...]
    o_sb = _rms(osb_ref[...], gsb_ref[...])
    o_fx = _rms(ofx_ref[...], gfx_ref[...])
    sb_w, fx_w = o_sb.shape[-1], o_fx.shape[-1]
    proj = (_dot(o_sb.reshape(rows, sb_w).astype(BF16), wo_ref[0:sb_w, :])
            + _dot(o_fx.reshape(rows, fx_w).astype(BF16), wo_ref[sb_w:sb_w + fx_w, :]))
    x2 = x_ref[...] + (1.0 + ada[:, 2:3, :]) * proj.reshape(nb, tt, d)
    h = _rms(x2, gffn_ref[...]) * (1.0 + ada[:, 4:5, :]) + ada[:, 3:4, :]
    h = h.reshape(rows, d).astype(BF16)
    d_ff = wg_ref.shape[1]
    f = None
    for lo in range(0, d_ff, FF_CHUNK):
        hi = min(lo + FF_CHUNK, d_ff)
        g = _dot(h, wg_ref[:, lo:hi])
        u = _dot(h, wu_ref[:, lo:hi])
        act = (g * (1.0 / (1.0 + jnp.exp(-g))) * u).astype(BF16)
        part = _dot(act, wd_ref[lo:hi, :])
        f = part if f is None else f + part
    x3 = x2 + (1.0 + ada[:, 5:6, :]) * f.reshape(nb, tt, d)
    y_ref[...] = _rms(x3, gfin_ref[...]) if final_norm else x3


def _post(x, o_sb, o_fx, ada, g_sb, g_fx, wo16, g_ffn, wg16, wu16, wd16, g_fin, final_norm):
    b, t, d = x.shape
    tt = min(t, ROW_BLOCK)
    nb = max(1, ROW_BLOCK // tt)
    tok = lambda w: pl.BlockSpec((nb, tt, w), lambda i, j: (i, j, 0))
    const = lambda a: pl.BlockSpec(a.shape, lambda i, j: (0, 0), pipeline_mode=pl.Buffered(1))
    return pl.pallas_call(
        functools.partial(_post_kernel, final_norm=final_norm),
        grid=(b // nb, t // tt),
        in_specs=[tok(d), tok(o_sb.shape[-1]), tok(o_fx.shape[-1]),
                  pl.BlockSpec((nb, 6, d), lambda i, j: (i, 0, 0)),
                  const(g_sb), const(g_fx), const(wo16), const(g_ffn),
                  const(wg16), const(wu16), const(wd16), const(g_fin)],
        out_specs=tok(d),
        out_shape=jax.ShapeDtypeStruct((b, t, d), F32),
        compiler_params=_params("arbitrary", "arbitrary"),
        name="post_mixer_ffn",
    )(x, o_sb, o_fx, ada, g_sb, g_fx, wo16, g_ffn, wg16, wu16, wd16, g_fin)


def _pad_lanes(a, n):
    return jnp.pad(a, ((0, 0), (0, n - a.shape[1])))


def kernel(x_prompt, x_sample, c_prompt, c_sample, cache_sb_k, cache_sb_v, cache_fox_k, cache_fox_v, cache_fox_logf, w_ada, b_ada, g_mix, w_in, b_f, g_sb_out, g_fox_out, w_o, g_ffn, w_gate, w_up, w_down, g_final):
    depth = w_ada.shape[0]
    bp, tp, d = x_prompt.shape
    bs, ts, _ = x_sample.shape
    past, n_sb, hd = cache_sb_k.shape[2:]
    n_fx = cache_fox_k.shape[3]
    assert hd == HEAD_DIM and n_sb % 2 == 0 and n_fx % 2 == 0
    assert tp % ROW_BLOCK == 0 and tp % ATT_BLOCK == 0 and ROW_BLOCK % ts == 0 and bs % (ROW_BLOCK // ts) == 0
    assert past % ATT_BLOCK == 0
    sb_w, fx_w = n_sb * hd, n_fx * hd
    qkv_cols = 3 * sb_w + 3 * fx_w
    att = min(FOX_BLOCK, tp)
    past_pad = -(-(past + ts) // (3 * LANES)) * (3 * LANES)
    row = lambda a: a.reshape(1, -1)

    xp, xs = x_prompt, x_sample
    outs = [[] for _ in range(10)]
    for l in range(depth):
        w16 = w_in[l][:, :qkv_cols].astype(BF16)
        in_cols = w_in.shape[2]
        w16t = jnp.pad(jnp.swapaxes(w_in[l], 0, 1), ((0, -in_cols % BF16_ROWS), (0, 0))).astype(BF16)
        wf16 = _pad_lanes(w_in[l][:, qkv_cols:], LANES).astype(BF16)
        bfp = _pad_lanes(row(b_f[l]), LANES)
        bfc = b_f[l].reshape(-1, 1)
        wo16, wg16 = w_o[l].astype(BF16), w_gate[l].astype(BF16)
        wu16, wd16 = w_up[l].astype(BF16), w_down[l].astype(BF16)

        c_all = jnp.concatenate([c_prompt, c_sample], axis=0)
        ada = _ada(c_all, w_ada[l], row(b_ada[l])).reshape(bp + bs, 6, d)
        ada_p, ada_s = ada[:bp], ada[bp:]

        (q_sb, k_sb, v_sb, k_sb16, v_sb16, q_fx, k_fx, v_fx, k_fx16, v_fx16, lf) = _pre(
            xp, ada_p, row(g_mix[l]), w16, w16t, wf16, bfp, bfc, sb_w, fx_w, n_fx, True)
        o_sb = _sb_prompt(q_sb, k_sb16, v_sb16)
        f_cum = _cumsum_rows(lf.reshape(bp * n_fx, tp), ROW_BLOCK)
        fq = jnp.swapaxes(f_cum.reshape(bp, n_fx, tp), 1, 2)
        fk = jnp.swapaxes(f_cum.reshape(bp, n_fx // 2, 2, tp // att, att), 2, 3)
        o_fx = _fox_prompt(q_fx, k_fx16, v_fx16, fq, fk)
        xp = _post(xp, o_sb, o_fx, ada_p, row(g_sb_out[l]), row(g_fox_out[l]), wo16, row(g_ffn[l]),
                   wg16, wu16, wd16, row(g_final), l + 1 == depth)
        for dst, a in zip(outs[0:4], (k_sb, v_sb, k_fx, v_fx)):
            dst.append(jnp.swapaxes(a, 1, 2).reshape(bp, tp, -1, hd))
        outs[4].append(jnp.swapaxes(lf, 1, 2))

        (q_sb, k_sb, v_sb, k_sb16, v_sb16, q_fx, k_fx, v_fx, k_fx16, v_fx16, lf) = _pre(
            xs, ada_s, row(g_mix[l]), w16, w16t, wf16, bfp, bfc, sb_w, fx_w, n_fx, False)
        time_minor = lambda a: jnp.swapaxes(a.reshape(a.shape[0], a.shape[1], -1), 1, 2)
        o_sb = _sb_sample(q_sb, time_minor(k_sb16), time_minor(v_sb16),
                          time_minor(cache_sb_k[l]), time_minor(cache_sb_v[l]))
        lf_all = jnp.concatenate([cache_fox_logf[l].astype(F32), lf], axis=1)
        lf_rows = _pad_lanes(jnp.swapaxes(lf_all, 1, 2).reshape(bs * n_fx, past + ts), past_pad)
        f_all = _cumsum_rows(lf_rows, past_pad // 3)
        fq = jnp.swapaxes(f_all.reshape(bs, n_fx, past_pad)[:, :, past:past + ts], 1, 2)
        fk = f_all.reshape(bs, n_fx // 2, 2, past_pad)
        o_fx = _fox_sample(q_fx, time_minor(k_fx16), time_minor(v_fx16),
                           time_minor(cache_fox_k[l]), time_minor(cache_fox_v[l]), fq, fk)
        xs = _post(xs, o_sb, o_fx, ada_s, row(g_sb_out[l]), row(g_fox_out[l]), wo16, row(g_ffn[l]),
                   wg16, wu16, wd16, row(g_final), l + 1 == depth)
        for dst, a in zip(outs[5:9], (k_sb, v_sb, k_fx, v_fx)):
            dst.append(a.reshape(bs, ts, -1, hd))
        outs[9].append(lf)

    return (xp, xs) + tuple(jnp.stack(o) for o in outs)
```

```python
import functools

import jax
import jax.numpy as jnp
from jax import lax
from jax.experimental import pallas as pl
from jax.experimental.pallas import tpu as pltpu

F32 = jnp.float32
BF16 = jnp.bfloat16
EPS = 1e-6
HEAD_DIM = 64
LANES = 128
BF16_ROWS = 16
ROW_BLOCK = 512
FF_CHUNK = 512
ATT_BLOCK = 256
SB_BLOCKS_PER_STEP = 8
FOX_BLOCKS_PER_STEP = 2
FOX_BLOCK = 512
NEG_BIG = -1e30
EXP_ZERO = -105.0
NORM_CHUNK = 2048
NORM_SLACK = 1.0 + 2.0 ** -8
VMEM_LIMIT = 56 * 1024 * 1024


def _params(*semantics):
    return pltpu.CompilerParams(dimension_semantics=semantics, vmem_limit_bytes=VMEM_LIMIT)


def _dot(a, b):
    return jnp.dot(a, b, preferred_element_type=F32)


def _dot_nt(a, b):
    return lax.dot_general(a, b, (((1,), (1,)), ((), ())), preferred_element_type=F32)


def _rms(x, g):
    return x * lax.rsqrt(jnp.mean(x * x, axis=-1, keepdims=True) + EPS) * g


def _split_bf16(x, parts):
    out = []
    for _ in range(parts - 1):
        p = x.astype(BF16)
        out.append(p)
        x = x - p.astype(F32)
    out.append(x.astype(BF16))
    return out


def _dot_split(x, m, parts):
    acc = None
    for p in _split_bf16(x, parts):
        d = _dot(p, m)
        acc = d if acc is None else acc + d
    return acc


def _keep_head(x, first, axis=1, fill=0.0):
    pos = lax.broadcasted_iota(jnp.int32, x.shape, axis)
    keep = (pos < HEAD_DIM) if first else (pos >= HEAD_DIM)
    return jnp.where(keep, x, jnp.full_like(x, fill))


def _per_head(col0, col1, shape):
    lane = lax.broadcasted_iota(jnp.int32, shape, 1)
    return jnp.where(lane < HEAD_DIM, col0, col1)


def _lanes_to(x, n):
    if n <= LANES:
        return x[:, :n]
    return jnp.concatenate([x] * (n // LANES), axis=1)


def _ada_kernel(c_ref, w_ref, b_ref, o_ref):
    c = c_ref[...]
    s = (c * (1.0 / (1.0 + jnp.exp(-c)))).astype(BF16)
    o_ref[...] = _dot(s, w_ref[...].astype(BF16)) + b_ref[...]


def _ada(c, w, b):
    rows, d = c.shape
    n = w.shape[1]
    tn = 1024
    return pl.pallas_call(
        _ada_kernel,
        grid=(n // tn,),
        in_specs=[pl.BlockSpec((rows, d), lambda j: (0, 0)),
                  pl.BlockSpec((d, tn), lambda j: (0, j)),
                  pl.BlockSpec((1, tn), lambda j: (0, j))],
        out_specs=pl.BlockSpec((rows, tn), lambda j: (0, j)),
        out_shape=jax.ShapeDtypeStruct((rows, n), F32),
        compiler_params=_params("arbitrary"),
        name="ada",
    )(c, w, b)


def _log_sigmoid(u):
    return jnp.minimum(u, 0.0) - jnp.log(1.0 + jnp.exp(-jnp.abs(u)))


def _pre_kernel(x_ref, ada_ref, g_ref, wt_ref, bf_ref, bfc_ref,
                qsb_ref, ksb_ref, vsb_ref, ksb16_ref, vsb16_ref,
                qfx_ref, kfx_ref, vfx_ref, kfx16_ref, vfx16_ref, lf_ref, *, sb_w, fx_w, n_f, time_minor):
    x = x_ref[...]
    nb, tt, d = x.shape
    ada = ada_ref[...]
    h = _rms(x, g_ref[...]) * (1.0 + ada[:, 1:2, :]) + ada[:, 0:1, :]
    h = h.reshape(nb * tt, d).astype(BF16)
    q_scale = HEAD_DIM ** -0.5

    def proj(lo, width):
        return _dot_nt(h, wt_ref[lo:lo + width, :]).reshape(nb, tt, width)

    base = 3 * sb_w
    qsb_ref[...] = (proj(0, sb_w) * q_scale).astype(BF16)
    qfx_ref[...] = (proj(base, fx_w) * q_scale).astype(BF16)
    if time_minor:
        sb = _dot_nt(wt_ref[sb_w:base, :], h)
        fx = _dot_nt(wt_ref[base + fx_w:, :], h)
        parts = (sb[:sb_w], sb[sb_w:], fx[:fx_w], fx[fx_w:2 * fx_w])
        lf_ref[...] = _log_sigmoid(fx[2 * fx_w:2 * fx_w + n_f] + bfc_ref[...])[None]
    else:
        parts = (proj(sb_w, sb_w), proj(2 * sb_w, sb_w), proj(base + fx_w, fx_w), proj(base + 2 * fx_w, fx_w))
        forget = _dot_nt(h, wt_ref[base + 3 * fx_w:, :])
        lf_ref[...] = _log_sigmoid(forget[:, :n_f] + bf_ref[...]).reshape(nb, tt, n_f)
    for r, out, out16 in zip(parts, (ksb_ref, vsb_ref, kfx_ref, vfx_ref),
                             (ksb16_ref, vsb16_ref, kfx16_ref, vfx16_ref)):
        r = r[None] if time_minor else r
        out[...] = r
        out16[...] = r.astype(BF16)


def _pre(x, ada, g, w16t, bfr, bfc, sb_w, fx_w, n_f, time_minor):
    b, t, d = x.shape
    tt = min(t, ROW_BLOCK)
    nb = max(1, ROW_BLOCK // tt)
    assert nb == 1 or not time_minor
    grid = (b // nb, t // tt)
    tok = lambda w: pl.BlockSpec((nb, tt, w), lambda i, j: (i, j, 0))
    const = lambda a: pl.BlockSpec(a.shape, lambda i, j: (0, 0))
    shape = lambda w, dt: jax.ShapeDtypeStruct((b, t, w), dt)
    if time_minor:
        kv = lambda w: pl.BlockSpec((1, w, tt), lambda i, j: (i, 0, j))
        kv_shape = lambda w, dt: jax.ShapeDtypeStruct((b, w, t), dt)
    else:
        kv, kv_shape = tok, shape
    return pl.pallas_call(
        functools.partial(_pre_kernel, sb_w=sb_w, fx_w=fx_w, n_f=n_f, time_minor=time_minor),
        grid=grid,
        in_specs=[tok(d), pl.BlockSpec((nb, 6, d), lambda i, j: (i, 0, 0)),
                  const(g), const(w16t), const(bfr), const(bfc)],
        out_specs=[tok(sb_w)] + [kv(sb_w)] * 4 + [tok(fx_w)] + [kv(fx_w)] * 4 + [kv(n_f)],
        out_shape=[shape(sb_w, BF16), kv_shape(sb_w, F32), kv_shape(sb_w, F32), kv_shape(sb_w, BF16),
                   kv_shape(sb_w, BF16),
                   shape(fx_w, BF16), kv_shape(fx_w, F32), kv_shape(fx_w, F32), kv_shape(fx_w, BF16),
                   kv_shape(fx_w, BF16),
                   kv_shape(n_f, F32)],
        compiler_params=_params("arbitrary", "arbitrary"),
        name="pre_mixer",
    )(x, ada, g, w16t, bfr, bfc)


def _cumsum_kernel(x_ref, o_ref, carry_ref):
    @pl.when(pl.program_id(0) == 0)
    def _():
        carry_ref[...] = jnp.zeros_like(carry_ref)

    x = x_ref[...]
    tc = x.shape[1]
    r = lax.broadcasted_iota(jnp.int32, (tc, tc), 0)
    c = lax.broadcasted_iota(jnp.int32, (tc, tc), 1)
    upper = jnp.where(r <= c, 1.0, 0.0).astype(BF16)
    carry = carry_ref[...]
    o_ref[...] = _dot_split(x, upper, 3) + _lanes_to(carry, tc)
    carry_ref[...] = carry + _dot_split(x, jnp.ones((tc, LANES), BF16), 3)


def _cumsum_rows(x, tc):
    rows, t = x.shape
    return pl.pallas_call(
        _cumsum_kernel,
        grid=(t // tc,),
        in_specs=[pl.BlockSpec((rows, tc), lambda j: (0, j))],
        out_specs=pl.BlockSpec((rows, tc), lambda j: (0, j)),
        out_shape=jax.ShapeDtypeStruct((rows, t), F32),
        scratch_shapes=[pltpu.VMEM((rows, LANES), F32)],
        compiler_params=_params("arbitrary"),
        name="cumsum_time",
    )(x)


def _tri_strict(n):
    r = lax.broadcasted_iota(jnp.int32, (n, n), 0)
    c = lax.broadcasted_iota(jnp.int32, (n, n), 1)
    return jnp.where(r > c, 1.0, 0.0).astype(BF16)


def _causal(tq, tk, strict):
    r = lax.broadcasted_iota(jnp.int32, (tq, tk), 0)
    c = lax.broadcasted_iota(jnp.int32, (tq, tk), 1)
    return (c < r) if strict else (c <= r)


def _sb_tiles(chains, mask):
    tk = chains[0][1].shape[1]
    tri = _tri_strict(tk)
    wide = tk % LANES == 0
    if wide:
        tri2 = jnp.concatenate([tri, tri], axis=0)
    zs = [_dot(q, k) for q, k, _, _ in chains]
    staged = []
    for z in zs:
        log_beta = jnp.minimum(z, 0.0) - jnp.log(1.0 + jnp.exp(-jnp.abs(z)))
        log_keep = log_beta - z
        if mask is not None:
            log_keep = jnp.where(mask, log_keep, 0.0)
        hi, lo = _split_bf16(log_keep, 2)
        staged.append((log_beta, log_keep[:, 0:1], hi, lo))
    if wide:
        betweens = [_dot(jnp.concatenate([hi, lo], axis=1), tri2) for _, _, hi, lo in staged]
    else:
        betweens = [_dot(hi, tri) + _dot(lo, tri) for _, _, hi, lo in staged]
    weights = []
    for (log_beta, _, _, _), between, (_, _, _, carry) in zip(staged, betweens, chains):
        a = jnp.exp(log_beta + between + carry)
        if mask is not None:
            a = jnp.where(mask, a, 0.0)
        weights.append(a.astype(BF16))
    pvs = [_dot_nt(a, v) for a, (_, _, v, _) in zip(weights, chains)]
    return [(pv, carry + (between[:, 0:1] + first))
            for pv, between, (_, first, _, _), (_, _, _, carry) in zip(pvs, betweens, staged, chains)]


def _sb_start(rows, groups):
    col = jnp.zeros((rows, 1), F32)
    return (col, col, jnp.zeros((rows, LANES), F32)) * groups


def _sb_visit(qs, kvs, state, mask):
    chains = []
    for g, ((q0, q1), (k, v)) in enumerate(zip(qs, kvs)):
        chains.append((q0, k, _keep_head(v, True, 0), state[3 * g]))
        chains.append((q1, k, _keep_head(v, False, 0), state[3 * g + 1]))
    res = _sb_tiles(chains, mask)
    out = ()
    for g in range(len(qs)):
        (pv0, c0), (pv1, c1) = res[2 * g], res[2 * g + 1]
        out += (c0, c1, state[3 * g + 2] + pv0 + pv1)
    return out


def _sb_sweep(visit, lefts, state):
    groups = len(lefts)

    def live(s):
        alive = jnp.bool_(False)
        for g in range(groups):
            more = jnp.max(jnp.maximum(s[1 + 3 * g], s[2 + 3 * g])) > EXP_ZERO
            alive = jnp.logical_or(alive, jnp.logical_and(s[0] < lefts[g], more))
        return alive

    def step(s):
        i, old = s[0], s[1:]
        new = visit([jnp.maximum(lefts[g] - 1 - i, 0) for g in range(groups)], old, None)
        out = (i + 1,)
        for g in range(groups):
            inside = i < lefts[g]
            out += tuple(jnp.where(inside, n, o) for n, o in zip(new[3 * g:3 * g + 3], old[3 * g:3 * g + 3]))
        return out

    return lax.while_loop(live, step, (jnp.int32(0),) + tuple(state))[1:]


def _head_column(f, h):
    lane = lax.broadcasted_iota(jnp.int32, f.shape, 1)
    return jnp.sum(jnp.where(lane == h, f, 0.0), axis=1, keepdims=True)


def _sb_prompt_kernel(q_ref, k_ref, v_ref, o_ref, *, tile, blocks):
    first = pl.program_id(2) * blocks
    qs = []
    for j in range(blocks):
        q = q_ref[0, j * tile:(j + 1) * tile, :]
        qs.append((_keep_head(q, True), _keep_head(q, False)))

    def visit(kbs, state, mask):
        kvs = []
        for j in range(blocks):
            start = pl.multiple_of(kbs[j] * tile, tile)
            kvs.append((k_ref[0, :, pl.ds(start, tile)], v_ref[0, :, pl.ds(start, tile)]))
        return _sb_visit(qs, kvs, state, mask)

    lefts = [first + j for j in range(blocks)]
    state = visit(lefts, _sb_start(tile, blocks), _causal(tile, tile, True))
    state = _sb_sweep(visit, lefts, state)
    for j in range(blocks):
        o_ref[0, j * tile:(j + 1) * tile, :] = state[2 + 3 * j]


def _head_norms_sq(x16):
    x = x16.astype(F32)
    r = lax.broadcasted_iota(jnp.int32, (LANES, LANES), 0)
    c = lax.broadcasted_iota(jnp.int32, (LANES, LANES), 1)
    same_head = jnp.where((r < HEAD_DIM) == (c < HEAD_DIM), 1.0, 0.0).astype(BF16)
    return _dot_split(x * x, same_head, 2)


def _fox_prompt_kernel(q_ref, k_ref, v_ref, fq_ref, fk_ref, o_ref, kmax_ref, *, tile, blocks):
    hp = pl.program_id(1)
    first = pl.program_id(2) * blocks
    shape = (tile, LANES)
    half_tile = tile // 2
    causal = _causal(tile, tile, False)
    zero = jnp.zeros(shape, F32)
    every = range(blocks)

    @pl.when(first == 0)
    def _():
        width = min(NORM_CHUNK, k_ref.shape[2])

        def chunk(i, best):
            kc = k_ref[0, :, pl.ds(pl.multiple_of(i * width, width), width)].astype(F32)
            sq = kc * kc
            return (jnp.maximum(best[0], jnp.sum(sq[:HEAD_DIM], axis=0, keepdims=True)),
                    jnp.maximum(best[1], jnp.sum(sq[HEAD_DIM:], axis=0, keepdims=True)))

        none = jnp.zeros((1, width), F32)
        best = lax.fori_loop(0, k_ref.shape[2] // width, chunk, (none, none))
        kmax_ref[...] = _per_head(jnp.max(best[0], axis=1, keepdims=True),
                                  jnp.max(best[1], axis=1, keepdims=True), (1, LANES))

    kmax = jnp.sqrt(kmax_ref[...]) * NORM_SLACK
    qis = [first + j for j in every]
    qs, fqs, reach = [], [], []
    for j in every:
        rows = slice(j * tile, (j + 1) * tile)
        q = q_ref[0, rows, :]
        qs.append((_keep_head(q, True), _keep_head(q, False)))
        fq = fq_ref[0, rows, :]
        fqs.append((_head_column(fq, 2 * hp), _head_column(fq, 2 * hp + 1)))
        reach.append(jnp.sqrt(_head_norms_sq(q)) * kmax + _per_head(fqs[j][0], fqs[j][1], shape))

    def tile_left(j, i):
        return jnp.maximum(qis[j] - 1 - i, 0)

    def key_cols(kb):
        return pl.ds(pl.multiple_of(kb * tile, tile), tile)

    def last_fk(kb):
        fk = fk_ref[0, 0, kb]
        return _per_head(fk[0:1, tile - 1:tile], fk[1:2, tile - 1:tile], (1, LANES))

    def scores(j, kb):
        k = k_ref[0, :, key_cols(kb)]
        fk = fk_ref[0, 0, kb]
        return _dot(qs[j][0], k) - fk[0:1, :], _dot(qs[j][1], k) - fk[1:2, :]

    def diagonal_scores(j):
        k = k_ref[0, :, key_cols(qis[j])]
        fk = fk_ref[0, 0, qis[j]]
        raw = [(_dot(q, k[:, :half_tile]) - fk[r:r + 1, :half_tile],
                _dot(q[half_tile:], k[:, half_tile:]) - fk[r:r + 1, half_tile:]) for r, q in enumerate(qs[j])]
        unseen = jnp.full((half_tile, half_tile), NEG_BIG, F32)
        return [(jnp.where(causal[:, :half_tile], left, NEG_BIG),
                 jnp.concatenate([unseen, jnp.where(causal[half_tile:, half_tile:], right, NEG_BIG)], axis=0))
                for left, right in raw]

    def best_gap(j, kb, m0, m1):
        return jnp.max(reach[j] - _per_head(m0, m1, shape) - last_fk(kb))

    def max_live(s):
        alive = jnp.bool_(False)
        for j in every:
            alive = jnp.logical_or(alive, jnp.logical_and(s[0] < qis[j], s[1 + j] > 0.0))
        return alive

    def max_step(s):
        i, ms = s[0], s[1 + blocks:]
        us = [scores(j, tile_left(j, i)) for j in every]
        gaps, new = (), ()
        for j in every:
            inside = i < qis[j]
            both = [jnp.where(inside, jnp.maximum(ms[2 * j + r], jnp.max(us[j][r], axis=1, keepdims=True) + fqs[j][r]),
                              ms[2 * j + r]) for r in range(2)]
            gaps += (best_gap(j, tile_left(j, i + 1), both[0], both[1]),)
            new += tuple(both)
        return (i + 1,) + gaps + new

    diagonal = [diagonal_scores(j) for j in every]
    ms = ()
    for j in every:
        ms += tuple(jnp.maximum(jnp.max(left, axis=1, keepdims=True), jnp.max(right, axis=1, keepdims=True)) + fq
                    for (left, right), fq in zip(diagonal[j], fqs[j]))
    gaps = tuple(best_gap(j, tile_left(j, 0), ms[2 * j], ms[2 * j + 1]) for j in every)
    ms = lax.while_loop(max_live, max_step, (jnp.int32(0),) + gaps + ms)[1 + blocks:]

    cs = [(ms[2 * j] - fqs[j][0], ms[2 * j + 1] - fqs[j][1]) for j in every]
    gap_rows = [jnp.max(reach[j] - _per_head(ms[2 * j], ms[2 * j + 1], shape), axis=0, keepdims=True) for j in every]

    def head_gaps(j, kb):
        gap = gap_rows[j] - last_fk(kb)
        head0 = lax.broadcasted_iota(jnp.int32, gap.shape, 1) < HEAD_DIM
        return jnp.max(jnp.where(head0, gap, NEG_BIG)), jnp.max(jnp.where(head0, NEG_BIG, gap))

    def with_ones(v, head1):
        ones_rows = (lax.broadcasted_iota(jnp.int32, v.shape, 0) < HEAD_DIM) == head1
        return jnp.where(ones_rows, jnp.ones_like(v), v)

    blank = jnp.zeros((half_tile, LANES), F32)
    state = ()
    for j in every:
        v = v_ref[0, :, key_cols(qis[j])]
        weights = [(jnp.exp(left - c).astype(BF16), jnp.exp(right - c).astype(BF16))
                   for (left, right), c in zip(diagonal[j], cs[j])]
        accs = [_dot_nt(left, vh[:, :half_tile])
                + jnp.concatenate([blank, _dot_nt(right[half_tile:], vh[:, half_tile:])], axis=0)
                for (left, right), vh in zip(weights, (with_ones(v, False), with_ones(v, True)))]
        state += (jnp.int32(0),) + head_gaps(j, tile_left(j, 0)) + tuple(accs)

    def pair_active(s, j):
        done, gap0, gap1 = s[1 + 5 * j:4 + 5 * j]
        return jnp.logical_and(s[0] < qis[j], jnp.minimum(gap0, gap1) > EXP_ZERO)

    def pair_live(s):
        alive = jnp.bool_(False)
        for j in every:
            alive = jnp.logical_or(alive, pair_active(s, j))
        return alive

    def pair_step(s):
        i = s[0]
        kbs = [tile_left(j, i) for j in every]
        us = [scores(j, kbs[j]) for j in every]
        ps = [[jnp.exp(u - c).astype(BF16) for u, c in zip(us[j], cs[j])] for j in every]
        out = (i + 1,)
        for j in every:
            done, gap0, gap1, acc0, acc1 = s[1 + 5 * j:6 + 5 * j]
            active = pair_active(s, j)
            v = v_ref[0, :, key_cols(kbs[j])]
            new0, new1 = head_gaps(j, tile_left(j, i + 1))
            out += (jnp.where(active, i + 1, done), jnp.where(active, new0, gap0), jnp.where(active, new1, gap1),
                    acc0 + jnp.where(active, _dot_nt(ps[j][0], with_ones(v, False)), 0.0),
                    acc1 + jnp.where(active, _dot_nt(ps[j][1], with_ones(v, True)), 0.0))
        return out

    state = lax.while_loop(pair_live, pair_step, (jnp.int32(0),) + state)[1:]

    seconds = [state[5 * j + 2] > state[5 * j + 1] for j in every]
    starts = [state[5 * j] for j in every]
    q_lone = [jnp.where(seconds[j], qs[j][1], qs[j][0]) for j in every]
    c_lone = [jnp.where(seconds[j], cs[j][1], cs[j][0]) for j in every]

    def lone_active(s, j):
        return jnp.logical_and(starts[j] + s[0] < qis[j], s[1 + 2 * j] > EXP_ZERO)

    def lone_live(s):
        alive = jnp.bool_(False)
        for j in every:
            alive = jnp.logical_or(alive, lone_active(s, j))
        return alive

    def lone_step(s):
        t = s[0]
        kbs = [tile_left(j, starts[j] + t) for j in every]
        us = []
        for j in every:
            k = k_ref[0, :, key_cols(kbs[j])]
            fk = fk_ref[0, 0, kbs[j]]
            fk_s = jnp.where(seconds[j], fk[1:2, :], fk[0:1, :])
            us.append([_dot(q_lone[j], k[:, lo:lo + half_tile]) - fk_s[:, lo:lo + half_tile] for lo in (0, half_tile)])
        ps = [[jnp.exp(u - c_lone[j]).astype(BF16) for u in us[j]] for j in every]
        out = (t + 1,)
        for j in every:
            gap, acc = s[1 + 2 * j:3 + 2 * j]
            active = lone_active(s, j)
            v_s = with_ones(v_ref[0, :, key_cols(kbs[j])], seconds[j])
            gaps = head_gaps(j, tile_left(j, starts[j] + t + 1))
            more = _dot_nt(ps[j][0], v_s[:, :half_tile]) + _dot_nt(ps[j][1], v_s[:, half_tile:])
            out += (jnp.where(active, jnp.where(seconds[j], gaps[1], gaps[0]), gap), acc + jnp.where(active, more, 0.0))
        return out

    lone = (jnp.int32(0),)
    for j in every:
        lone += (jnp.maximum(state[5 * j + 1], state[5 * j + 2]), zero)
    lone = lax.while_loop(lone_live, lone_step, lone)[1:]

    lane = lax.broadcasted_iota(jnp.int32, shape, 1)
    half = LANES // 2
    for j in every:
        acc0 = state[5 * j + 3] + jnp.where(seconds[j], 0.0, lone[2 * j + 1])
        acc1 = state[5 * j + 4] + jnp.where(seconds[j], lone[2 * j + 1], 0.0)
        o_ref[0, j * tile:(j + 1) * tile, :] = jnp.where(lane < HEAD_DIM, acc0 / pltpu.roll(acc0, half, 1),
                                                         acc1 / pltpu.roll(acc1, half, 1))


def _prompt_specs(t, tile):
    qspec = pl.BlockSpec((1, tile, LANES), lambda b, h, i: (b, i, h))
    kvspec = pl.BlockSpec((1, LANES, t), lambda b, h, i: (b, h, 0))
    return qspec, kvspec


def _sb_prompt(q, k, v):
    b, t, w = q.shape
    tile = ATT_BLOCK
    blocks = SB_BLOCKS_PER_STEP if t % (tile * SB_BLOCKS_PER_STEP) == 0 else 1
    qspec, kvspec = _prompt_specs(t, tile * blocks)
    return pl.pallas_call(
        functools.partial(_sb_prompt_kernel, tile=tile, blocks=blocks),
        grid=(b, w // LANES, t // (tile * blocks)),
        in_specs=[qspec, kvspec, kvspec],
        out_specs=qspec,
        out_shape=jax.ShapeDtypeStruct((b, t, w), F32),
        compiler_params=_params("arbitrary", "arbitrary", "arbitrary"),
        name="sb_prompt",
    )(q, k, v)


def _fox_prompt(q, k, v, fq, fk):
    b, t, w = q.shape
    tile = fk.shape[-1]
    blocks = FOX_BLOCKS_PER_STEP if t % (tile * FOX_BLOCKS_PER_STEP) == 0 else 1
    qspec, kvspec = _prompt_specs(t, tile * blocks)
    n_f = fq.shape[-1]
    return pl.pallas_call(
        functools.partial(_fox_prompt_kernel, tile=tile, blocks=blocks),
        grid=(b, w // LANES, t // (tile * blocks)),
        in_specs=[qspec, kvspec, kvspec,
                  pl.BlockSpec((1, tile * blocks, n_f), lambda b, h, i: (b, i, 0)),
                  pl.BlockSpec((1, 1, t // tile, 2, tile), lambda b, h, i: (b, h, 0, 0, 0))],
        out_specs=qspec,
        out_shape=jax.ShapeDtypeStruct((b, t, w), F32),
        scratch_shapes=[pltpu.VMEM((1, LANES), F32)],
        compiler_params=_params("arbitrary", "arbitrary", "arbitrary"),
        name="fox_prompt",
    )(q, k, v, fq, fk)


def _group_rows(g):
    return slice(g * LANES, (g + 1) * LANES)


def _sb_sample_kernel(q_ref, kn_ref, vn_ref, kc_ref, vc_ref, o_ref, *, tile):
    n, width = q_ref.shape[1:]
    groups = width // LANES
    past = kc_ref.shape[2]
    qs = []
    for g in range(groups):
        q = q_ref[0, :, _group_rows(g)]
        qs.append((_keep_head(q, True), _keep_head(q, False)))

    def visit(kbs, state, mask):
        kvs = []
        for g in range(groups):
            if mask is not None:
                kvs.append((kn_ref[0, _group_rows(g), :], vn_ref[0, _group_rows(g), :]))
            else:
                cols = pl.ds(pl.multiple_of(kbs[g] * tile, tile), tile)
                kvs.append((kc_ref[0, _group_rows(g), cols].astype(BF16),
                            vc_ref[0, _group_rows(g), cols].astype(BF16)))
        return _sb_visit(qs, kvs, state, mask)

    state = visit(None, _sb_start(n, groups), _causal(n, n, True))
    state = _sb_sweep(visit, [past // tile] * groups, state)
    for g in range(groups):
        o_ref[0, :, _group_rows(g)] = state[2 + 3 * g]


def _fox_sample_kernel(q_ref, kn_ref, vn_ref, kc_ref, vc_ref, fq_ref, fk_ref, o_ref):
    n, width = q_ref.shape[1:]
    groups = width // LANES
    past = kc_ref.shape[2]
    fq = fq_ref[0]
    causal = _causal(n, n, False)
    scored = []
    for g in range(groups):
        q = q_ref[0, :, _group_rows(g)]
        kn = kn_ref[0, _group_rows(g), :]
        kc = kc_ref[0, _group_rows(g), :].astype(BF16)
        for h in range(2):
            qh = _keep_head(q, h == 0)
            fk = fk_ref[0, g, h:h + 1, :]
            u_new = jnp.where(causal, _dot(qh, kn) - fk[:, past:past + n], NEG_BIG)
            u_old = _dot(qh, kc) - fk[:, 0:past]
            scored.append((u_new, u_old, fq[:, 2 * g + h:2 * g + h + 1]))
    weights = []
    for u_new, u_old, fq_col in scored:
        m = jnp.maximum(jnp.max(u_new, axis=1, keepdims=True), jnp.max(u_old, axis=1, keepdims=True)) + fq_col
        c = m - fq_col
        weights.append((jnp.exp(u_new - c).astype(BF16), jnp.exp(u_old - c).astype(BF16)))
    lane = lax.broadcasted_iota(jnp.int32, (n, LANES), 1)
    half = LANES // 2
    for g in range(groups):
        vn = vn_ref[0, _group_rows(g), :]
        vc = vc_ref[0, _group_rows(g), :].astype(BF16)
        accs = []
        for h in range(2):
            p_new, p_old = weights[2 * g + h]
            accs.append(_dot_nt(p_new, _keep_head(vn, h == 0, 0, 1.0))
                        + _dot_nt(p_old, _keep_head(vc, h == 0, 0, 1.0)))
        o_ref[0, :, _group_rows(g)] = jnp.where(lane < HEAD_DIM, accs[0] / pltpu.roll(accs[0], half, 1),
                                                accs[1] / pltpu.roll(accs[1], half, 1))


def _sample_specs(n, width, past):
    qspec = pl.BlockSpec((1, n, width), lambda b: (b, 0, 0))
    new = pl.BlockSpec((1, width, n), lambda b: (b, 0, 0))
    cache = pl.BlockSpec((1, width, past), lambda b: (b, 0, 0))
    return qspec, new, cache


def _sb_sample(q, kn, vn, kc, vc):
    b, n, w = q.shape
    qspec, new, cache = _sample_specs(n, w, kc.shape[2])
    return pl.pallas_call(
        functools.partial(_sb_sample_kernel, tile=ATT_BLOCK),
        grid=(b,),
        in_specs=[qspec, new, new, cache, cache],
        out_specs=qspec,
        out_shape=jax.ShapeDtypeStruct((b, n, w), F32),
        compiler_params=_params("arbitrary"),
        name="sb_sample",
    )(q, kn, vn, kc, vc)


def _fox_sample(q, kn, vn, kc, vc, fq, fk):
    b, n, w = q.shape
    qspec, new, cache = _sample_specs(n, w, kc.shape[2])
    return pl.pallas_call(
        _fox_sample_kernel,
        grid=(b,),
        in_specs=[qspec, new, new, cache, cache,
                  pl.BlockSpec((1, n, fq.shape[-1]), lambda b: (b, 0, 0)),
                  pl.BlockSpec((1,) + fk.shape[1:], lambda b: (b, 0, 0, 0))],
        out_specs=qspec,
        out_shape=jax.ShapeDtypeStruct((b, n, w), F32),
        compiler_params=_params("arbitrary"),
        name="fox_sample",
    )(q, kn, vn, kc, vc, fq, fk)


def _post_kernel(x_ref, osb_ref, ofx_ref, ada_ref, gsb_ref, gfx_ref, wo_ref, gffn_ref,
                 wg_ref, wu_ref, wd_ref, gfin_ref, y_ref, *, final_norm):
    nb, tt, d = x_ref.shape
    rows = nb * tt
    ada = ada_ref[...]
    o_sb = _rms(osb_ref[...], gsb_ref[...])
    o_fx = _rms(ofx_ref[...], gfx_ref[...])
    sb_w, fx_w = o_sb.shape[-1], o_fx.shape[-1]
    proj = (_dot(o_sb.reshape(rows, sb_w).astype(BF16), wo_ref[0:sb_w, :])
            + _dot(o_fx.reshape(rows, fx_w).astype(BF16), wo_ref[sb_w:sb_w + fx_w, :]))
    x2 = x_ref[...] + (1.0 + ada[:, 2:3, :]) * proj.reshape(nb, tt, d)
    h = _rms(x2, gffn_ref[...]) * (1.0 + ada[:, 4:5, :]) + ada[:, 3:4, :]
    h = h.reshape(rows, d).astype(BF16)
    d_ff = wg_ref.shape[1]
    f = None
    for lo in range(0, d_ff, FF_CHUNK):
        hi = min(lo + FF_CHUNK, d_ff)
        g = _dot(h, wg_ref[:, lo:hi])
        u = _dot(h, wu_ref[:, lo:hi])
        act = (g * (1.0 / (1.0 + jnp.exp(-g))) * u).astype(BF16)
        part = _dot(act, wd_ref[lo:hi, :])
        f = part if f is None else f + part
    x3 = x2 + (1.0 + ada[:, 5:6, :]) * f.reshape(nb, tt, d)
    y_ref[...] = _rms(x3, gfin_ref[...]) if final_norm else x3


def _post(x, o_sb, o_fx, ada, g_sb, g_fx, wo16, g_ffn, wg16, wu16, wd16, g_fin, final_norm):
    b, t, d = x.shape
    tt = min(t, ROW_BLOCK)
    nb = max(1, ROW_BLOCK // tt)
    tok = lambda w: pl.BlockSpec((nb, tt, w), lambda i, j: (i, j, 0))
    const = lambda a: pl.BlockSpec(a.shape, lambda i, j: (0, 0), pipeline_mode=pl.Buffered(1))
    return pl.pallas_call(
        functools.partial(_post_kernel, final_norm=final_norm),
        grid=(b // nb, t // tt),
        in_specs=[tok(d), tok(o_sb.shape[-1]), tok(o_fx.shape[-1]),
                  pl.BlockSpec((nb, 6, d), lambda i, j: (i, 0, 0)),
                  const(g_sb), const(g_fx), const(wo16), const(g_ffn),
                  const(wg16), const(wu16), const(wd16), const(g_fin)],
        out_specs=tok(d),
        out_shape=jax.ShapeDtypeStruct((b, t, d), F32),
        compiler_params=_params("arbitrary", "arbitrary"),
        name="post_mixer_ffn",
    )(x, o_sb, o_fx, ada, g_sb, g_fx, wo16, g_ffn, wg16, wu16, wd16, g_fin)


def _pad_lanes(a, n):
    return jnp.pad(a, ((0, 0), (0, n - a.shape[1])))


def kernel(x_prompt, x_sample, c_prompt, c_sample, cache_sb_k, cache_sb_v, cache_fox_k, cache_fox_v, cache_fox_logf, w_ada, b_ada, g_mix, w_in, b_f, g_sb_out, g_fox_out, w_o, g_ffn, w_gate, w_up, w_down, g_final):
    depth = w_ada.shape[0]
    bp, tp, d = x_prompt.shape
    bs, ts, _ = x_sample.shape
    past, n_sb, hd = cache_sb_k.shape[2:]
    n_fx = cache_fox_k.shape[3]
    assert hd == HEAD_DIM and n_sb % 2 == 0 and n_fx % 2 == 0
    assert tp % ROW_BLOCK == 0 and tp % ATT_BLOCK == 0 and ROW_BLOCK % ts == 0 and bs % (ROW_BLOCK // ts) == 0
    assert past % ATT_BLOCK == 0
    sb_w, fx_w = n_sb * hd, n_fx * hd
    att = min(FOX_BLOCK, tp)
    past_pad = -(-(past + ts) // (3 * LANES)) * (3 * LANES)
    row = lambda a: a.reshape(1, -1)

    xp, xs = x_prompt, x_sample
    outs = [[] for _ in range(10)]
    for l in range(depth):
        in_cols = w_in.shape[2]
        w16t = jnp.pad(jnp.swapaxes(w_in[l], 0, 1), ((0, -in_cols % BF16_ROWS), (0, 0))).astype(BF16)
        bfr, bfc = row(b_f[l]), b_f[l].reshape(-1, 1)
        wo16, wg16 = w_o[l].astype(BF16), w_gate[l].astype(BF16)
        wu16, wd16 = w_up[l].astype(BF16), w_down[l].astype(BF16)

        c_all = jnp.concatenate([c_prompt, c_sample], axis=0)
        ada = _ada(c_all, w_ada[l], row(b_ada[l])).reshape(bp + bs, 6, d)
        ada_p, ada_s = ada[:bp], ada[bp:]

        (q_sb, k_sb, v_sb, k_sb16, v_sb16, q_fx, k_fx, v_fx, k_fx16, v_fx16, lf) = _pre(
            xp, ada_p, row(g_mix[l]), w16t, bfr, bfc, sb_w, fx_w, n_fx, True)
        o_sb = _sb_prompt(q_sb, k_sb16, v_sb16)
        f_cum = _cumsum_rows(lf.reshape(bp * n_fx, tp), ROW_BLOCK)
        fq = jnp.swapaxes(f_cum.reshape(bp, n_fx, tp), 1, 2)
        fk = jnp.swapaxes(f_cum.reshape(bp, n_fx // 2, 2, tp // att, att), 2, 3)
        o_fx = _fox_prompt(q_fx, k_fx16, v_fx16, fq, fk)
        xp = _post(xp, o_sb, o_fx, ada_p, row(g_sb_out[l]), row(g_fox_out[l]), wo16, row(g_ffn[l]),
                   wg16, wu16, wd16, row(g_final), l + 1 == depth)
        for dst, a in zip(outs[0:4], (k_sb, v_sb, k_fx, v_fx)):
            dst.append(jnp.swapaxes(a, 1, 2).reshape(bp, tp, -1, hd))
        outs[4].append(jnp.swapaxes(lf, 1, 2))

        (q_sb, k_sb, v_sb, k_sb16, v_sb16, q_fx, k_fx, v_fx, k_fx16, v_fx16, lf) = _pre(
            xs, ada_s, row(g_mix[l]), w16t, bfr, bfc, sb_w, fx_w, n_fx, False)
        time_minor = lambda a: jnp.swapaxes(a.reshape(a.shape[0], a.shape[1], -1), 1, 2)
        o_sb = _sb_sample(q_sb, time_minor(k_sb16), time_minor(v_sb16),
                          time_minor(cache_sb_k[l]), time_minor(cache_sb_v[l]))
        lf_all = jnp.concatenate([cache_fox_logf[l].astype(F32), lf], axis=1)
        lf_rows = _pad_lanes(jnp.swapaxes(lf_all, 1, 2).reshape(bs * n_fx, past + ts), past_pad)
        f_all = _cumsum_rows(lf_rows, past_pad // 3)
        fq = jnp.swapaxes(f_all.reshape(bs, n_fx, past_pad)[:, :, past:past + ts], 1, 2)
        fk = f_all.reshape(bs, n_fx // 2, 2, past_pad)
        o_fx = _fox_sample(q_fx, time_minor(k_fx16), time_minor(v_fx16),
                           time_minor(cache_fox_k[l]), time_minor(cache_fox_v[l]), fq, fk)
        xs = _post(xs, o_sb, o_fx, ada_s, row(g_sb_out[l]), row(g_fox_out[l]), wo16, row(g_ffn[l]),
                   wg16, wu16, wd16, row(g_final), l + 1 == depth)
        for dst, a in zip(outs[5:9], (k_sb, v_sb, k_fx, v_fx)):
            dst.append(a.reshape(bs, ts, -1, hd))
        outs[9].append(lf)

    return (xp, xs) + tuple(jnp.stack(o) for o in outs)
```

```python
import functools

import jax
import jax.numpy as jnp
from jax import lax
from jax.experimental import pallas as pl
from jax.experimental.pallas import tpu as pltpu

F32 = jnp.float32
BF16 = jnp.bfloat16
EPS = 1e-6
HEAD_DIM = 64
LANES = 128
BF16_ROWS = 16
ROW_BLOCK = 512
FF_CHUNK = 512
ATT_BLOCK = 256
SB_BLOCKS_PER_STEP = 8
FOX_BLOCKS_PER_STEP = 2
FOX_BLOCK = 512
NEG_BIG = -1e30
EXP_ZERO = -105.0
NORM_CHUNK = 2048
NORM_SLACK = 1.0 + 2.0 ** -8
VMEM_LIMIT = 56 * 1024 * 1024


def _params(*semantics):
    return pltpu.CompilerParams(dimension_semantics=semantics, vmem_limit_bytes=VMEM_LIMIT)


def _dot(a, b):
    return jnp.dot(a, b, preferred_element_type=F32)


def _dot_nt(a, b):
    return lax.dot_general(a, b, (((1,), (1,)), ((), ())), preferred_element_type=F32)


def _rms(x, g):
    return x * lax.rsqrt(jnp.mean(x * x, axis=-1, keepdims=True) + EPS) * g


def _split_bf16(x, parts):
    out = []
    for _ in range(parts - 1):
        p = x.astype(BF16)
        out.append(p)
        x = x - p.astype(F32)
    out.append(x.astype(BF16))
    return out


def _dot_split(x, m, parts):
    acc = None
    for p in _split_bf16(x, parts):
        d = _dot(p, m)
        acc = d if acc is None else acc + d
    return acc


def _keep_head(x, first, axis=1, fill=0.0):
    pos = lax.broadcasted_iota(jnp.int32, x.shape, axis)
    keep = (pos < HEAD_DIM) if first else (pos >= HEAD_DIM)
    return jnp.where(keep, x, jnp.full_like(x, fill))


def _per_head(col0, col1, shape):
    lane = lax.broadcasted_iota(jnp.int32, shape, 1)
    return jnp.where(lane < HEAD_DIM, col0, col1)


def _softmax_out(acc0, acc1):
    first = lax.broadcasted_iota(jnp.int32, acc0.shape, 1) < HEAD_DIM
    return jnp.where(first, acc0, acc1) / pltpu.roll(jnp.where(first, acc1, acc0), LANES // 2, 1)


def _lanes_to(x, n):
    if n <= LANES:
        return x[:, :n]
    return jnp.concatenate([x] * (n // LANES), axis=1)


def _ada_kernel(c_ref, w_ref, b_ref, o_ref):
    c = c_ref[...]
    s = (c * (1.0 / (1.0 + jnp.exp(-c)))).astype(BF16)
    o_ref[...] = _dot(s, w_ref[...].astype(BF16)) + b_ref[...]


def _ada(c, w, b):
    rows, d = c.shape
    n = w.shape[1]
    tn = 1024
    return pl.pallas_call(
        _ada_kernel,
        grid=(n // tn,),
        in_specs=[pl.BlockSpec((rows, d), lambda j: (0, 0)),
                  pl.BlockSpec((d, tn), lambda j: (0, j)),
                  pl.BlockSpec((1, tn), lambda j: (0, j))],
        out_specs=pl.BlockSpec((rows, tn), lambda j: (0, j)),
        out_shape=jax.ShapeDtypeStruct((rows, n), F32),
        compiler_params=_params("arbitrary"),
        name="ada",
    )(c, w, b)


def _log_sigmoid(u):
    return jnp.minimum(u, 0.0) - jnp.log(1.0 + jnp.exp(-jnp.abs(u)))


def _pre_kernel(x_ref, ada_ref, g_ref, wt_ref, bf_ref, bfc_ref,
                qsb_ref, ksb_ref, vsb_ref, ksb16_ref, vsb16_ref,
                qfx_ref, kfx_ref, vfx_ref, kfx16_ref, vfx16_ref, lf_ref, *, sb_w, fx_w, n_f, time_minor):
    x = x_ref[...]
    nb, tt, d = x.shape
    ada = ada_ref[...]
    h = _rms(x, g_ref[...]) * (1.0 + ada[:, 1:2, :]) + ada[:, 0:1, :]
    h = h.reshape(nb * tt, d).astype(BF16)
    q_scale = HEAD_DIM ** -0.5

    def proj(lo, width):
        return _dot_nt(h, wt_ref[lo:lo + width, :]).reshape(nb, tt, width)

    base = 3 * sb_w
    qsb_ref[...] = (proj(0, sb_w) * q_scale).astype(BF16)
    qfx_ref[...] = (proj(base, fx_w) * q_scale).astype(BF16)
    if time_minor:
        sb = _dot_nt(wt_ref[sb_w:base, :], h)
        fx = _dot_nt(wt_ref[base + fx_w:, :], h)
        parts = (sb[:sb_w], sb[sb_w:], fx[:fx_w], fx[fx_w:2 * fx_w])
        lf_ref[...] = _log_sigmoid(fx[2 * fx_w:2 * fx_w + n_f] + bfc_ref[...])[None]
    else:
        parts = (proj(sb_w, sb_w), proj(2 * sb_w, sb_w), proj(base + fx_w, fx_w), proj(base + 2 * fx_w, fx_w))
        forget = _dot_nt(h, wt_ref[base + 3 * fx_w:, :])
        lf_ref[...] = _log_sigmoid(forget[:, :n_f] + bf_ref[...]).reshape(nb, tt, n_f)
    for r, out, out16 in zip(parts, (ksb_ref, vsb_ref, kfx_ref, vfx_ref),
                             (ksb16_ref, vsb16_ref, kfx16_ref, vfx16_ref)):
        r = r[None] if time_minor else r
        out[...] = r
        out16[...] = r.astype(BF16)


def _pre(x, ada, g, w16t, bfr, bfc, sb_w, fx_w, n_f, time_minor):
    b, t, d = x.shape
    tt = min(t, ROW_BLOCK)
    nb = max(1, ROW_BLOCK // tt)
    assert nb == 1 or not time_minor
    grid = (b // nb, t // tt)
    tok = lambda w: pl.BlockSpec((nb, tt, w), lambda i, j: (i, j, 0))
    const = lambda a: pl.BlockSpec(a.shape, lambda i, j: (0, 0))
    shape = lambda w, dt: jax.ShapeDtypeStruct((b, t, w), dt)
    if time_minor:
        kv = lambda w: pl.BlockSpec((1, w, tt), lambda i, j: (i, 0, j))
        kv_shape = lambda w, dt: jax.ShapeDtypeStruct((b, w, t), dt)
    else:
        kv, kv_shape = tok, shape
    return pl.pallas_call(
        functools.partial(_pre_kernel, sb_w=sb_w, fx_w=fx_w, n_f=n_f, time_minor=time_minor),
        grid=grid,
        in_specs=[tok(d), pl.BlockSpec((nb, 6, d), lambda i, j: (i, 0, 0)),
                  const(g), const(w16t), const(bfr), const(bfc)],
        out_specs=[tok(sb_w)] + [kv(sb_w)] * 4 + [tok(fx_w)] + [kv(fx_w)] * 4 + [kv(n_f)],
        out_shape=[shape(sb_w, BF16), kv_shape(sb_w, F32), kv_shape(sb_w, F32), kv_shape(sb_w, BF16),
                   kv_shape(sb_w, BF16),
                   shape(fx_w, BF16), kv_shape(fx_w, F32), kv_shape(fx_w, F32), kv_shape(fx_w, BF16),
                   kv_shape(fx_w, BF16),
                   kv_shape(n_f, F32)],
        compiler_params=_params("arbitrary", "arbitrary"),
        name="pre_mixer",
    )(x, ada, g, w16t, bfr, bfc)


def _cumsum_kernel(x_ref, o_ref, *, tc):
    r = lax.broadcasted_iota(jnp.int32, (tc, tc), 0)
    c = lax.broadcasted_iota(jnp.int32, (tc, tc), 1)
    upper = jnp.where(r <= c, 1.0, 0.0).astype(BF16)
    ones = jnp.ones((tc, LANES), BF16)
    carry = jnp.zeros((x_ref.shape[0], LANES), F32)
    for lo in range(0, x_ref.shape[1], tc):
        x = x_ref[:, lo:lo + tc]
        o_ref[:, lo:lo + tc] = _dot_split(x, upper, 3) + _lanes_to(carry, tc)
        carry = carry + _dot_split(x, ones, 3)


def _cumsum_rows(x, tc):
    return pl.pallas_call(
        functools.partial(_cumsum_kernel, tc=tc),
        out_shape=jax.ShapeDtypeStruct(x.shape, F32),
        compiler_params=pltpu.CompilerParams(vmem_limit_bytes=VMEM_LIMIT),
        name="cumsum_time",
    )(x)


def _tri_strict(n):
    r = lax.broadcasted_iota(jnp.int32, (n, n), 0)
    c = lax.broadcasted_iota(jnp.int32, (n, n), 1)
    return jnp.where(r > c, 1.0, 0.0).astype(BF16)


def _causal(tq, tk, strict):
    r = lax.broadcasted_iota(jnp.int32, (tq, tk), 0)
    c = lax.broadcasted_iota(jnp.int32, (tq, tk), 1)
    return (c < r) if strict else (c <= r)


def _sb_tiles(chains, mask):
    tk = chains[0][1].shape[1]
    tri = _tri_strict(tk)
    wide = tk % LANES == 0
    if wide:
        tri2 = jnp.concatenate([tri, tri], axis=0)
    zs = [_dot(q, k) for q, k, _, _ in chains]
    staged = []
    for z in zs:
        log_beta = jnp.minimum(z, 0.0) - jnp.log(1.0 + jnp.exp(-jnp.abs(z)))
        log_keep = log_beta - z
        if mask is not None:
            log_keep = jnp.where(mask, log_keep, 0.0)
        hi, lo = _split_bf16(log_keep, 2)
        staged.append((log_beta, log_keep[:, 0:1], hi, lo))
    if wide:
        betweens = [_dot(jnp.concatenate([hi, lo], axis=1), tri2) for _, _, hi, lo in staged]
    else:
        betweens = [_dot(hi, tri) + _dot(lo, tri) for _, _, hi, lo in staged]
    weights = []
    for (log_beta, _, _, _), between, (_, _, _, carry) in zip(staged, betweens, chains):
        a = jnp.exp(log_beta + between + carry)
        if mask is not None:
            a = jnp.where(mask, a, 0.0)
        weights.append(a.astype(BF16))
    pvs = [_dot_nt(a, v) for a, (_, _, v, _) in zip(weights, chains)]
    return [(pv, carry + (between[:, 0:1] + first))
            for pv, between, (_, first, _, _), (_, _, _, carry) in zip(pvs, betweens, staged, chains)]


def _sb_start(rows, groups):
    col = jnp.zeros((rows, 1), F32)
    return (col, col, jnp.zeros((rows, LANES), F32)) * groups


def _sb_visit(qs, kvs, state, mask):
    chains = []
    for g, ((q0, q1), (k, v)) in enumerate(zip(qs, kvs)):
        chains.append((q0, k, _keep_head(v, True, 0), state[3 * g]))
        chains.append((q1, k, _keep_head(v, False, 0), state[3 * g + 1]))
    res = _sb_tiles(chains, mask)
    out = ()
    for g in range(len(qs)):
        (pv0, c0), (pv1, c1) = res[2 * g], res[2 * g + 1]
        out += (c0, c1, state[3 * g + 2] + pv0 + pv1)
    return out


def _sb_sweep(visit, lefts, state):
    groups = len(lefts)

    def live(s):
        alive = jnp.bool_(False)
        for g in range(groups):
            more = jnp.max(jnp.maximum(s[1 + 3 * g], s[2 + 3 * g])) > EXP_ZERO
            alive = jnp.logical_or(alive, jnp.logical_and(s[0] < lefts[g], more))
        return alive

    def step(s):
        i, old = s[0], s[1:]
        new = visit([jnp.maximum(lefts[g] - 1 - i, 0) for g in range(groups)], old, None)
        out = (i + 1,)
        for g in range(groups):
            inside = i < lefts[g]
            out += tuple(jnp.where(inside, n, o) for n, o in zip(new[3 * g:3 * g + 3], old[3 * g:3 * g + 3]))
        return out

    return lax.while_loop(live, step, (jnp.int32(0),) + tuple(state))[1:]


def _head_column(f, h):
    lane = lax.broadcasted_iota(jnp.int32, f.shape, 1)
    return jnp.sum(jnp.where(lane == h, f, 0.0), axis=1, keepdims=True)


def _sb_prompt_kernel(q_ref, k_ref, v_ref, o_ref, *, tile, blocks):
    first = pl.program_id(2) * blocks
    qs = []
    for j in range(blocks):
        q = q_ref[0, j * tile:(j + 1) * tile, :]
        qs.append((_keep_head(q, True), _keep_head(q, False)))

    def visit(kbs, state, mask):
        kvs = []
        for j in range(blocks):
            start = pl.multiple_of(kbs[j] * tile, tile)
            kvs.append((k_ref[0, :, pl.ds(start, tile)], v_ref[0, :, pl.ds(start, tile)]))
        return _sb_visit(qs, kvs, state, mask)

    lefts = [first + j for j in range(blocks)]
    state = visit(lefts, _sb_start(tile, blocks), _causal(tile, tile, True))
    state = _sb_sweep(visit, lefts, state)
    for j in range(blocks):
        o_ref[0, j * tile:(j + 1) * tile, :] = state[2 + 3 * j]


def _head_norms_sq(x16):
    x = x16.astype(F32)
    r = lax.broadcasted_iota(jnp.int32, (LANES, LANES), 0)
    c = lax.broadcasted_iota(jnp.int32, (LANES, LANES), 1)
    same_head = jnp.where((r < HEAD_DIM) == (c < HEAD_DIM), 1.0, 0.0).astype(BF16)
    return _dot_split(x * x, same_head, 2)


def _fox_prompt_kernel(q_ref, k_ref, v_ref, fq_ref, fk_ref, o_ref, kmax_ref, *, tile, blocks):
    hp = pl.program_id(1)
    first = pl.program_id(2) * blocks
    shape = (tile, LANES)
    half_tile = tile // 2
    causal = _causal(tile, tile, False)
    zero = jnp.zeros(shape, F32)
    every = range(blocks)

    @pl.when(first == 0)
    def _():
        width = min(NORM_CHUNK, k_ref.shape[2])

        def chunk(i, best):
            kc = k_ref[0, :, pl.ds(pl.multiple_of(i * width, width), width)].astype(F32)
            sq = kc * kc
            return (jnp.maximum(best[0], jnp.sum(sq[:HEAD_DIM], axis=0, keepdims=True)),
                    jnp.maximum(best[1], jnp.sum(sq[HEAD_DIM:], axis=0, keepdims=True)))

        none = jnp.zeros((1, width), F32)
        best = lax.fori_loop(0, k_ref.shape[2] // width, chunk, (none, none))
        kmax_ref[...] = _per_head(jnp.max(best[0], axis=1, keepdims=True),
                                  jnp.max(best[1], axis=1, keepdims=True), (1, LANES))

    kmax = jnp.sqrt(kmax_ref[...]) * NORM_SLACK
    qis = [first + j for j in every]
    qs, fqs, reach = [], [], []
    for j in every:
        rows = slice(j * tile, (j + 1) * tile)
        q = q_ref[0, rows, :]
        qs.append((_keep_head(q, True), _keep_head(q, False)))
        fq = fq_ref[0, rows, :]
        fqs.append((_head_column(fq, 2 * hp), _head_column(fq, 2 * hp + 1)))
        reach.append(jnp.sqrt(_head_norms_sq(q)) * kmax + _per_head(fqs[j][0], fqs[j][1], shape))

    def tile_left(j, i):
        return jnp.maximum(qis[j] - 1 - i, 0)

    def key_cols(kb):
        return pl.ds(pl.multiple_of(kb * tile, tile), tile)

    def last_fk(kb):
        fk = fk_ref[0, 0, kb]
        return _per_head(fk[0:1, tile - 1:tile], fk[1:2, tile - 1:tile], (1, LANES))

    def scores(j, kb):
        k = k_ref[0, :, key_cols(kb)]
        fk = fk_ref[0, 0, kb]
        return _dot(qs[j][0], k) - fk[0:1, :], _dot(qs[j][1], k) - fk[1:2, :]

    def diagonal_scores(j):
        k = k_ref[0, :, key_cols(qis[j])]
        fk = fk_ref[0, 0, qis[j]]
        raw = [(_dot(q, k[:, :half_tile]) - fk[r:r + 1, :half_tile],
                _dot(q[half_tile:], k[:, half_tile:]) - fk[r:r + 1, half_tile:]) for r, q in enumerate(qs[j])]
        unseen = jnp.full((half_tile, half_tile), NEG_BIG, F32)
        return [(jnp.where(causal[:, :half_tile], left, NEG_BIG),
                 jnp.concatenate([unseen, jnp.where(causal[half_tile:, half_tile:], right, NEG_BIG)], axis=0))
                for left, right in raw]

    def best_gap(j, kb, m0, m1):
        return jnp.max(reach[j] - _per_head(m0, m1, shape) - last_fk(kb))

    def max_live(s):
        alive = jnp.bool_(False)
        for j in every:
            alive = jnp.logical_or(alive, jnp.logical_and(s[0] < qis[j], s[1 + j] > 0.0))
        return alive

    def max_step(s):
        i, ms = s[0], s[1 + blocks:]
        us = [scores(j, tile_left(j, i)) for j in every]
        gaps, new = (), ()
        for j in every:
            inside = i < qis[j]
            both = [jnp.where(inside, jnp.maximum(ms[2 * j + r], jnp.max(us[j][r], axis=1, keepdims=True) + fqs[j][r]),
                              ms[2 * j + r]) for r in range(2)]
            gaps += (best_gap(j, tile_left(j, i + 1), both[0], both[1]),)
            new += tuple(both)
        return (i + 1,) + gaps + new

    diagonal = [diagonal_scores(j) for j in every]
    ms = ()
    for j in every:
        ms += tuple(jnp.maximum(jnp.max(left, axis=1, keepdims=True), jnp.max(right, axis=1, keepdims=True)) + fq
                    for (left, right), fq in zip(diagonal[j], fqs[j]))
    gaps = tuple(best_gap(j, tile_left(j, 0), ms[2 * j], ms[2 * j + 1]) for j in every)
    ms = lax.while_loop(max_live, max_step, (jnp.int32(0),) + gaps + ms)[1 + blocks:]

    cs = [(ms[2 * j] - fqs[j][0], ms[2 * j + 1] - fqs[j][1]) for j in every]
    gap_rows = [jnp.max(reach[j] - _per_head(ms[2 * j], ms[2 * j + 1], shape), axis=0, keepdims=True) for j in every]

    def head_gaps(j, kb):
        gap = gap_rows[j] - last_fk(kb)
        head0 = lax.broadcasted_iota(jnp.int32, gap.shape, 1) < HEAD_DIM
        return jnp.max(jnp.where(head0, gap, NEG_BIG)), jnp.max(jnp.where(head0, NEG_BIG, gap))

    def with_ones(v, head1):
        ones_rows = (lax.broadcasted_iota(jnp.int32, v.shape, 0) < HEAD_DIM) == head1
        return jnp.where(ones_rows, jnp.ones_like(v), v)

    blank = jnp.zeros((half_tile, LANES), F32)
    state = ()
    for j in every:
        v = v_ref[0, :, key_cols(qis[j])]
        weights = [(jnp.exp(left - c).astype(BF16), jnp.exp(right - c).astype(BF16))
                   for (left, right), c in zip(diagonal[j], cs[j])]
        accs = [_dot_nt(left, vh[:, :half_tile])
                + jnp.concatenate([blank, _dot_nt(right[half_tile:], vh[:, half_tile:])], axis=0)
                for (left, right), vh in zip(weights, (with_ones(v, False), with_ones(v, True)))]
        state += (jnp.int32(0),) + head_gaps(j, tile_left(j, 0)) + tuple(accs)

    def pair_active(s, j):
        done, gap0, gap1 = s[1 + 5 * j:4 + 5 * j]
        return jnp.logical_and(s[0] < qis[j], jnp.minimum(gap0, gap1) > EXP_ZERO)

    def pair_live(s):
        alive = jnp.bool_(False)
        for j in every:
            alive = jnp.logical_or(alive, pair_active(s, j))
        return alive

    def pair_step(s):
        i = s[0]
        kbs = [tile_left(j, i) for j in every]
        us = [scores(j, kbs[j]) for j in every]
        ps = [[jnp.exp(u - c).astype(BF16) for u, c in zip(us[j], cs[j])] for j in every]
        out = (i + 1,)
        for j in every:
            done, gap0, gap1, acc0, acc1 = s[1 + 5 * j:6 + 5 * j]
            active = pair_active(s, j)
            v = v_ref[0, :, key_cols(kbs[j])]
            new0, new1 = head_gaps(j, tile_left(j, i + 1))
            out += (jnp.where(active, i + 1, done), jnp.where(active, new0, gap0), jnp.where(active, new1, gap1),
                    acc0 + jnp.where(active, _dot_nt(ps[j][0], with_ones(v, False)), 0.0),
                    acc1 + jnp.where(active, _dot_nt(ps[j][1], with_ones(v, True)), 0.0))
        return out

    state = lax.while_loop(pair_live, pair_step, (jnp.int32(0),) + state)[1:]

    seconds = [state[5 * j + 2] > state[5 * j + 1] for j in every]
    starts = [state[5 * j] for j in every]
    q_lone = [jnp.where(seconds[j], qs[j][1], qs[j][0]) for j in every]
    c_lone = [jnp.where(seconds[j], cs[j][1], cs[j][0]) for j in every]

    def lone_active(s, j):
        return jnp.logical_and(starts[j] + s[0] < qis[j], s[1 + 2 * j] > EXP_ZERO)

    def lone_live(s):
        alive = jnp.bool_(False)
        for j in every:
            alive = jnp.logical_or(alive, lone_active(s, j))
        return alive

    def lone_step(s):
        t = s[0]
        kbs = [tile_left(j, starts[j] + t) for j in every]
        us = []
        for j in every:
            k = k_ref[0, :, key_cols(kbs[j])]
            fk = fk_ref[0, 0, kbs[j]]
            fk_s = jnp.where(seconds[j], fk[1:2, :], fk[0:1, :])
            us.append([_dot(q_lone[j], k[:, lo:lo + half_tile]) - fk_s[:, lo:lo + half_tile] for lo in (0, half_tile)])
        ps = [[jnp.exp(u - c_lone[j]).astype(BF16) for u in us[j]] for j in every]
        out = (t + 1,)
        for j in every:
            gap, acc = s[1 + 2 * j:3 + 2 * j]
            active = lone_active(s, j)
            v_s = with_ones(v_ref[0, :, key_cols(kbs[j])], seconds[j])
            gaps = head_gaps(j, tile_left(j, starts[j] + t + 1))
            more = _dot_nt(ps[j][0], v_s[:, :half_tile]) + _dot_nt(ps[j][1], v_s[:, half_tile:])
            out += (jnp.where(active, jnp.where(seconds[j], gaps[1], gaps[0]), gap), acc + jnp.where(active, more, 0.0))
        return out

    lone = (jnp.int32(0),)
    for j in every:
        lone += (jnp.maximum(state[5 * j + 1], state[5 * j + 2]), zero)
    lone = lax.while_loop(lone_live, lone_step, lone)[1:]

    for j in every:
        acc0 = state[5 * j + 3] + jnp.where(seconds[j], 0.0, lone[2 * j + 1])
        acc1 = state[5 * j + 4] + jnp.where(seconds[j], lone[2 * j + 1], 0.0)
        o_ref[0, j * tile:(j + 1) * tile, :] = _softmax_out(acc0, acc1)


def _prompt_specs(t, tile):
    qspec = pl.BlockSpec((1, tile, LANES), lambda b, h, i: (b, i, h))
    kvspec = pl.BlockSpec((1, LANES, t), lambda b, h, i: (b, h, 0))
    return qspec, kvspec


def _sb_prompt(q, k, v):
    b, t, w = q.shape
    tile = ATT_BLOCK
    blocks = SB_BLOCKS_PER_STEP if t % (tile * SB_BLOCKS_PER_STEP) == 0 else 1
    qspec, kvspec = _prompt_specs(t, tile * blocks)
    return pl.pallas_call(
        functools.partial(_sb_prompt_kernel, tile=tile, blocks=blocks),
        grid=(b, w // LANES, t // (tile * blocks)),
        in_specs=[qspec, kvspec, kvspec],
        out_specs=qspec,
        out_shape=jax.ShapeDtypeStruct((b, t, w), F32),
        compiler_params=_params("arbitrary", "arbitrary", "arbitrary"),
        name="sb_prompt",
    )(q, k, v)


def _fox_prompt(q, k, v, fq, fk):
    b, t, w = q.shape
    tile = fk.shape[-1]
    blocks = FOX_BLOCKS_PER_STEP if t % (tile * FOX_BLOCKS_PER_STEP) == 0 else 1
    qspec, kvspec = _prompt_specs(t, tile * blocks)
    n_f = fq.shape[-1]
    return pl.pallas_call(
        functools.partial(_fox_prompt_kernel, tile=tile, blocks=blocks),
        grid=(b, w // LANES, t // (tile * blocks)),
        in_specs=[qspec, kvspec, kvspec,
                  pl.BlockSpec((1, tile * blocks, n_f), lambda b, h, i: (b, i, 0)),
                  pl.BlockSpec((1, 1, t // tile, 2, tile), lambda b, h, i: (b, h, 0, 0, 0))],
        out_specs=qspec,
        out_shape=jax.ShapeDtypeStruct((b, t, w), F32),
        scratch_shapes=[pltpu.VMEM((1, LANES), F32)],
        compiler_params=_params("arbitrary", "arbitrary", "arbitrary"),
        name="fox_prompt",
    )(q, k, v, fq, fk)


def _group_rows(g):
    return slice(g * LANES, (g + 1) * LANES)


def _sb_sample_kernel(q_ref, kn_ref, vn_ref, kc_ref, vc_ref, o_ref, *, tile):
    n, width = q_ref.shape[1:]
    groups = width // LANES
    past = kc_ref.shape[2]
    qs = []
    for g in range(groups):
        q = q_ref[0, :, _group_rows(g)]
        qs.append((_keep_head(q, True), _keep_head(q, False)))

    def visit(kbs, state, mask):
        kvs = []
        for g in range(groups):
            if mask is not None:
                kvs.append((kn_ref[0, _group_rows(g), :], vn_ref[0, _group_rows(g), :]))
            else:
                cols = pl.ds(pl.multiple_of(kbs[g] * tile, tile), tile)
                kvs.append((kc_ref[0, _group_rows(g), cols].astype(BF16),
                            vc_ref[0, _group_rows(g), cols].astype(BF16)))
        return _sb_visit(qs, kvs, state, mask)

    state = visit(None, _sb_start(n, groups), _causal(n, n, True))
    state = _sb_sweep(visit, [past // tile] * groups, state)
    for g in range(groups):
        o_ref[0, :, _group_rows(g)] = state[2 + 3 * g]


def _fox_sample_kernel(q_ref, kn_ref, vn_ref, kc_ref, vc_ref, fq_ref, fk_ref, o_ref):
    n, width = q_ref.shape[1:]
    groups = width // LANES
    past = kc_ref.shape[2]
    fq = fq_ref[0]
    causal = _causal(n, n, False)
    scored = []
    for g in range(groups):
        q = q_ref[0, :, _group_rows(g)]
        kn = kn_ref[0, _group_rows(g), :]
        kc = kc_ref[0, _group_rows(g), :].astype(BF16)
        for h in range(2):
            qh = _keep_head(q, h == 0)
            fk = fk_ref[0, g, h:h + 1, :]
            u_new = jnp.where(causal, _dot(qh, kn) - fk[:, past:past + n], NEG_BIG)
            u_old = _dot(qh, kc) - fk[:, 0:past]
            scored.append((u_new, u_old, fq[:, 2 * g + h:2 * g + h + 1]))
    weights = []
    for u_new, u_old, fq_col in scored:
        m = jnp.maximum(jnp.max(u_new, axis=1, keepdims=True), jnp.max(u_old, axis=1, keepdims=True)) + fq_col
        c = m - fq_col
        weights.append((jnp.exp(u_new - c).astype(BF16), jnp.exp(u_old - c).astype(BF16)))
    for g in range(groups):
        vn = vn_ref[0, _group_rows(g), :]
        vc = vc_ref[0, _group_rows(g), :].astype(BF16)
        accs = []
        for h in range(2):
            p_new, p_old = weights[2 * g + h]
            accs.append(_dot_nt(p_new, _keep_head(vn, h == 0, 0, 1.0))
                        + _dot_nt(p_old, _keep_head(vc, h == 0, 0, 1.0)))
        o_ref[0, :, _group_rows(g)] = _softmax_out(accs[0], accs[1])


def _sample_specs(n, width, past):
    qspec = pl.BlockSpec((1, n, width), lambda b: (b, 0, 0))
    new = pl.BlockSpec((1, width, n), lambda b: (b, 0, 0))
    cache = pl.BlockSpec((1, width, past), lambda b: (b, 0, 0))
    return qspec, new, cache


def _sb_sample(q, kn, vn, kc, vc):
    b, n, w = q.shape
    qspec, new, cache = _sample_specs(n, w, kc.shape[2])
    return pl.pallas_call(
        functools.partial(_sb_sample_kernel, tile=ATT_BLOCK),
        grid=(b,),
        in_specs=[qspec, new, new, cache, cache],
        out_specs=qspec,
        out_shape=jax.ShapeDtypeStruct((b, n, w), F32),
        compiler_params=_params("arbitrary"),
        name="sb_sample",
    )(q, kn, vn, kc, vc)


def _fox_sample(q, kn, vn, kc, vc, fq, fk):
    b, n, w = q.shape
    qspec, new, cache = _sample_specs(n, w, kc.shape[2])
    return pl.pallas_call(
        _fox_sample_kernel,
        grid=(b,),
        in_specs=[qspec, new, new, cache, cache,
                  pl.BlockSpec((1, n, fq.shape[-1]), lambda b: (b, 0, 0)),
                  pl.BlockSpec((1,) + fk.shape[1:], lambda b: (b, 0, 0, 0))],
        out_specs=qspec,
        out_shape=jax.ShapeDtypeStruct((b, n, w), F32),
        compiler_params=_params("arbitrary"),
        name="fox_sample",
    )(q, kn, vn, kc, vc, fq, fk)


def _post_kernel(x_ref, osb_ref, ofx_ref, ada_ref, gsb_ref, gfx_ref, wo_ref, gffn_ref,
                 wg_ref, wu_ref, wd_ref, gfin_ref, y_ref, *, final_norm):
    nb, tt, d = x_ref.shape
    rows = nb * tt
    ada = ada_ref[...]
    o_sb = _rms(osb_ref[...], gsb_ref[...])
    o_fx = _rms(ofx_ref[...], gfx_ref[...])
    sb_w, fx_w = o_sb.shape[-1], o_fx.shape[-1]
    proj = (_dot(o_sb.reshape(rows, sb_w).astype(BF16), wo_ref[0:sb_w, :])
            + _dot(o_fx.reshape(rows, fx_w).astype(BF16), wo_ref[sb_w:sb_w + fx_w, :]))
    x2 = x_ref[...] + (1.0 + ada[:, 2:3, :]) * proj.reshape(nb, tt, d)
    h = _rms(x2, gffn_ref[...]) * (1.0 + ada[:, 4:5, :]) + ada[:, 3:4, :]
    h = h.reshape(rows, d).astype(BF16)
    d_ff = wg_ref.shape[1]
    f = None
    for lo in range(0, d_ff, FF_CHUNK):
        hi = min(lo + FF_CHUNK, d_ff)
        g = _dot(h, wg_ref[:, lo:hi])
        u = _dot(h, wu_ref[:, lo:hi])
        act = (g * (1.0 / (1.0 + jnp.exp(-g))) * u).astype(BF16)
        part = _dot(act, wd_ref[lo:hi, :])
        f = part if f is None else f + part
    x3 = x2 + (1.0 + ada[:, 5:6, :]) * f.reshape(nb, tt, d)
    y_ref[...] = _rms(x3, gfin_ref[...]) if final_norm else x3


def _post(x, o_sb, o_fx, ada, g_sb, g_fx, wo16, g_ffn, wg16, wu16, wd16, g_fin, final_norm):
    b, t, d = x.shape
    tt = min(t, ROW_BLOCK)
    nb = max(1, ROW_BLOCK // tt)
    tok = lambda w: pl.BlockSpec((nb, tt, w), lambda i, j: (i, j, 0))
    const = lambda a: pl.BlockSpec(a.shape, lambda i, j: (0, 0), pipeline_mode=pl.Buffered(1))
    return pl.pallas_call(
        functools.partial(_post_kernel, final_norm=final_norm),
        grid=(b // nb, t // tt),
        in_specs=[tok(d), tok(o_sb.shape[-1]), tok(o_fx.shape[-1]),
                  pl.BlockSpec((nb, 6, d), lambda i, j: (i, 0, 0)),
                  const(g_sb), const(g_fx), const(wo16), const(g_ffn),
                  const(wg16), const(wu16), const(wd16), const(g_fin)],
        out_specs=tok(d),
        out_shape=jax.ShapeDtypeStruct((b, t, d), F32),
        compiler_params=_params("arbitrary", "arbitrary"),
        name="post_mixer_ffn",
    )(x, o_sb, o_fx, ada, g_sb, g_fx, wo16, g_ffn, wg16, wu16, wd16, g_fin)


def _pad_lanes(a, n):
    return jnp.pad(a, ((0, 0), (0, n - a.shape[1])))


def kernel(x_prompt, x_sample, c_prompt, c_sample, cache_sb_k, cache_sb_v, cache_fox_k, cache_fox_v, cache_fox_logf, w_ada, b_ada, g_mix, w_in, b_f, g_sb_out, g_fox_out, w_o, g_ffn, w_gate, w_up, w_down, g_final):
    depth = w_ada.shape[0]
    bp, tp, d = x_prompt.shape
    bs, ts, _ = x_sample.shape
    past, n_sb, hd = cache_sb_k.shape[2:]
    n_fx = cache_fox_k.shape[3]
    assert hd == HEAD_DIM and n_sb % 2 == 0 and n_fx % 2 == 0
    assert tp % ROW_BLOCK == 0 and tp % ATT_BLOCK == 0 and ROW_BLOCK % ts == 0 and bs % (ROW_BLOCK // ts) == 0
    assert past % ATT_BLOCK == 0
    sb_w, fx_w = n_sb * hd, n_fx * hd
    att = min(FOX_BLOCK, tp)
    past_pad = -(-(past + ts) // (3 * LANES)) * (3 * LANES)
    row = lambda a: a.reshape(1, -1)

    xp, xs = x_prompt, x_sample
    outs = [[] for _ in range(10)]
    for l in range(depth):
        in_cols = w_in.shape[2]
        w16t = jnp.pad(jnp.swapaxes(w_in[l], 0, 1), ((0, -in_cols % BF16_ROWS), (0, 0))).astype(BF16)
        bfr, bfc = row(b_f[l]), b_f[l].reshape(-1, 1)
        wo16, wg16 = w_o[l].astype(BF16), w_gate[l].astype(BF16)
        wu16, wd16 = w_up[l].astype(BF16), w_down[l].astype(BF16)

        c_all = jnp.concatenate([c_prompt, c_sample], axis=0)
        ada = _ada(c_all, w_ada[l], row(b_ada[l])).reshape(bp + bs, 6, d)
        ada_p, ada_s = ada[:bp], ada[bp:]

        (q_sb, k_sb, v_sb, k_sb16, v_sb16, q_fx, k_fx, v_fx, k_fx16, v_fx16, lf) = _pre(
            xp, ada_p, row(g_mix[l]), w16t, bfr, bfc, sb_w, fx_w, n_fx, True)
        o_sb = _sb_prompt(q_sb, k_sb16, v_sb16)
        f_cum = _cumsum_rows(lf.reshape(bp * n_fx, tp), ROW_BLOCK)
        fq = jnp.swapaxes(f_cum.reshape(bp, n_fx, tp), 1, 2)
        fk = jnp.swapaxes(f_cum.reshape(bp, n_fx // 2, 2, tp // att, att), 2, 3)
        o_fx = _fox_prompt(q_fx, k_fx16, v_fx16, fq, fk)
        xp = _post(xp, o_sb, o_fx, ada_p, row(g_sb_out[l]), row(g_fox_out[l]), wo16, row(g_ffn[l]),
                   wg16, wu16, wd16, row(g_final), l + 1 == depth)
        for dst, a in zip(outs[0:4], (k_sb, v_sb, k_fx, v_fx)):
            dst.append(jnp.swapaxes(a, 1, 2).reshape(bp, tp, -1, hd))
        outs[4].append(jnp.swapaxes(lf, 1, 2))

        (q_sb, k_sb, v_sb, k_sb16, v_sb16, q_fx, k_fx, v_fx, k_fx16, v_fx16, lf) = _pre(
            xs, ada_s, row(g_mix[l]), w16t, bfr, bfc, sb_w, fx_w, n_fx, False)
        time_minor = lambda a: jnp.swapaxes(a.reshape(a.shape[0], a.shape[1], -1), 1, 2)
        o_sb = _sb_sample(q_sb, time_minor(k_sb16), time_minor(v_sb16),
                          time_minor(cache_sb_k[l]), time_minor(cache_sb_v[l]))
        lf_all = jnp.concatenate([cache_fox_logf[l].astype(F32), lf], axis=1)
        lf_rows = _pad_lanes(jnp.swapaxes(lf_all, 1, 2).reshape(bs * n_fx, past + ts), past_pad)
        f_all = _cumsum_rows(lf_rows, past_pad // 3)
        fq = jnp.swapaxes(f_all.reshape(bs, n_fx, past_pad)[:, :, past:past + ts], 1, 2)
        fk = f_all.reshape(bs, n_fx // 2, 2, past_pad)
        o_fx = _fox_sample(q_fx, time_minor(k_fx16), time_minor(v_fx16),
                           time_minor(cache_fox_k[l]), time_minor(cache_fox_v[l]), fq, fk)
        xs = _post(xs, o_sb, o_fx, ada_s, row(g_sb_out[l]), row(g_fox_out[l]), wo16, row(g_ffn[l]),
                   wg16, wu16, wd16, row(g_final), l + 1 == depth)
        for dst, a in zip(outs[5:9], (k_sb, v_sb, k_fx, v_fx)):
            dst.append(a.reshape(bs, ts, -1, hd))
        outs[9].append(lf)

    return (xp, xs) + tuple(jnp.stack(o) for o in outs)
```

```python
import functools

import jax
import jax.numpy as jnp
from jax import lax
from jax.experimental import pallas as pl
from jax.experimental.pallas import tpu as pltpu

F32 = jnp.float32
BF16 = jnp.bfloat16
EPS = 1e-6
HEAD_DIM = 64
LANES = 128
BF16_ROWS = 16
ROW_BLOCK = 512
FF_CHUNK = 512
ATT_BLOCK = 256
SB_BLOCKS_PER_STEP = 8
FOX_BLOCKS_PER_STEP = 2
FOX_BLOCK = 512
NEG_BIG = -1e30
EXP_ZERO = -105.0
NORM_CHUNK = 2048
NORM_SLACK = 1.0 + 2.0 ** -8
VMEM_LIMIT = 56 * 1024 * 1024


def _params(*semantics):
    return pltpu.CompilerParams(dimension_semantics=semantics, vmem_limit_bytes=VMEM_LIMIT)


def _dot(a, b):
    return jnp.dot(a, b, preferred_element_type=F32)


def _dot_nt(a, b):
    return lax.dot_general(a, b, (((1,), (1,)), ((), ())), preferred_element_type=F32)


def _rms(x, g):
    return x * lax.rsqrt(jnp.mean(x * x, axis=-1, keepdims=True) + EPS) * g


def _split_bf16(x, parts):
    out = []
    for _ in range(parts - 1):
        p = x.astype(BF16)
        out.append(p)
        x = x - p.astype(F32)
    out.append(x.astype(BF16))
    return out


def _dot_split(x, m, parts):
    acc = None
    for p in _split_bf16(x, parts):
        d = _dot(p, m)
        acc = d if acc is None else acc + d
    return acc


def _keep_head(x, first, axis=1, fill=0.0):
    pos = lax.broadcasted_iota(jnp.int32, x.shape, axis)
    keep = (pos < HEAD_DIM) if first else (pos >= HEAD_DIM)
    return jnp.where(keep, x, jnp.full_like(x, fill))


def _per_head(col0, col1, shape):
    lane = lax.broadcasted_iota(jnp.int32, shape, 1)
    return jnp.where(lane < HEAD_DIM, col0, col1)


def _softmax_out(acc0, acc1):
    first = lax.broadcasted_iota(jnp.int32, acc0.shape, 1) < HEAD_DIM
    return jnp.where(first, acc0, acc1) / pltpu.roll(jnp.where(first, acc1, acc0), LANES // 2, 1)


def _lanes_to(x, n):
    if n <= LANES:
        return x[:, :n]
    return jnp.concatenate([x] * (n // LANES), axis=1)


def _ada_kernel(c_ref, w_ref, b_ref, o_ref):
    c = c_ref[...]
    s = (c * (1.0 / (1.0 + jnp.exp(-c)))).astype(BF16)
    o_ref[...] = _dot(s, w_ref[...].astype(BF16)) + b_ref[...]


def _ada(c, w, b):
    rows, d = c.shape
    n = w.shape[1]
    tn = 1024
    return pl.pallas_call(
        _ada_kernel,
        grid=(n // tn,),
        in_specs=[pl.BlockSpec((rows, d), lambda j: (0, 0)),
                  pl.BlockSpec((d, tn), lambda j: (0, j)),
                  pl.BlockSpec((1, tn), lambda j: (0, j))],
        out_specs=pl.BlockSpec((rows, tn), lambda j: (0, j)),
        out_shape=jax.ShapeDtypeStruct((rows, n), F32),
        compiler_params=_params("arbitrary"),
        name="ada",
    )(c, w, b)


def _log_sigmoid(u):
    return jnp.minimum(u, 0.0) - jnp.log(1.0 + jnp.exp(-jnp.abs(u)))


def _pre_kernel(x_ref, ada_ref, g_ref, wt_ref, bf_ref, bfc_ref,
                qsb_ref, ksb_ref, vsb_ref, ksb16_ref, vsb16_ref,
                qfx_ref, kfx_ref, vfx_ref, kfx16_ref, vfx16_ref, lf_ref, *, sb_w, fx_w, n_f, time_minor):
    x = x_ref[...]
    nb, tt, d = x.shape
    ada = ada_ref[...]
    h = _rms(x, g_ref[...]) * (1.0 + ada[:, 1:2, :]) + ada[:, 0:1, :]
    h = h.reshape(nb * tt, d).astype(BF16)
    q_scale = HEAD_DIM ** -0.5

    def proj(lo, width):
        return _dot_nt(h, wt_ref[lo:lo + width, :]).reshape(nb, tt, width)

    base = 3 * sb_w
    qsb_ref[...] = (proj(0, sb_w) * q_scale).astype(BF16)
    qfx_ref[...] = (proj(base, fx_w) * q_scale).astype(BF16)
    if time_minor:
        sb = _dot_nt(wt_ref[sb_w:base, :], h)
        fx = _dot_nt(wt_ref[base + fx_w:, :], h)
        parts = (sb[:sb_w], sb[sb_w:], fx[:fx_w], fx[fx_w:2 * fx_w])
        lf_ref[...] = _log_sigmoid(fx[2 * fx_w:2 * fx_w + n_f] + bfc_ref[...])[None]
    else:
        parts = (proj(sb_w, sb_w), proj(2 * sb_w, sb_w), proj(base + fx_w, fx_w), proj(base + 2 * fx_w, fx_w))
        forget = _dot_nt(h, wt_ref[base + 3 * fx_w:, :])
        lf_ref[...] = _log_sigmoid(forget[:, :n_f] + bf_ref[...]).reshape(nb, tt, n_f)
    for r, out, out16 in zip(parts, (ksb_ref, vsb_ref, kfx_ref, vfx_ref),
                             (ksb16_ref, vsb16_ref, kfx16_ref, vfx16_ref)):
        r = r[None] if time_minor else r
        out[...] = r
        out16[...] = r.astype(BF16)


def _pre(x, ada, g, w16t, bfr, bfc, sb_w, fx_w, n_f, time_minor):
    b, t, d = x.shape
    tt = min(t, ROW_BLOCK)
    nb = max(1, ROW_BLOCK // tt)
    assert nb == 1 or not time_minor
    grid = (b // nb, t // tt)
    tok = lambda w: pl.BlockSpec((nb, tt, w), lambda i, j: (i, j, 0))
    const = lambda a: pl.BlockSpec(a.shape, lambda i, j: (0, 0))
    shape = lambda w, dt: jax.ShapeDtypeStruct((b, t, w), dt)
    if time_minor:
        kv = lambda w: pl.BlockSpec((1, w, tt), lambda i, j: (i, 0, j))
        kv_shape = lambda w, dt: jax.ShapeDtypeStruct((b, w, t), dt)
    else:
        kv, kv_shape = tok, shape
    return pl.pallas_call(
        functools.partial(_pre_kernel, sb_w=sb_w, fx_w=fx_w, n_f=n_f, time_minor=time_minor),
        grid=grid,
        in_specs=[tok(d), pl.BlockSpec((nb, 6, d), lambda i, j: (i, 0, 0)),
                  const(g), const(w16t), const(bfr), const(bfc)],
        out_specs=[tok(sb_w)] + [kv(sb_w)] * 4 + [tok(fx_w)] + [kv(fx_w)] * 4 + [kv(n_f)],
        out_shape=[shape(sb_w, BF16), kv_shape(sb_w, F32), kv_shape(sb_w, F32), kv_shape(sb_w, BF16),
                   kv_shape(sb_w, BF16),
                   shape(fx_w, BF16), kv_shape(fx_w, F32), kv_shape(fx_w, F32), kv_shape(fx_w, BF16),
                   kv_shape(fx_w, BF16),
                   kv_shape(n_f, F32)],
        compiler_params=_params("arbitrary", "arbitrary"),
        name="pre_mixer",
    )(x, ada, g, w16t, bfr, bfc)


def _cumsum_kernel(x_ref, o_ref, *, tc):
    r = lax.broadcasted_iota(jnp.int32, (tc, tc), 0)
    c = lax.broadcasted_iota(jnp.int32, (tc, tc), 1)
    upper = jnp.where(r <= c, 1.0, 0.0).astype(BF16)
    ones = jnp.ones((tc, LANES), BF16)
    carry = jnp.zeros((x_ref.shape[0], LANES), F32)
    for lo in range(0, x_ref.shape[1], tc):
        x = x_ref[:, lo:lo + tc]
        o_ref[:, lo:lo + tc] = _dot_split(x, upper, 3) + _lanes_to(carry, tc)
        carry = carry + _dot_split(x, ones, 3)


def _cumsum_rows(x, tc):
    return pl.pallas_call(
        functools.partial(_cumsum_kernel, tc=tc),
        out_shape=jax.ShapeDtypeStruct(x.shape, F32),
        compiler_params=pltpu.CompilerParams(vmem_limit_bytes=VMEM_LIMIT),
        name="cumsum_time",
    )(x)


def _tri_strict(n):
    r = lax.broadcasted_iota(jnp.int32, (n, n), 0)
    c = lax.broadcasted_iota(jnp.int32, (n, n), 1)
    return jnp.where(r > c, 1.0, 0.0).astype(BF16)


def _causal(tq, tk, strict):
    r = lax.broadcasted_iota(jnp.int32, (tq, tk), 0)
    c = lax.broadcasted_iota(jnp.int32, (tq, tk), 1)
    return (c < r) if strict else (c <= r)


def _sb_tiles(chains, mask):
    tk = chains[0][1].shape[1]
    tri = _tri_strict(tk)
    wide = tk % LANES == 0
    if wide:
        tri2 = jnp.concatenate([tri, tri], axis=0)
    zs = [_dot(q, k) for q, k, _, _ in chains]
    staged = []
    for z in zs:
        log_beta = jnp.minimum(z, 0.0) - jnp.log(1.0 + jnp.exp(-jnp.abs(z)))
        log_keep = log_beta - z
        if mask is not None:
            log_keep = jnp.where(mask, log_keep, 0.0)
        hi, lo = _split_bf16(log_keep, 2)
        staged.append((log_beta, log_keep[:, 0:1], hi, lo))
    if wide:
        betweens = [_dot(jnp.concatenate([hi, lo], axis=1), tri2) for _, _, hi, lo in staged]
    else:
        betweens = [_dot(hi, tri) + _dot(lo, tri) for _, _, hi, lo in staged]
    weights = []
    for (log_beta, _, _, _), between, (_, _, _, carry) in zip(staged, betweens, chains):
        a = jnp.exp(log_beta + between + carry)
        if mask is not None:
            a = jnp.where(mask, a, 0.0)
        weights.append(a.astype(BF16))
    pvs = [_dot_nt(a, v) for a, (_, _, v, _) in zip(weights, chains)]
    return [(pv, carry + (between[:, 0:1] + first))
            for pv, between, (_, first, _, _), (_, _, _, carry) in zip(pvs, betweens, staged, chains)]


def _sb_start(rows, groups):
    col = jnp.zeros((rows, 1), F32)
    return (col, col, jnp.zeros((rows, LANES), F32)) * groups


def _sb_visit(qs, kvs, state, mask):
    chains = []
    for g, ((q0, q1), (k, v)) in enumerate(zip(qs, kvs)):
        chains.append((q0, k, _keep_head(v, True, 0), state[3 * g]))
        chains.append((q1, k, _keep_head(v, False, 0), state[3 * g + 1]))
    res = _sb_tiles(chains, mask)
    out = ()
    for g in range(len(qs)):
        (pv0, c0), (pv1, c1) = res[2 * g], res[2 * g + 1]
        out += (c0, c1, state[3 * g + 2] + pv0 + pv1)
    return out


def _sb_sweep(visit, lefts, state):
    groups = len(lefts)

    def live(s):
        alive = jnp.bool_(False)
        for g in range(groups):
            more = jnp.max(jnp.maximum(s[1 + 3 * g], s[2 + 3 * g])) > EXP_ZERO
            alive = jnp.logical_or(alive, jnp.logical_and(s[0] < lefts[g], more))
        return alive

    def step(s):
        i, old = s[0], s[1:]
        new = visit([jnp.maximum(lefts[g] - 1 - i, 0) for g in range(groups)], old, None)
        out = (i + 1,)
        for g in range(groups):
            inside = i < lefts[g]
            out += tuple(jnp.where(inside, n, o) for n, o in zip(new[3 * g:3 * g + 3], old[3 * g:3 * g + 3]))
        return out

    return lax.while_loop(live, step, (jnp.int32(0),) + tuple(state))[1:]


def _head_column(f, h):
    lane = lax.broadcasted_iota(jnp.int32, f.shape, 1)
    return jnp.sum(jnp.where(lane == h, f, 0.0), axis=1, keepdims=True)


def _sb_prompt_kernel(q_ref, k_ref, v_ref, o_ref, *, tile, blocks):
    first = pl.program_id(2) * blocks
    qs = []
    for j in range(blocks):
        q = q_ref[0, j * tile:(j + 1) * tile, :]
        qs.append((_keep_head(q, True), _keep_head(q, False)))

    def visit(kbs, state, mask):
        kvs = []
        for j in range(blocks):
            start = pl.multiple_of(kbs[j] * tile, tile)
            kvs.append((k_ref[0, :, pl.ds(start, tile)], v_ref[0, :, pl.ds(start, tile)]))
        return _sb_visit(qs, kvs, state, mask)

    lefts = [first + j for j in range(blocks)]
    state = visit(lefts, _sb_start(tile, blocks), _causal(tile, tile, True))
    state = _sb_sweep(visit, lefts, state)
    for j in range(blocks):
        o_ref[0, j * tile:(j + 1) * tile, :] = state[2 + 3 * j]


def _head_norms_sq(x16):
    x = x16.astype(F32)
    r = lax.broadcasted_iota(jnp.int32, (LANES, LANES), 0)
    c = lax.broadcasted_iota(jnp.int32, (LANES, LANES), 1)
    same_head = jnp.where((r < HEAD_DIM) == (c < HEAD_DIM), 1.0, 0.0).astype(BF16)
    return _dot_split(x * x, same_head, 2)


def _fox_prompt_kernel(q_ref, k_ref, v_ref, fq_ref, fk_ref, o_ref, kmax_ref, *, tile, blocks):
    hp = pl.program_id(1)
    first = pl.program_id(2) * blocks
    shape = (tile, LANES)
    half_tile = tile // 2
    causal = _causal(tile, tile, False)
    zero = jnp.zeros(shape, F32)
    every = range(blocks)

    @pl.when(first == 0)
    def _():
        width = min(NORM_CHUNK, k_ref.shape[2])

        def chunk(i, best):
            kc = k_ref[0, :, pl.ds(pl.multiple_of(i * width, width), width)].astype(F32)
            sq = kc * kc
            return (jnp.maximum(best[0], jnp.sum(sq[:HEAD_DIM], axis=0, keepdims=True)),
                    jnp.maximum(best[1], jnp.sum(sq[HEAD_DIM:], axis=0, keepdims=True)))

        none = jnp.zeros((1, width), F32)
        best = lax.fori_loop(0, k_ref.shape[2] // width, chunk, (none, none))
        kmax_ref[...] = _per_head(jnp.max(best[0], axis=1, keepdims=True),
                                  jnp.max(best[1], axis=1, keepdims=True), (1, LANES))

    kmax = jnp.sqrt(kmax_ref[...]) * NORM_SLACK
    qis = [first + j for j in every]
    qs, fqs, reach = [], [], []
    for j in every:
        rows = slice(j * tile, (j + 1) * tile)
        q = q_ref[0, rows, :]
        qs.append((_keep_head(q, True), _keep_head(q, False)))
        fq = fq_ref[0, rows, :]
        fqs.append((_head_column(fq, 2 * hp), _head_column(fq, 2 * hp + 1)))
        reach.append(jnp.sqrt(_head_norms_sq(q)) * kmax + _per_head(fqs[j][0], fqs[j][1], shape))

    def tile_left(j, i):
        return jnp.maximum(qis[j] - 1 - i, 0)

    def key_cols(kb):
        return pl.ds(pl.multiple_of(kb * tile, tile), tile)

    def last_fk(kb):
        fk = fk_ref[0, 0, kb]
        return _per_head(fk[0:1, tile - 1:tile], fk[1:2, tile - 1:tile], (1, LANES))

    def scores(j, kb):
        k = k_ref[0, :, key_cols(kb)]
        fk = fk_ref[0, 0, kb]
        return _dot(qs[j][0], k) - fk[0:1, :], _dot(qs[j][1], k) - fk[1:2, :]

    def diagonal_scores(j):
        k = k_ref[0, :, key_cols(qis[j])]
        fk = fk_ref[0, 0, qis[j]]
        raw = [(_dot(q, k[:, :half_tile]) - fk[r:r + 1, :half_tile],
                _dot(q[half_tile:], k[:, half_tile:]) - fk[r:r + 1, half_tile:]) for r, q in enumerate(qs[j])]
        unseen = jnp.full((half_tile, half_tile), NEG_BIG, F32)
        return [(jnp.where(causal[:, :half_tile], left, NEG_BIG),
                 jnp.concatenate([unseen, jnp.where(causal[half_tile:, half_tile:], right, NEG_BIG)], axis=0))
                for left, right in raw]

    def best_gap(j, kb, m0, m1):
        return jnp.max(reach[j] - _per_head(m0, m1, shape) - last_fk(kb))

    def max_live(s):
        alive = jnp.bool_(False)
        for j in every:
            alive = jnp.logical_or(alive, jnp.logical_and(s[0] < qis[j], s[1 + j] > 0.0))
        return alive

    def max_step(s, us=None):
        i, ms = s[0], s[1 + blocks:]
        if us is None:
            us = [scores(j, tile_left(j, i)) for j in every]
        gaps, new = (), ()
        for j in every:
            inside = i < qis[j]
            both = [jnp.where(inside, jnp.maximum(ms[2 * j + r], jnp.max(us[j][r], axis=1, keepdims=True) + fqs[j][r]),
                              ms[2 * j + r]) for r in range(2)]
            gaps += (best_gap(j, tile_left(j, i + 1), both[0], both[1]),)
            new += tuple(both)
        return (i + 1,) + gaps + new

    diagonal = [diagonal_scores(j) for j in every]
    ms = ()
    for j in every:
        ms += tuple(jnp.maximum(jnp.max(left, axis=1, keepdims=True), jnp.max(right, axis=1, keepdims=True)) + fq
                    for (left, right), fq in zip(diagonal[j], fqs[j]))
    gaps = tuple(best_gap(j, tile_left(j, 0), ms[2 * j], ms[2 * j + 1]) for j in every)
    nearest = [scores(j, tile_left(j, 0)) for j in every]
    ms = lax.while_loop(max_live, max_step, max_step((jnp.int32(0),) + gaps + ms, nearest))[1 + blocks:]

    cs = [(ms[2 * j] - fqs[j][0], ms[2 * j + 1] - fqs[j][1]) for j in every]
    gap_rows = [jnp.max(reach[j] - _per_head(ms[2 * j], ms[2 * j + 1], shape), axis=0, keepdims=True) for j in every]

    def head_gaps(j, kb):
        gap = gap_rows[j] - last_fk(kb)
        head0 = lax.broadcasted_iota(jnp.int32, gap.shape, 1) < HEAD_DIM
        return jnp.max(jnp.where(head0, gap, NEG_BIG)), jnp.max(jnp.where(head0, NEG_BIG, gap))

    def with_ones(v, head1):
        ones_rows = (lax.broadcasted_iota(jnp.int32, v.shape, 0) < HEAD_DIM) == head1
        return jnp.where(ones_rows, jnp.ones_like(v), v)

    blank = jnp.zeros((half_tile, LANES), F32)
    state = ()
    for j in every:
        v = v_ref[0, :, key_cols(qis[j])]
        weights = [(jnp.exp(left - c).astype(BF16), jnp.exp(right - c).astype(BF16))
                   for (left, right), c in zip(diagonal[j], cs[j])]
        accs = [_dot_nt(left, vh[:, :half_tile])
                + jnp.concatenate([blank, _dot_nt(right[half_tile:], vh[:, half_tile:])], axis=0)
                for (left, right), vh in zip(weights, (with_ones(v, False), with_ones(v, True)))]
        state += (jnp.int32(0),) + head_gaps(j, tile_left(j, 0)) + tuple(accs)

    def pair_active(s, j):
        done, gap0, gap1 = s[1 + 5 * j:4 + 5 * j]
        return jnp.logical_and(s[0] < qis[j], jnp.minimum(gap0, gap1) > EXP_ZERO)

    def pair_live(s):
        alive = jnp.bool_(False)
        for j in every:
            alive = jnp.logical_or(alive, pair_active(s, j))
        return alive

    def pair_step(s, us=None):
        i = s[0]
        kbs = [tile_left(j, i) for j in every]
        if us is None:
            us = [scores(j, kbs[j]) for j in every]
        ps =[[jnp.exp(u - c).astype(BF16) for u, c in zip(us[j], cs[j])] for j in every]
        out = (i + 1,)
        for j in every:
            done, gap0, gap1, acc0, acc1 = s[1 + 5 * j:6 + 5 * j]
            active = pair_active(s, j)
            v = v_ref[0, :, key_cols(kbs[j])]
            new0, new1 = head_gaps(j, tile_left(j, i + 1))
            out += (jnp.where(active, i + 1, done), jnp.where(active, new0, gap0), jnp.where(active, new1, gap1),
                    acc0 + jnp.where(active, _dot_nt(ps[j][0], with_ones(v, False)), 0.0),
                    acc1 + jnp.where(active, _dot_nt(ps[j][1], with_ones(v, True)), 0.0))
        return out

    state = lax.while_loop(pair_live, pair_step, pair_step((jnp.int32(0),) + state, nearest))[1:]

    seconds = [state[5 * j + 2] > state[5 * j + 1] for j in every]
    starts = [state[5 * j] for j in every]
    q_lone = [jnp.where(seconds[j], qs[j][1], qs[j][0]) for j in every]
    c_lone = [jnp.where(seconds[j], cs[j][1], cs[j][0]) for j in every]

    def lone_active(s, j):
        return jnp.logical_and(starts[j] + s[0] < qis[j], s[1 + 2 * j] > EXP_ZERO)

    def lone_live(s):
        alive = jnp.bool_(False)
        for j in every:
            alive = jnp.logical_or(alive, lone_active(s, j))
        return alive

    def lone_step(s):
        t = s[0]
        kbs = [tile_left(j, starts[j] + t) for j in every]
        us = []
        for j in every:
            k = k_ref[0, :, key_cols(kbs[j])]
            fk = fk_ref[0, 0, kbs[j]]
            fk_s = jnp.where(seconds[j], fk[1:2, :], fk[0:1, :])
            us.append([_dot(q_lone[j], k[:, lo:lo + half_tile]) - fk_s[:, lo:lo + half_tile] for lo in (0, half_tile)])
        ps = [[jnp.exp(u - c_lone[j]).astype(BF16) for u in us[j]] for j in every]
        out = (t + 1,)
        for j in every:
            gap, acc = s[1 + 2 * j:3 + 2 * j]
            active = lone_active(s, j)
            v_s = with_ones(v_ref[0, :, key_cols(kbs[j])], seconds[j])
            gaps = head_gaps(j, tile_left(j, starts[j] + t + 1))
            more = _dot_nt(ps[j][0], v_s[:, :half_tile]) + _dot_nt(ps[j][1], v_s[:, half_tile:])
            out += (jnp.where(active, jnp.where(seconds[j], gaps[1], gaps[0]), gap), acc + jnp.where(active, more, 0.0))
        return out

    lone = (jnp.int32(0),)
    for j in every:
        lone += (jnp.maximum(state[5 * j + 1], state[5 * j + 2]), zero)
    lone = lax.while_loop(lone_live, lone_step, lone)[1:]

    for j in every:
        acc0 = state[5 * j + 3] + jnp.where(seconds[j], 0.0, lone[2 * j + 1])
        acc1 = state[5 * j + 4] + jnp.where(seconds[j], lone[2 * j + 1], 0.0)
        o_ref[0, j * tile:(j + 1) * tile, :] = _softmax_out(acc0, acc1)


def _prompt_specs(t, tile):
    qspec = pl.BlockSpec((1, tile, LANES), lambda b, h, i: (b, i, h))
    kvspec = pl.BlockSpec((1, LANES, t), lambda b, h, i: (b, h, 0))
    return qspec, kvspec


def _sb_prompt(q, k, v):
    b, t, w = q.shape
    tile = ATT_BLOCK
    blocks = SB_BLOCKS_PER_STEP if t % (tile * SB_BLOCKS_PER_STEP) == 0 else 1
    qspec, kvspec = _prompt_specs(t, tile * blocks)
    return pl.pallas_call(
        functools.partial(_sb_prompt_kernel, tile=tile, blocks=blocks),
        grid=(b, w // LANES, t // (tile * blocks)),
        in_specs=[qspec, kvspec, kvspec],
        out_specs=qspec,
        out_shape=jax.ShapeDtypeStruct((b, t, w), F32),
        compiler_params=_params("arbitrary", "arbitrary", "arbitrary"),
        name="sb_prompt",
    )(q, k, v)


def _fox_prompt(q, k, v, fq, fk):
    b, t, w = q.shape
    tile = fk.shape[-1]
    blocks = FOX_BLOCKS_PER_STEP if t % (tile * FOX_BLOCKS_PER_STEP) == 0 else 1
    qspec, kvspec = _prompt_specs(t, tile * blocks)
    n_f = fq.shape[-1]
    return pl.pallas_call(
        functools.partial(_fox_prompt_kernel, tile=tile, blocks=blocks),
        grid=(b, w // LANES, t // (tile * blocks)),
        in_specs=[qspec, kvspec, kvspec,
                  pl.BlockSpec((1, tile * blocks, n_f), lambda b, h, i: (b, i, 0)),
                  pl.BlockSpec((1, 1, t // tile, 2, tile), lambda b, h, i: (b, h, 0, 0, 0))],
        out_specs=qspec,
        out_shape=jax.ShapeDtypeStruct((b, t, w), F32),
        scratch_shapes=[pltpu.VMEM((1, LANES), F32)],
        compiler_params=_params("arbitrary", "arbitrary", "arbitrary"),
        name="fox_prompt",
    )(q, k, v, fq, fk)


def _group_rows(g):
    return slice(g * LANES, (g + 1) * LANES)


def _sb_sample_kernel(q_ref, kn_ref, vn_ref, kc_ref, vc_ref, o_ref, *, tile):
    n, width = q_ref.shape[1:]
    groups = width // LANES
    past = kc_ref.shape[2]
    qs = []
    for g in range(groups):
        q = q_ref[0, :, _group_rows(g)]
        qs.append((_keep_head(q, True), _keep_head(q, False)))

    def visit(kbs, state, mask):
        kvs = []
        for g in range(groups):
            if mask is not None:
                kvs.append((kn_ref[0, _group_rows(g), :], vn_ref[0, _group_rows(g), :]))
            else:
                cols = pl.ds(pl.multiple_of(kbs[g] * tile, tile), tile)
                kvs.append((kc_ref[0, _group_rows(g), cols].astype(BF16),
                            vc_ref[0, _group_rows(g), cols].astype(BF16)))
        return _sb_visit(qs, kvs, state, mask)

    state = visit(None, _sb_start(n, groups), _causal(n, n, True))
    state = _sb_sweep(visit, [past // tile] * groups, state)
    for g in range(groups):
        o_ref[0, :, _group_rows(g)] = state[2 + 3 * g]


def _fox_sample_kernel(q_ref, kn_ref, vn_ref, kc_ref, vc_ref, fq_ref, fk_ref, o_ref):
    n, width = q_ref.shape[1:]
    groups = width // LANES
    past = kc_ref.shape[2]
    fq = fq_ref[0]
    causal = _causal(n, n, False)
    scored = []
    for g in range(groups):
        q = q_ref[0, :, _group_rows(g)]
        kn = kn_ref[0, _group_rows(g), :]
        kc = kc_ref[0, _group_rows(g), :].astype(BF16)
        for h in range(2):
            qh = _keep_head(q, h == 0)
            fk = fk_ref[0, g, h:h + 1, :]
            u_new = jnp.where(causal, _dot(qh, kn) - fk[:, past:past + n], NEG_BIG)
            u_old = _dot(qh, kc) - fk[:, 0:past]
            scored.append((u_new, u_old, fq[:, 2 * g + h:2 * g + h + 1]))
    weights = []
    for u_new, u_old, fq_col in scored:
        m = jnp.maximum(jnp.max(u_new, axis=1, keepdims=True), jnp.max(u_old, axis=1, keepdims=True)) + fq_col
        c = m - fq_col
        weights.append((jnp.exp(u_new - c).astype(BF16), jnp.exp(u_old - c).astype(BF16)))
    for g in range(groups):
        vn = vn_ref[0, _group_rows(g), :]
        vc = vc_ref[0, _group_rows(g), :].astype(BF16)
        accs = []
        for h in range(2):
            p_new, p_old = weights[2 * g + h]
            accs.append(_dot_nt(p_new, _keep_head(vn, h == 0, 0, 1.0))
                        + _dot_nt(p_old, _keep_head(vc, h == 0, 0, 1.0)))
        o_ref[0, :, _group_rows(g)] = _softmax_out(accs[0], accs[1])


def _sample_specs(n, width, past):
    qspec = pl.BlockSpec((1, n, width), lambda b: (b, 0, 0))
    new = pl.BlockSpec((1, width, n), lambda b: (b, 0, 0))
    cache = pl.BlockSpec((1, width, past), lambda b: (b, 0, 0))
    return qspec, new, cache


def _sb_sample(q, kn, vn, kc, vc):
    b, n, w = q.shape
    qspec, new, cache = _sample_specs(n, w, kc.shape[2])
    return pl.pallas_call(
        functools.partial(_sb_sample_kernel, tile=ATT_BLOCK),
        grid=(b,),
        in_specs=[qspec, new, new, cache, cache],
        out_specs=qspec,
        out_shape=jax.ShapeDtypeStruct((b, n, w), F32),
        compiler_params=_params("arbitrary"),
        name="sb_sample",
    )(q, kn, vn, kc, vc)


def _fox_sample(q, kn, vn, kc, vc, fq, fk):
    b, n, w = q.shape
    qspec, new, cache = _sample_specs(n, w, kc.shape[2])
    return pl.pallas_call(
        _fox_sample_kernel,
        grid=(b,),
        in_specs=[qspec, new, new, cache, cache,
                  pl.BlockSpec((1, n, fq.shape[-1]), lambda b: (b, 0, 0)),
                  pl.BlockSpec((1,) + fk.shape[1:], lambda b: (b, 0, 0, 0))],
        out_specs=qspec,
        out_shape=jax.ShapeDtypeStruct((b, n, w), F32),
        compiler_params=_params("arbitrary"),
        name="fox_sample",
    )(q, kn, vn, kc, vc, fq, fk)


def _post_kernel(x_ref, osb_ref, ofx_ref, ada_ref, gsb_ref, gfx_ref, wo_ref, gffn_ref,
                 wg_ref, wu_ref, wd_ref, gfin_ref, y_ref, *, final_norm):
    nb, tt, d = x_ref.shape
    rows = nb * tt
    ada = ada_ref[...]
    o_sb = _rms(osb_ref[...], gsb_ref[...])
    o_fx = _rms(ofx_ref[...], gfx_ref[...])
    sb_w, fx_w = o_sb.shape[-1], o_fx.shape[-1]
    proj = (_dot(o_sb.reshape(rows, sb_w).astype(BF16), wo_ref[0:sb_w, :])
            + _dot(o_fx.reshape(rows, fx_w).astype(BF16), wo_ref[sb_w:sb_w + fx_w, :]))
    x2 = x_ref[...] + (1.0 + ada[:, 2:3, :]) * proj.reshape(nb, tt, d)
    h = _rms(x2, gffn_ref[...]) * (1.0 + ada[:, 4:5, :]) + ada[:, 3:4, :]
    h = h.reshape(rows, d).astype(BF16)
    d_ff = wg_ref.shape[1]
    f = None
    for lo in range(0, d_ff, FF_CHUNK):
        hi = min(lo + FF_CHUNK, d_ff)
        g = _dot(h, wg_ref[:, lo:hi])
        u = _dot(h, wu_ref[:, lo:hi])
        act = (g * (1.0 / (1.0 + jnp.exp(-g))) * u).astype(BF16)
        part = _dot(act, wd_ref[lo:hi, :])
        f = part if f is None else f + part
    x3 = x2 + (1.0 + ada[:, 5:6, :]) * f.reshape(nb, tt, d)
    y_ref[...] = _rms(x3, gfin_ref[...]) if final_norm else x3


def _post(x, o_sb, o_fx, ada, g_sb, g_fx, wo16, g_ffn, wg16, wu16, wd16, g_fin, final_norm):
    b, t, d = x.shape
    tt = min(t, ROW_BLOCK)
    nb = max(1, ROW_BLOCK // tt)
    tok = lambda w: pl.BlockSpec((nb, tt, w), lambda i, j: (i, j, 0))
    const = lambda a: pl.BlockSpec(a.shape, lambda i, j: (0, 0), pipeline_mode=pl.Buffered(1))
    return pl.pallas_call(
        functools.partial(_post_kernel, final_norm=final_norm),
        grid=(b // nb, t // tt),
        in_specs=[tok(d), tok(o_sb.shape[-1]), tok(o_fx.shape[-1]),
                  pl.BlockSpec((nb, 6, d), lambda i, j: (i, 0, 0)),
                  const(g_sb), const(g_fx), const(wo16), const(g_ffn),
                  const(wg16), const(wu16), const(wd16), const(g_fin)],
        out_specs=tok(d),
        out_shape=jax.ShapeDtypeStruct((b, t, d), F32),
        compiler_params=_params("arbitrary", "arbitrary"),
        name="post_mixer_ffn",
    )(x, o_sb, o_fx, ada, g_sb, g_fx, wo16, g_ffn, wg16, wu16, wd16, g_fin)


def _pad_lanes(a, n):
    return jnp.pad(a, ((0, 0), (0, n - a.shape[1])))


def kernel(x_prompt, x_sample, c_prompt, c_sample, cache_sb_k, cache_sb_v, cache_fox_k, cache_fox_v, cache_fox_logf, w_ada, b_ada, g_mix, w_in, b_f, g_sb_out, g_fox_out, w_o, g_ffn, w_gate, w_up, w_down, g_final):
    depth = w_ada.shape[0]
    bp, tp, d = x_prompt.shape
    bs, ts, _ = x_sample.shape
    past, n_sb, hd = cache_sb_k.shape[2:]
    n_fx = cache_fox_k.shape[3]
    assert hd == HEAD_DIM and n_sb % 2 == 0 and n_fx % 2 == 0
    assert tp % ROW_BLOCK == 0 and tp % ATT_BLOCK == 0 and ROW_BLOCK % ts == 0 and bs % (ROW_BLOCK // ts) == 0
    assert past % ATT_BLOCK == 0
    sb_w, fx_w = n_sb * hd, n_fx * hd
    att = min(FOX_BLOCK, tp)
    past_pad = -(-(past + ts) // (3 * LANES)) * (3 * LANES)
    row = lambda a: a.reshape(1, -1)

    xp, xs = x_prompt, x_sample
    outs = [[] for _ in range(10)]
    for l in range(depth):
        in_cols = w_in.shape[2]
        w16t = jnp.pad(jnp.swapaxes(w_in[l], 0, 1), ((0, -in_cols % BF16_ROWS), (0, 0))).astype(BF16)
        bfr, bfc = row(b_f[l]), b_f[l].reshape(-1, 1)
        wo16, wg16 = w_o[l].astype(BF16), w_gate[l].astype(BF16)
        wu16, wd16 = w_up[l].astype(BF16), w_down[l].astype(BF16)

        c_all = jnp.concatenate([c_prompt, c_sample], axis=0)
        ada = _ada(c_all, w_ada[l], row(b_ada[l])).reshape(bp + bs, 6, d)
        ada_p, ada_s = ada[:bp], ada[bp:]

        (q_sb, k_sb, v_sb, k_sb16, v_sb16, q_fx, k_fx, v_fx, k_fx16, v_fx16, lf) = _pre(
            xp, ada_p, row(g_mix[l]), w16t, bfr, bfc, sb_w, fx_w, n_fx, True)
        o_sb = _sb_prompt(q_sb, k_sb16, v_sb16)
        f_cum = _cumsum_rows(lf.reshape(bp * n_fx, tp), ROW_BLOCK)
        fq = jnp.swapaxes(f_cum.reshape(bp, n_fx, tp), 1, 2)
        fk = jnp.swapaxes(f_cum.reshape(bp, n_fx // 2, 2, tp // att, att), 2, 3)
        o_fx = _fox_prompt(q_fx, k_fx16, v_fx16, fq, fk)
        xp = _post(xp, o_sb, o_fx, ada_p, row(g_sb_out[l]), row(g_fox_out[l]), wo16, row(g_ffn[l]),
                   wg16, wu16, wd16, row(g_final), l + 1 == depth)
        for dst, a in zip(outs[0:4], (k_sb, v_sb, k_fx, v_fx)):
            dst.append(jnp.swapaxes(a, 1, 2).reshape(bp, tp, -1, hd))
        outs[4].append(jnp.swapaxes(lf, 1, 2))

        (q_sb, k_sb, v_sb, k_sb16, v_sb16, q_fx, k_fx, v_fx, k_fx16, v_fx16, lf) = _pre(
            xs, ada_s, row(g_mix[l]), w16t, bfr, bfc, sb_w, fx_w, n_fx, False)
        time_minor = lambda a: jnp.swapaxes(a.reshape(a.shape[0], a.shape[1], -1), 1, 2)
        o_sb = _sb_sample(q_sb, time_minor(k_sb16), time_minor(v_sb16),
                          time_minor(cache_sb_k[l]), time_minor(cache_sb_v[l]))
        lf_all = jnp.concatenate([cache_fox_logf[l].astype(F32), lf], axis=1)
        lf_rows = _pad_lanes(jnp.swapaxes(lf_all, 1, 2).reshape(bs * n_fx, past + ts), past_pad)
        f_all = _cumsum_rows(lf_rows, past_pad // 3)
        fq = jnp.swapaxes(f_all.reshape(bs, n_fx, past_pad)[:, :, past:past + ts], 1, 2)
        fk = f_all.reshape(bs, n_fx // 2, 2, past_pad)
        o_fx = _fox_sample(q_fx, time_minor(k_fx16), time_minor(v_fx16),
                           time_minor(cache_fox_k[l]), time_minor(cache_fox_v[l]), fq, fk)
        xs = _post(xs, o_sb, o_fx, ada_s, row(g_sb_out[l]), row(g_fox_out[l]), wo16, row(g_ffn[l]),
                   wg16, wu16, wd16, row(g_final), l + 1 == depth)
        for dst, a in zip(outs[5:9], (k_sb, v_sb, k_fx, v_fx)):
            dst.append(a.reshape(bs, ts, -1, hd))
        outs[9].append(lf)

    return (xp, xs) + tuple(jnp.stack(o) for o in outs)
```

```python
import functools

import jax
import jax.numpy as jnp
from jax import lax
from jax.experimental import pallas as pl
from jax.experimental.pallas import tpu as pltpu

F32 = jnp.float32
BF16 = jnp.bfloat16
EPS = 1e-6
HEAD_DIM = 64
LANES = 128
BF16_ROWS = 16
ROW_BLOCK = 512
FF_CHUNK = 512
ATT_BLOCK = 256
SB_BLOCKS_PER_STEP = 8
FOX_BLOCKS_PER_STEP = 2
FOX_BLOCK = 512
NEG_BIG = -1e30
EXP_ZERO = -105.0
NORM_CHUNK = 2048
NORM_SLACK = 1.0 + 2.0 ** -8
VMEM_LIMIT = 56 * 1024 * 1024


def _params(*semantics):
    return pltpu.CompilerParams(dimension_semantics=semantics, vmem_limit_bytes=VMEM_LIMIT)


def _dot(a, b):
    return jnp.dot(a, b, preferred_element_type=F32)


def _dot_nt(a, b):
    return lax.dot_general(a, b, (((1,), (1,)), ((), ())), preferred_element_type=F32)


def _rms(x, g):
    return x * lax.rsqrt(jnp.mean(x * x, axis=-1, keepdims=True) + EPS) * g


def _split_bf16(x, parts):
    out = []
    for _ in range(parts - 1):
        p = x.astype(BF16)
        out.append(p)
        x = x - p.astype(F32)
    out.append(x.astype(BF16))
    return out


def _dot_split(x, m, parts):
    acc = None
    for p in _split_bf16(x, parts):
        d = _dot(p, m)
        acc = d if acc is None else acc + d
    return acc


def _keep_head(x, first, axis=1, fill=0.0):
    pos = lax.broadcasted_iota(jnp.int32, x.shape, axis)
    keep = (pos < HEAD_DIM) if first else (pos >= HEAD_DIM)
    return jnp.where(keep, x, jnp.full_like(x, fill))


def _per_head(col0, col1, shape):
    lane = lax.broadcasted_iota(jnp.int32, shape, 1)
    return jnp.where(lane < HEAD_DIM, col0, col1)


def _softmax_out(acc0, acc1):
    first = lax.broadcasted_iota(jnp.int32, acc0.shape, 1) < HEAD_DIM
    return jnp.where(first, acc0, acc1) / pltpu.roll(jnp.where(first, acc1, acc0), LANES // 2, 1)


def _lanes_to(x, n):
    if n <= LANES:
        return x[:, :n]
    return jnp.concatenate([x] * (n // LANES), axis=1)


def _ada_kernel(c_ref, w_ref, b_ref, o_ref):
    c = c_ref[...]
    s = (c * (1.0 / (1.0 + jnp.exp(-c)))).astype(BF16)
    o_ref[...] = _dot(s, w_ref[...].astype(BF16)) + b_ref[...]


def _ada(c, w, b):
    rows, d = c.shape
    n = w.shape[1]
    tn = 1024
    return pl.pallas_call(
        _ada_kernel,
        grid=(n // tn,),
        in_specs=[pl.BlockSpec((rows, d), lambda j: (0, 0)),
                  pl.BlockSpec((d, tn), lambda j: (0, j)),
                  pl.BlockSpec((1, tn), lambda j: (0, j))],
        out_specs=pl.BlockSpec((rows, tn), lambda j: (0, j)),
        out_shape=jax.ShapeDtypeStruct((rows, n), F32),
        compiler_params=_params("arbitrary"),
        name="ada",
    )(c, w, b)


def _log_sigmoid(u):
    return jnp.minimum(u, 0.0) - jnp.log(1.0 + jnp.exp(-jnp.abs(u)))


def _pre_kernel(x_ref, ada_ref, g_ref, wt_ref, bf_ref, bfc_ref,
                qsb_ref, ksb_ref, vsb_ref, ksb16_ref, vsb16_ref,
                qfx_ref, kfx_ref, vfx_ref, kfx16_ref, vfx16_ref, lf_ref, *, sb_w, fx_w, n_f, time_minor):
    x = x_ref[...]
    nb, tt, d = x.shape
    ada = ada_ref[...]
    h = _rms(x, g_ref[...]) * (1.0 + ada[:, 1:2, :]) + ada[:, 0:1, :]
    h = h.reshape(nb * tt, d).astype(BF16)
    q_scale = HEAD_DIM ** -0.5

    def proj(lo, width):
        return _dot_nt(h, wt_ref[lo:lo + width, :]).reshape(nb, tt, width)

    base = 3 * sb_w
    qsb_ref[...] = (proj(0, sb_w) * q_scale).astype(BF16)
    qfx_ref[...] = (proj(base, fx_w) * q_scale).astype(BF16)
    if time_minor:
        sb = _dot_nt(wt_ref[sb_w:base, :], h)
        fx = _dot_nt(wt_ref[base + fx_w:, :], h)
        parts = (sb[:sb_w], sb[sb_w:], fx[:fx_w], fx[fx_w:2 * fx_w])
        lf_ref[...] = _log_sigmoid(fx[2 * fx_w:2 * fx_w + n_f] + bfc_ref[...])[None]
    else:
        parts = (proj(sb_w, sb_w), proj(2 * sb_w, sb_w), proj(base + fx_w, fx_w), proj(base + 2 * fx_w, fx_w))
        forget = _dot_nt(h, wt_ref[base + 3 * fx_w:, :])
        lf_ref[...] = _log_sigmoid(forget[:, :n_f] + bf_ref[...]).reshape(nb, tt, n_f)
    for r, out, out16 in zip(parts, (ksb_ref, vsb_ref, kfx_ref, vfx_ref),
                             (ksb16_ref, vsb16_ref, kfx16_ref, vfx16_ref)):
        r = r[None] if time_minor else r
        out[...] = r
        out16[...] = r.astype(BF16)


def _pre(x, ada, g, w16t, bfr, bfc, sb_w, fx_w, n_f, time_minor):
    b, t, d = x.shape
    tt = min(t, ROW_BLOCK)
    nb = max(1, ROW_BLOCK // tt)
    assert nb == 1 or not time_minor
    grid = (b // nb, t // tt)
    tok = lambda w: pl.BlockSpec((nb, tt, w), lambda i, j: (i, j, 0))
    const = lambda a: pl.BlockSpec(a.shape, lambda i, j: (0, 0))
    shape = lambda w, dt: jax.ShapeDtypeStruct((b, t, w), dt)
    if time_minor:
        kv = lambda w: pl.BlockSpec((1, w, tt), lambda i, j: (i, 0, j))
        kv_shape = lambda w, dt: jax.ShapeDtypeStruct((b, w, t), dt)
    else:
        kv, kv_shape = tok, shape
    return pl.pallas_call(
        functools.partial(_pre_kernel, sb_w=sb_w, fx_w=fx_w, n_f=n_f, time_minor=time_minor),
        grid=grid,
        in_specs=[tok(d), pl.BlockSpec((nb, 6, d), lambda i, j: (i, 0, 0)),
                  const(g), const(w16t), const(bfr), const(bfc)],
        out_specs=[tok(sb_w)] + [kv(sb_w)] * 4 + [tok(fx_w)] + [kv(fx_w)] * 4 + [kv(n_f)],
        out_shape=[shape(sb_w, BF16), kv_shape(sb_w, F32), kv_shape(sb_w, F32), kv_shape(sb_w, BF16),
                   kv_shape(sb_w, BF16),
                   shape(fx_w, BF16), kv_shape(fx_w, F32), kv_shape(fx_w, F32), kv_shape(fx_w, BF16),
                   kv_shape(fx_w, BF16),
                   kv_shape(n_f, F32)],
        compiler_params=_params("arbitrary", "arbitrary"),
        name="pre_mixer",
    )(x, ada, g, w16t, bfr, bfc)


def _cumsum_kernel(x_ref, o_ref, *, tc):
    r = lax.broadcasted_iota(jnp.int32, (tc, tc), 0)
    c = lax.broadcasted_iota(jnp.int32, (tc, tc), 1)
    upper = jnp.where(r <= c, 1.0, 0.0).astype(BF16)
    ones = jnp.ones((tc, LANES), BF16)
    carry = jnp.zeros((x_ref.shape[0], LANES), F32)
    for lo in range(0, x_ref.shape[1], tc):
        x = x_ref[:, lo:lo + tc]
        o_ref[:, lo:lo + tc] = _dot_split(x, upper, 3) + _lanes_to(carry, tc)
        carry = carry + _dot_split(x, ones, 3)


def _cumsum_rows(x, tc):
    return pl.pallas_call(
        functools.partial(_cumsum_kernel, tc=tc),
        out_shape=jax.ShapeDtypeStruct(x.shape, F32),
        compiler_params=pltpu.CompilerParams(vmem_limit_bytes=VMEM_LIMIT),
        name="cumsum_time",
    )(x)


def _tri_strict(n):
    r = lax.broadcasted_iota(jnp.int32, (n, n), 0)
    c = lax.broadcasted_iota(jnp.int32, (n, n), 1)
    return jnp.where(r > c, 1.0, 0.0).astype(BF16)


def _causal(tq, tk, strict):
    r = lax.broadcasted_iota(jnp.int32, (tq, tk), 0)
    c = lax.broadcasted_iota(jnp.int32, (tq, tk), 1)
    return (c < r) if strict else (c <= r)


def _sb_tiles(chains, mask):
    tri = _tri_strict(chains[0][1].shape[1])
    zs = [_dot(q, k) for q, k, _, _ in chains]
    staged = []
    for z in zs:
        log_beta = jnp.minimum(z, 0.0) - jnp.log(1.0 + jnp.exp(-jnp.abs(z)))
        log_keep = log_beta - z
        if mask is not None:
            log_keep = jnp.where(mask, log_keep, 0.0)
        staged.append((log_beta, log_keep[:, 0:1], log_keep.astype(BF16)))
    betweens = [_dot(terms, tri) for _, _, terms in staged]
    weights = []
    for (log_beta, _, _), between, (_, _, _, carry) in zip(staged, betweens, chains):
        a = jnp.exp(log_beta + between + carry)
        if mask is not None:
            a = jnp.where(mask, a, 0.0)
        weights.append(a.astype(BF16))
    pvs = [_dot_nt(a, v) for a, (_, _, v, _) in zip(weights, chains)]
    return [(pv, carry + (between[:, 0:1] + first))
            for pv, between, (_, first, _), (_, _, _, carry) in zip(pvs, betweens, staged, chains)]


def _sb_start(rows, groups):
    col = jnp.zeros((rows, 1), F32)
    return (col, col, jnp.zeros((rows, LANES), F32)) * groups


def _sb_visit(qs, kvs, state, mask):
    chains = []
    for g, ((q0, q1), (k, v)) in enumerate(zip(qs, kvs)):
        chains.append((q0, k, _keep_head(v, True, 0), state[3 * g]))
        chains.append((q1, k, _keep_head(v, False, 0), state[3 * g + 1]))
    res = _sb_tiles(chains, mask)
    out = ()
    for g in range(len(qs)):
        (pv0, c0), (pv1, c1) = res[2 * g], res[2 * g + 1]
        out += (c0, c1, state[3 * g + 2] + pv0 + pv1)
    return out


def _sb_sweep(visit, lefts, state):
    groups = len(lefts)

    def live(s):
        alive = jnp.bool_(False)
        for g in range(groups):
            more = jnp.max(jnp.maximum(s[1 + 3 * g], s[2 + 3 * g])) > EXP_ZERO
            alive = jnp.logical_or(alive, jnp.logical_and(s[0] < lefts[g], more))
        return alive

    def step(s):
        i, old = s[0], s[1:]
        new = visit([jnp.maximum(lefts[g] - 1 - i, 0) for g in range(groups)], old, None)
        out = (i + 1,)
        for g in range(groups):
            inside = i < lefts[g]
            out += tuple(jnp.where(inside, n, o) for n, o in zip(new[3 * g:3 * g + 3], old[3 * g:3 * g + 3]))
        return out

    return lax.while_loop(live, step, (jnp.int32(0),) + tuple(state))[1:]


def _head_column(f, h):
    lane = lax.broadcasted_iota(jnp.int32, f.shape, 1)
    return jnp.sum(jnp.where(lane == h, f, 0.0), axis=1, keepdims=True)


def _sb_prompt_kernel(q_ref, k_ref, v_ref, o_ref, *, tile, blocks):
    first = pl.program_id(2) * blocks
    qs = []
    for j in range(blocks):
        q = q_ref[0, j * tile:(j + 1) * tile, :]
        qs.append((_keep_head(q, True), _keep_head(q, False)))

    def visit(kbs, state, mask):
        kvs = []
        for j in range(blocks):
            start = pl.multiple_of(kbs[j] * tile, tile)
            kvs.append((k_ref[0, :, pl.ds(start, tile)], v_ref[0, :, pl.ds(start, tile)]))
        return _sb_visit(qs, kvs, state, mask)

    lefts = [first + j for j in range(blocks)]
    state = visit(lefts, _sb_start(tile, blocks), _causal(tile, tile, True))
    state = _sb_sweep(visit, lefts, state)
    for j in range(blocks):
        o_ref[0, j * tile:(j + 1) * tile, :] = state[2 + 3 * j]


def _head_norms_sq(x16):
    x = x16.astype(F32)
    r = lax.broadcasted_iota(jnp.int32, (LANES, LANES), 0)
    c = lax.broadcasted_iota(jnp.int32, (LANES, LANES), 1)
    same_head = jnp.where((r < HEAD_DIM) == (c < HEAD_DIM), 1.0, 0.0).astype(BF16)
    return _dot_split(x * x, same_head, 2)


def _fox_prompt_kernel(q_ref, k_ref, v_ref, fq_ref, fk_ref, o_ref, kmax_ref, *, tile, blocks):
    hp = pl.program_id(1)
    first = pl.program_id(2) * blocks
    shape = (tile, LANES)
    half_tile = tile // 2
    causal = _causal(tile, tile, False)
    zero = jnp.zeros(shape, F32)
    every = range(blocks)

    @pl.when(first == 0)
    def _():
        width = min(NORM_CHUNK, k_ref.shape[2])

        def chunk(i, best):
            kc = k_ref[0, :, pl.ds(pl.multiple_of(i * width, width), width)].astype(F32)
            sq = kc * kc
            return (jnp.maximum(best[0], jnp.sum(sq[:HEAD_DIM], axis=0, keepdims=True)),
                    jnp.maximum(best[1], jnp.sum(sq[HEAD_DIM:], axis=0, keepdims=True)))

        none = jnp.zeros((1, width), F32)
        best = lax.fori_loop(0, k_ref.shape[2] // width, chunk, (none, none))
        kmax_ref[...] = _per_head(jnp.max(best[0], axis=1, keepdims=True),
                                  jnp.max(best[1], axis=1, keepdims=True), (1, LANES))

    kmax = jnp.sqrt(kmax_ref[...]) * NORM_SLACK
    qis = [first + j for j in every]
    qs, fqs, reach = [], [], []
    for j in every:
        rows = slice(j * tile, (j + 1) * tile)
        q = q_ref[0, rows, :]
        qs.append((_keep_head(q, True), _keep_head(q, False)))
        fq = fq_ref[0, rows, :]
        fqs.append((_head_column(fq, 2 * hp), _head_column(fq, 2 * hp + 1)))
        reach.append(jnp.sqrt(_head_norms_sq(q)) * kmax + _per_head(fqs[j][0], fqs[j][1], shape))

    def tile_left(j, i):
        return jnp.maximum(qis[j] - 1 - i, 0)

    def key_cols(kb):
        return pl.ds(pl.multiple_of(kb * tile, tile), tile)

    def last_fk(kb):
        fk = fk_ref[0, 0, kb]
        return _per_head(fk[0:1, tile - 1:tile], fk[1:2, tile - 1:tile], (1, LANES))

    def scores(j, kb):
        k = k_ref[0, :, key_cols(kb)]
        fk = fk_ref[0, 0, kb]
        return _dot(qs[j][0], k) - fk[0:1, :], _dot(qs[j][1], k) - fk[1:2, :]

    def diagonal_scores(j):
        k = k_ref[0, :, key_cols(qis[j])]
        fk = fk_ref[0, 0, qis[j]]
        raw = [(_dot(q, k[:, :half_tile]) - fk[r:r + 1, :half_tile],
                _dot(q[half_tile:], k[:, half_tile:]) - fk[r:r + 1, half_tile:]) for r, q in enumerate(qs[j])]
        unseen = jnp.full((half_tile, half_tile), NEG_BIG, F32)
        return [(jnp.where(causal[:, :half_tile], left, NEG_BIG),
                 jnp.concatenate([unseen, jnp.where(causal[half_tile:, half_tile:], right, NEG_BIG)], axis=0))
                for left, right in raw]

    def best_gap(j, kb, m0, m1):
        return jnp.max(reach[j] - _per_head(m0, m1, shape) - last_fk(kb))

    def max_live(s):
        alive = jnp.bool_(False)
        for j in every:
            alive = jnp.logical_or(alive, jnp.logical_and(s[0] < qis[j], s[1 + j] > 0.0))
        return alive

    def max_step(s, us=None):
        i, ms = s[0], s[1 + blocks:]
        if us is None:
            us = [scores(j, tile_left(j, i)) for j in every]
        gaps, new = (), ()
        for j in every:
            inside = i < qis[j]
            both = [jnp.where(inside, jnp.maximum(ms[2 * j + r], jnp.max(us[j][r], axis=1, keepdims=True) + fqs[j][r]),
                              ms[2 * j + r]) for r in range(2)]
            gaps += (best_gap(j, tile_left(j, i + 1), both[0], both[1]),)
            new += tuple(both)
        return (i + 1,) + gaps + new

    diagonal = [diagonal_scores(j) for j in every]
    ms = ()
    for j in every:
        ms += tuple(jnp.maximum(jnp.max(left, axis=1, keepdims=True), jnp.max(right, axis=1, keepdims=True)) + fq
                    for (left, right), fq in zip(diagonal[j], fqs[j]))
    gaps = tuple(best_gap(j, tile_left(j, 0), ms[2 * j], ms[2 * j + 1]) for j in every)
    nearest = [scores(j, tile_left(j, 0)) for j in every]
    ms = lax.while_loop(max_live, max_step, max_step((jnp.int32(0),) + gaps + ms, nearest))[1 + blocks:]

    cs = [(ms[2 * j] - fqs[j][0], ms[2 * j + 1] - fqs[j][1]) for j in every]
    gap_rows = [jnp.max(reach[j] - _per_head(ms[2 * j], ms[2 * j + 1], shape), axis=0, keepdims=True) for j in every]

    def head_gaps(j, kb):
        gap = gap_rows[j] - last_fk(kb)
        head0 = lax.broadcasted_iota(jnp.int32, gap.shape, 1) < HEAD_DIM
        return jnp.max(jnp.where(head0, gap, NEG_BIG)), jnp.max(jnp.where(head0, NEG_BIG, gap))

    def with_ones(v, head1):
        ones_rows = (lax.broadcasted_iota(jnp.int32, v.shape, 0) < HEAD_DIM) == head1
        return jnp.where(ones_rows, jnp.ones_like(v), v)

    blank = jnp.zeros((half_tile, LANES), F32)
    state = ()
    for j in every:
        v = v_ref[0, :, key_cols(qis[j])]
        weights = [(jnp.exp(left - c).astype(BF16), jnp.exp(right - c).astype(BF16))
                   for (left, right), c in zip(diagonal[j], cs[j])]
        accs = [_dot_nt(left, vh[:, :half_tile])
                + jnp.concatenate([blank, _dot_nt(right[half_tile:], vh[:, half_tile:])], axis=0)
                for (left, right), vh in zip(weights, (with_ones(v, False), with_ones(v, True)))]
        state += (jnp.int32(0),) + head_gaps(j, tile_left(j, 0)) + tuple(accs)

    def pair_active(s, j):
        done, gap0, gap1 = s[1 + 5 * j:4 + 5 * j]
        return jnp.logical_and(s[0] < qis[j], jnp.minimum(gap0, gap1) > EXP_ZERO)

    def pair_live(s):
        alive = jnp.bool_(False)
        for j in every:
            alive = jnp.logical_or(alive, pair_active(s, j))
        return alive

    def pair_step(s, us=None):
        i = s[0]
        kbs = [tile_left(j, i) for j in every]
        if us is None:
            us = [scores(j, kbs[j]) for j in every]
        ps =[[jnp.exp(u - c).astype(BF16) for u, c in zip(us[j], cs[j])] for j in every]
        out = (i + 1,)
        for j in every:
            done, gap0, gap1, acc0, acc1 = s[1 + 5 * j:6 + 5 * j]
            active = pair_active(s, j)
            v = v_ref[0, :, key_cols(kbs[j])]
            new0, new1 = head_gaps(j, tile_left(j, i + 1))
            out += (jnp.where(active, i + 1, done), jnp.where(active, new0, gap0), jnp.where(active, new1, gap1),
                    acc0 + jnp.where(active, _dot_nt(ps[j][0], with_ones(v, False)), 0.0),
                    acc1 + jnp.where(active, _dot_nt(ps[j][1], with_ones(v, True)), 0.0))
        return out

    state = lax.while_loop(pair_live, pair_step, pair_step((jnp.int32(0),) + state, nearest))[1:]

    seconds = [state[5 * j + 2] > state[5 * j + 1] for j in every]
    starts = [state[5 * j] for j in every]
    q_lone = [jnp.where(seconds[j], qs[j][1], qs[j][0]) for j in every]
    c_lone = [jnp.where(seconds[j], cs[j][1], cs[j][0]) for j in every]

    def lone_active(s, j):
        return jnp.logical_and(starts[j] + s[0] < qis[j], s[1 + 2 * j] > EXP_ZERO)

    def lone_live(s):
        alive = jnp.bool_(False)
        for j in every:
            alive = jnp.logical_or(alive, lone_active(s, j))
        return alive

    def lone_step(s):
        t = s[0]
        kbs = [tile_left(j, starts[j] + t) for j in every]
        us = []
        for j in every:
            k = k_ref[0, :, key_cols(kbs[j])]
            fk = fk_ref[0, 0, kbs[j]]
            fk_s = jnp.where(seconds[j], fk[1:2, :], fk[0:1, :])
            us.append([_dot(q_lone[j], k[:, lo:lo + half_tile]) - fk_s[:, lo:lo + half_tile] for lo in (0, half_tile)])
        ps = [[jnp.exp(u - c_lone[j]).astype(BF16) for u in us[j]] for j in every]
        out = (t + 1,)
        for j in every:
            gap, acc = s[1 + 2 * j:3 + 2 * j]
            active = lone_active(s, j)
            v_s = with_ones(v_ref[0, :, key_cols(kbs[j])], seconds[j])
            gaps = head_gaps(j, tile_left(j, starts[j] + t + 1))
            more = _dot_nt(ps[j][0], v_s[:, :half_tile]) + _dot_nt(ps[j][1], v_s[:, half_tile:])
            out += (jnp.where(active, jnp.where(seconds[j], gaps[1], gaps[0]), gap), acc + jnp.where(active, more, 0.0))
        return out

    lone = (jnp.int32(0),)
    for j in every:
        lone += (jnp.maximum(state[5 * j + 1], state[5 * j + 2]), zero)
    lone = lax.while_loop(lone_live, lone_step, lone)[1:]

    for j in every:
        acc0 = state[5 * j + 3] + jnp.where(seconds[j], 0.0, lone[2 * j + 1])
        acc1 = state[5 * j + 4] + jnp.where(seconds[j], lone[2 * j + 1], 0.0)
        o_ref[0, j * tile:(j + 1) * tile, :] = _softmax_out(acc0, acc1)


def _prompt_specs(t, tile):
    qspec = pl.BlockSpec((1, tile, LANES), lambda b, h, i: (b, i, h))
    kvspec = pl.BlockSpec((1, LANES, t), lambda b, h, i: (b, h, 0))
    return qspec, kvspec


def _sb_prompt(q, k, v):
    b, t, w = q.shape
    tile = ATT_BLOCK
    blocks = SB_BLOCKS_PER_STEP if t % (tile * SB_BLOCKS_PER_STEP) == 0 else 1
    qspec, kvspec = _prompt_specs(t, tile * blocks)
    return pl.pallas_call(
        functools.partial(_sb_prompt_kernel, tile=tile, blocks=blocks),
        grid=(b, w // LANES, t // (tile * blocks)),
        in_specs=[qspec, kvspec, kvspec],
        out_specs=qspec,
        out_shape=jax.ShapeDtypeStruct((b, t, w), F32),
        compiler_params=_params("arbitrary", "arbitrary", "arbitrary"),
        name="sb_prompt",
    )(q, k, v)


def _fox_prompt(q, k, v, fq, fk):
    b, t, w = q.shape
    tile = fk.shape[-1]
    blocks = FOX_BLOCKS_PER_STEP if t % (tile * FOX_BLOCKS_PER_STEP) == 0 else 1
    qspec, kvspec = _prompt_specs(t, tile * blocks)
    n_f = fq.shape[-1]
    return pl.pallas_call(
        functools.partial(_fox_prompt_kernel, tile=tile, blocks=blocks),
        grid=(b, w // LANES, t // (tile * blocks)),
        in_specs=[qspec, kvspec, kvspec,
                  pl.BlockSpec((1, tile * blocks, n_f), lambda b, h, i: (b, i, 0)),
                  pl.BlockSpec((1, 1, t // tile, 2, tile), lambda b, h, i: (b, h, 0, 0, 0))],
        out_specs=qspec,
        out_shape=jax.ShapeDtypeStruct((b, t, w), F32),
        scratch_shapes=[pltpu.VMEM((1, LANES), F32)],
        compiler_params=_params("arbitrary", "arbitrary", "arbitrary"),
        name="fox_prompt",
    )(q, k, v, fq, fk)


def _group_rows(g):
    return slice(g * LANES, (g + 1) * LANES)


def _sb_sample_kernel(q_ref, kn_ref, vn_ref, kc_ref, vc_ref, o_ref, *, tile):
    n, width = q_ref.shape[1:]
    groups = width // LANES
    past = kc_ref.shape[2]
    qs = []
    for g in range(groups):
        q = q_ref[0, :, _group_rows(g)]
        qs.append((_keep_head(q, True), _keep_head(q, False)))

    def visit(kbs, state, mask):
        kvs = []
        for g in range(groups):
            if mask is not None:
                kvs.append((kn_ref[0, _group_rows(g), :], vn_ref[0, _group_rows(g), :]))
            else:
                cols = pl.ds(pl.multiple_of(kbs[g] * tile, tile), tile)
                kvs.append((kc_ref[0, _group_rows(g), cols].astype(BF16),
                            vc_ref[0, _group_rows(g), cols].astype(BF16)))
        return _sb_visit(qs, kvs, state, mask)

    state = visit(None, _sb_start(n, groups), _causal(n, n, True))
    state = _sb_sweep(visit, [past // tile] * groups, state)
    for g in range(groups):
        o_ref[0, :, _group_rows(g)] = state[2 + 3 * g]


def _fox_sample_kernel(q_ref, kn_ref, vn_ref, kc_ref, vc_ref, fq_ref, fk_ref, o_ref):
    n, width = q_ref.shape[1:]
    groups = width // LANES
    past = kc_ref.shape[2]
    fq = fq_ref[0]
    causal = _causal(n, n, False)
    scored = []
    for g in range(groups):
        q = q_ref[0, :, _group_rows(g)]
        kn = kn_ref[0, _group_rows(g), :]
        kc = kc_ref[0, _group_rows(g), :].astype(BF16)
        for h in range(2):
            qh = _keep_head(q, h == 0)
            fk = fk_ref[0, g, h:h + 1, :]
            u_new = jnp.where(causal, _dot(qh, kn) - fk[:, past:past + n], NEG_BIG)
            u_old = _dot(qh, kc) - fk[:, 0:past]
            scored.append((u_new, u_old, fq[:, 2 * g + h:2 * g + h + 1]))
    weights = []
    for u_new, u_old, fq_col in scored:
        m = jnp.maximum(jnp.max(u_new, axis=1, keepdims=True), jnp.max(u_old, axis=1, keepdims=True)) + fq_col
        c = m - fq_col
        weights.append((jnp.exp(u_new - c).astype(BF16), jnp.exp(u_old - c).astype(BF16)))
    for g in range(groups):
        vn = vn_ref[0, _group_rows(g), :]
        vc = vc_ref[0, _group_rows(g), :].astype(BF16)
        accs = []
        for h in range(2):
            p_new, p_old = weights[2 * g + h]
            accs.append(_dot_nt(p_new, _keep_head(vn, h == 0, 0, 1.0))
                        + _dot_nt(p_old, _keep_head(vc, h == 0, 0, 1.0)))
        o_ref[0, :, _group_rows(g)] = _softmax_out(accs[0], accs[1])


def _sample_specs(n, width, past):
    qspec = pl.BlockSpec((1, n, width), lambda b: (b, 0, 0))
    new = pl.BlockSpec((1, width, n), lambda b: (b, 0, 0))
    cache = pl.BlockSpec((1, width, past), lambda b: (b, 0, 0))
    return qspec, new, cache


def _sb_sample(q, kn, vn, kc, vc):
    b, n, w = q.shape
    qspec, new, cache = _sample_specs(n, w, kc.shape[2])
    return pl.pallas_call(
        functools.partial(_sb_sample_kernel, tile=ATT_BLOCK),
        grid=(b,),
        in_specs=[qspec, new, new, cache, cache],
        out_specs=qspec,
        out_shape=jax.ShapeDtypeStruct((b, n, w), F32),
        compiler_params=_params("arbitrary"),
        name="sb_sample",
    )(q, kn, vn, kc, vc)


def _fox_sample(q, kn, vn, kc, vc, fq, fk):
    b, n, w = q.shape
    qspec, new, cache = _sample_specs(n, w, kc.shape[2])
    return pl.pallas_call(
        _fox_sample_kernel,
        grid=(b,),
        in_specs=[qspec, new, new, cache, cache,
                  pl.BlockSpec((1, n, fq.shape[-1]), lambda b: (b, 0, 0)),
                  pl.BlockSpec((1,) + fk.shape[1:], lambda b: (b, 0, 0, 0))],
        out_specs=qspec,
        out_shape=jax.ShapeDtypeStruct((b, n, w), F32),
        compiler_params=_params("arbitrary"),
        name="fox_sample",
    )(q, kn, vn, kc, vc, fq, fk)


def _post_kernel(x_ref, osb_ref, ofx_ref, ada_ref, gsb_ref, gfx_ref, wo_ref, gffn_ref,
                 wg_ref, wu_ref, wd_ref, gfin_ref, y_ref, *, final_norm):
    nb, tt, d = x_ref.shape
    rows = nb * tt
    ada = ada_ref[...]
    o_sb = _rms(osb_ref[...], gsb_ref[...])
    o_fx = _rms(ofx_ref[...], gfx_ref[...])
    sb_w, fx_w = o_sb.shape[-1], o_fx.shape[-1]
    proj = (_dot(o_sb.reshape(rows, sb_w).astype(BF16), wo_ref[0:sb_w, :])
            + _dot(o_fx.reshape(rows, fx_w).astype(BF16), wo_ref[sb_w:sb_w + fx_w, :]))
    x2 = x_ref[...] + (1.0 + ada[:, 2:3, :]) * proj.reshape(nb, tt, d)
    h = _rms(x2, gffn_ref[...]) * (1.0 + ada[:, 4:5, :]) + ada[:, 3:4, :]
    h = h.reshape(rows, d).astype(BF16)
    d_ff = wg_ref.shape[1]
    f = None
    for lo in range(0, d_ff, FF_CHUNK):
        hi = min(lo + FF_CHUNK, d_ff)
        g = _dot(h, wg_ref[:, lo:hi])
        u = _dot(h, wu_ref[:, lo:hi])
        act = (g * (1.0 / (1.0 + jnp.exp(-g))) * u).astype(BF16)
        part = _dot(act, wd_ref[lo:hi, :])
        f = part if f is None else f + part
    x3 = x2 + (1.0 + ada[:, 5:6, :]) * f.reshape(nb, tt, d)
    y_ref[...] = _rms(x3, gfin_ref[...]) if final_norm else x3


def _post(x, o_sb, o_fx, ada, g_sb, g_fx, wo16, g_ffn, wg16, wu16, wd16, g_fin, final_norm):
    b, t, d = x.shape
    tt = min(t, ROW_BLOCK)
    nb = max(1, ROW_BLOCK // tt)
    tok = lambda w: pl.BlockSpec((nb, tt, w), lambda i, j: (i, j, 0))
    const = lambda a: pl.BlockSpec(a.shape, lambda i, j: (0, 0), pipeline_mode=pl.Buffered(1))
    return pl.pallas_call(
        functools.partial(_post_kernel, final_norm=final_norm),
        grid=(b // nb, t // tt),
        in_specs=[tok(d), tok(o_sb.shape[-1]), tok(o_fx.shape[-1]),
                  pl.BlockSpec((nb, 6, d), lambda i, j: (i, 0, 0)),
                  const(g_sb), const(g_fx), const(wo16), const(g_ffn),
                  const(wg16), const(wu16), const(wd16), const(g_fin)],
        out_specs=tok(d),
        out_shape=jax.ShapeDtypeStruct((b, t, d), F32),
        compiler_params=_params("arbitrary", "arbitrary"),
        name="post_mixer_ffn",
    )(x, o_sb, o_fx, ada, g_sb, g_fx, wo16, g_ffn, wg16, wu16, wd16, g_fin)


def _pad_lanes(a, n):
    return jnp.pad(a, ((0, 0), (0, n - a.shape[1])))


def kernel(x_prompt, x_sample, c_prompt, c_sample, cache_sb_k, cache_sb_v, cache_fox_k, cache_fox_v, cache_fox_logf, w_ada, b_ada, g_mix, w_in, b_f, g_sb_out, g_fox_out, w_o, g_ffn, w_gate, w_up, w_down, g_final):
    depth = w_ada.shape[0]
    bp, tp, d = x_prompt.shape
    bs, ts, _ = x_sample.shape
    past, n_sb, hd = cache_sb_k.shape[2:]
    n_fx = cache_fox_k.shape[3]
    assert hd == HEAD_DIM and n_sb % 2 == 0 and n_fx % 2 == 0
    assert tp % ROW_BLOCK == 0 and tp % ATT_BLOCK == 0 and ROW_BLOCK % ts == 0 and bs % (ROW_BLOCK // ts) == 0
    assert past % ATT_BLOCK == 0
    sb_w, fx_w = n_sb * hd, n_fx * hd
    att = min(FOX_BLOCK, tp)
    past_pad = -(-(past + ts) // (3 * LANES)) * (3 * LANES)
    row = lambda a: a.reshape(1, -1)

    xp, xs = x_prompt, x_sample
    outs = [[] for _ in range(10)]
    for l in range(depth):
        in_cols = w_in.shape[2]
        w16t = jnp.pad(jnp.swapaxes(w_in[l], 0, 1), ((0, -in_cols % BF16_ROWS), (0, 0))).astype(BF16)
        bfr, bfc = row(b_f[l]), b_f[l].reshape(-1, 1)
        wo16, wg16 = w_o[l].astype(BF16), w_gate[l].astype(BF16)
        wu16, wd16 = w_up[l].astype(BF16), w_down[l].astype(BF16)

        c_all = jnp.concatenate([c_prompt, c_sample], axis=0)
        ada = _ada(c_all, w_ada[l], row(b_ada[l])).reshape(bp + bs, 6, d)
        ada_p, ada_s = ada[:bp], ada[bp:]

        (q_sb, k_sb, v_sb, k_sb16, v_sb16, q_fx, k_fx, v_fx, k_fx16, v_fx16, lf) = _pre(
            xp, ada_p, row(g_mix[l]), w16t, bfr, bfc, sb_w, fx_w, n_fx, True)
        o_sb = _sb_prompt(q_sb, k_sb16, v_sb16)
        f_cum = _cumsum_rows(lf.reshape(bp * n_fx, tp), ROW_BLOCK)
        fq = jnp.swapaxes(f_cum.reshape(bp, n_fx, tp), 1, 2)
        fk = jnp.swapaxes(f_cum.reshape(bp, n_fx // 2, 2, tp // att, att), 2, 3)
        o_fx = _fox_prompt(q_fx, k_fx16, v_fx16, fq, fk)
        xp = _post(xp, o_sb, o_fx, ada_p, row(g_sb_out[l]), row(g_fox_out[l]), wo16, row(g_ffn[l]),
                   wg16, wu16, wd16, row(g_final), l + 1 == depth)
        for dst, a in zip(outs[0:4], (k_sb, v_sb, k_fx, v_fx)):
            dst.append(jnp.swapaxes(a, 1, 2).reshape(bp, tp, -1, hd))
        outs[4].append(jnp.swapaxes(lf, 1, 2))

        (q_sb, k_sb, v_sb, k_sb16, v_sb16, q_fx, k_fx, v_fx, k_fx16, v_fx16, lf) = _pre(
            xs, ada_s, row(g_mix[l]), w16t, bfr, bfc, sb_w, fx_w, n_fx, False)
        time_minor = lambda a: jnp.swapaxes(a.reshape(a.shape[0], a.shape[1], -1), 1, 2)
        o_sb = _sb_sample(q_sb, time_minor(k_sb16), time_minor(v_sb16),
                          time_minor(cache_sb_k[l]), time_minor(cache_sb_v[l]))
        lf_all = jnp.concatenate([cache_fox_logf[l].astype(F32), lf], axis=1)
        lf_rows = _pad_lanes(jnp.swapaxes(lf_all, 1, 2).reshape(bs * n_fx, past + ts), past_pad)
        f_all = _cumsum_rows(lf_rows, past_pad // 3)
        fq = jnp.swapaxes(f_all.reshape(bs, n_fx, past_pad)[:, :, past:past + ts], 1, 2)
        fk = f_all.reshape(bs, n_fx // 2, 2, past_pad)
        o_fx = _fox_sample(q_fx, time_minor(k_fx16), time_minor(v_fx16),
                           time_minor(cache_fox_k[l]), time_minor(cache_fox_v[l]), fq, fk)
        xs = _post(xs, o_sb, o_fx, ada_s, row(g_sb_out[l]), row(g_fox_out[l]), wo16, row(g_ffn[l]),
                   wg16, wu16, wd16, row(g_final), l + 1 == depth)
        for dst, a in zip(outs[5:9], (k_sb, v_sb, k_fx, v_fx)):
            dst.append(a.reshape(bs, ts, -1, hd))
        outs[9].append(lf)

    return (xp, xs) + tuple(jnp.stack(o) for o in outs)
```

```python
import functools

import jax
import jax.numpy as jnp
from jax import lax
from jax.experimental import pallas as pl
from jax.experimental.pallas import tpu as pltpu

F32 = jnp.float32
BF16 = jnp.bfloat16
EPS = 1e-6
HEAD_DIM = 64
LANES = 128
BF16_ROWS = 16
ROW_BLOCK = 512
FF_CHUNK = 512
ATT_BLOCK = 256
SB_BLOCKS_PER_STEP = 8
FOX_BLOCKS_PER_STEP = 2
FOX_BLOCK = 512
NEG_BIG = -1e30
EXP_ZERO = -105.0
NORM_CHUNK = 2048
NORM_SLACK = 1.0 + 2.0 ** -8
VMEM_LIMIT = 56 * 1024 * 1024


def _params(*semantics):
    return pltpu.CompilerParams(dimension_semantics=semantics, vmem_limit_bytes=VMEM_LIMIT)


def _dot(a, b):
    return jnp.dot(a, b, preferred_element_type=F32)


def _dot_nt(a, b):
    return lax.dot_general(a, b, (((1,), (1,)), ((), ())), preferred_element_type=F32)


def _rms(x, g):
    return x * lax.rsqrt(jnp.mean(x * x, axis=-1, keepdims=True) + EPS) * g


def _split_bf16(x, parts):
    out = []
    for _ in range(parts - 1):
        p = x.astype(BF16)
        out.append(p)
        x = x - p.astype(F32)
    out.append(x.astype(BF16))
    return out


def _dot_split(x, m, parts):
    acc = None
    for p in _split_bf16(x, parts):
        d = _dot(p, m)
        acc = d if acc is None else acc + d
    return acc


def _keep_head(x, first, axis=1, fill=0.0):
    pos = lax.broadcasted_iota(jnp.int32, x.shape, axis)
    keep = (pos < HEAD_DIM) if first else (pos >= HEAD_DIM)
    return jnp.where(keep, x, jnp.full_like(x, fill))


def _per_head(col0, col1, shape):
    lane = lax.broadcasted_iota(jnp.int32, shape, 1)
    return jnp.where(lane < HEAD_DIM, col0, col1)


def _softmax_out(acc0, acc1):
    first = lax.broadcasted_iota(jnp.int32, acc0.shape, 1) < HEAD_DIM
    return jnp.where(first, acc0, acc1) / pltpu.roll(jnp.where(first, acc1, acc0), LANES // 2, 1)


def _lanes_to(x, n):
    if n <= LANES:
        return x[:, :n]
    return jnp.concatenate([x] * (n // LANES), axis=1)


def _ada_kernel(c_ref, w_ref, b_ref, o_ref):
    c = c_ref[...]
    s = (c * (1.0 / (1.0 + jnp.exp(-c)))).astype(BF16)
    o_ref[...] = _dot(s, w_ref[...].astype(BF16)) + b_ref[...]


def _ada(c, w, b):
    rows, d = c.shape
    n = w.shape[1]
    tn = 1024
    return pl.pallas_call(
        _ada_kernel,
        grid=(n // tn,),
        in_specs=[pl.BlockSpec((rows, d), lambda j: (0, 0)),
                  pl.BlockSpec((d, tn), lambda j: (0, j)),
                  pl.BlockSpec((1, tn), lambda j: (0, j))],
        out_specs=pl.BlockSpec((rows, tn), lambda j: (0, j)),
        out_shape=jax.ShapeDtypeStruct((rows, n), F32),
        compiler_params=_params("arbitrary"),
        name="ada",
    )(c, w, b)


def _log_sigmoid(u):
    return jnp.minimum(u, 0.0) - jnp.log(1.0 + jnp.exp(-jnp.abs(u)))


def _pre_kernel(x_ref, ada_ref, g_ref, wt_ref, bf_ref, bfc_ref,
                qsb_ref, ksb_ref, vsb_ref, ksb16_ref, vsb16_ref,
                qfx_ref, kfx_ref, vfx_ref, kfx16_ref, vfx16_ref, lf_ref, *, sb_w, fx_w, n_f, time_minor):
    x = x_ref[...]
    nb, tt, d = x.shape
    ada = ada_ref[...]
    h = _rms(x, g_ref[...]) * (1.0 + ada[:, 1:2, :]) + ada[:, 0:1, :]
    h = h.reshape(nb * tt, d).astype(BF16)
    q_scale = HEAD_DIM ** -0.5

    def proj(lo, width):
        return _dot_nt(h, wt_ref[lo:lo + width, :]).reshape(nb, tt, width)

    base = 3 * sb_w
    qsb_ref[...] = (proj(0, sb_w) * q_scale).astype(BF16)
    qfx_ref[...] = (proj(base, fx_w) * q_scale).astype(BF16)
    if time_minor:
        sb = _dot_nt(wt_ref[sb_w:base, :], h)
        fx = _dot_nt(wt_ref[base + fx_w:, :], h)
        parts = (sb[:sb_w], sb[sb_w:], fx[:fx_w], fx[fx_w:2 * fx_w])
        lf_ref[...] = _log_sigmoid(fx[2 * fx_w:2 * fx_w + n_f] + bfc_ref[...])[None]
    else:
        parts = (proj(sb_w, sb_w), proj(2 * sb_w, sb_w), proj(base + fx_w, fx_w), proj(base + 2 * fx_w, fx_w))
        forget = _dot_nt(h, wt_ref[base + 3 * fx_w:, :])
        lf_ref[...] = _log_sigmoid(forget[:, :n_f] + bf_ref[...]).reshape(nb, tt, n_f)
    for r, out, out16 in zip(parts, (ksb_ref, vsb_ref, kfx_ref, vfx_ref),
                             (ksb16_ref, vsb16_ref, kfx16_ref, vfx16_ref)):
        r = r[None] if time_minor else r
        out[...] = r
        out16[...] = r.astype(BF16)


def _pre(x, ada, g, w16t, bfr, bfc, sb_w, fx_w, n_f, time_minor):
    b, t, d = x.shape
    tt = min(t, ROW_BLOCK)
    nb = max(1, ROW_BLOCK // tt)
    assert nb == 1 or not time_minor
    grid = (b // nb, t // tt)
    tok = lambda w: pl.BlockSpec((nb, tt, w), lambda i, j: (i, j, 0))
    const = lambda a: pl.BlockSpec(a.shape, lambda i, j: (0, 0))
    shape = lambda w, dt: jax.ShapeDtypeStruct((b, t, w), dt)
    if time_minor:
        kv = lambda w: pl.BlockSpec((1, w, tt), lambda i, j: (i, 0, j))
        kv_shape = lambda w, dt: jax.ShapeDtypeStruct((b, w, t), dt)
    else:
        kv, kv_shape = tok, shape
    return pl.pallas_call(
        functools.partial(_pre_kernel, sb_w=sb_w, fx_w=fx_w, n_f=n_f, time_minor=time_minor),
        grid=grid,
        in_specs=[tok(d), pl.BlockSpec((nb, 6, d), lambda i, j: (i, 0, 0)),
                  const(g), const(w16t), const(bfr), const(bfc)],
        out_specs=[tok(sb_w)] + [kv(sb_w)] * 4 + [tok(fx_w)] + [kv(fx_w)] * 4 + [kv(n_f)],
        out_shape=[shape(sb_w, BF16), kv_shape(sb_w, F32), kv_shape(sb_w, F32), kv_shape(sb_w, BF16),
                   kv_shape(sb_w, BF16),
                   shape(fx_w, BF16), kv_shape(fx_w, F32), kv_shape(fx_w, F32), kv_shape(fx_w, BF16),
                   kv_shape(fx_w, BF16),
                   kv_shape(n_f, F32)],
        compiler_params=_params("arbitrary", "arbitrary"),
        name="pre_mixer",
    )(x, ada, g, w16t, bfr, bfc)


def _cumsum_kernel(x_ref, o_ref, *, tc):
    r = lax.broadcasted_iota(jnp.int32, (tc, tc), 0)
    c = lax.broadcasted_iota(jnp.int32, (tc, tc), 1)
    upper = jnp.where(r <= c, 1.0, 0.0).astype(BF16)
    ones = jnp.ones((tc, LANES), BF16)
    carry = jnp.zeros((x_ref.shape[0], LANES), F32)
    for lo in range(0, x_ref.shape[1], tc):
        x = x_ref[:, lo:lo + tc]
        o_ref[:, lo:lo + tc] = _dot_split(x, upper, 3) + _lanes_to(carry, tc)
        carry = carry + _dot_split(x, ones, 3)


def _cumsum_rows(x, tc):
    return pl.pallas_call(
        functools.partial(_cumsum_kernel, tc=tc),
        out_shape=jax.ShapeDtypeStruct(x.shape, F32),
        compiler_params=pltpu.CompilerParams(vmem_limit_bytes=VMEM_LIMIT),
        name="cumsum_time",
    )(x)


def _tri_strict(n):
    r = lax.broadcasted_iota(jnp.int32, (n, n), 0)
    c = lax.broadcasted_iota(jnp.int32, (n, n), 1)
    return jnp.where(r > c, 1.0, 0.0).astype(BF16)


def _causal(tq, tk, strict):
    r = lax.broadcasted_iota(jnp.int32, (tq, tk), 0)
    c = lax.broadcasted_iota(jnp.int32, (tq, tk), 1)
    return (c < r) if strict else (c <= r)


def _sb_tiles(chains, mask):
    tri = _tri_strict(chains[0][1].shape[1])
    zs = [_dot(q, k) for q, k, _, _ in chains]
    staged = []
    for z in zs:
        log_beta = jnp.minimum(z, 0.0) - jnp.log(1.0 + jnp.exp(-jnp.abs(z)))
        log_keep = log_beta - z
        if mask is not None:
            log_keep = jnp.where(mask, log_keep, 0.0)
        staged.append((log_beta, log_keep[:, 0:1], log_keep.astype(BF16)))
    betweens = [_dot(terms, tri) for _, _, terms in staged]
    weights = []
    for (log_beta, _, _), between, (_, _, _, carry) in zip(staged, betweens, chains):
        a = jnp.exp(log_beta + between + carry)
        if mask is not None:
            a = jnp.where(mask, a, 0.0)
        weights.append(a.astype(BF16))
    pvs = [_dot_nt(a, v) for a, (_, _, v, _) in zip(weights, chains)]
    return [(pv, carry + (between[:, 0:1] + first))
            for pv, between, (_, first, _), (_, _, _, carry) in zip(pvs, betweens, staged, chains)]


def _sb_start(rows, groups):
    col = jnp.zeros((rows, 1), F32)
    return (col, col, jnp.zeros((rows, LANES), F32)) * groups


def _sb_visit(qs, kvs, state, mask):
    chains = []
    for g, ((q0, q1), (k, v)) in enumerate(zip(qs, kvs)):
        chains.append((q0, k, _keep_head(v, True, 0), state[3 * g]))
        chains.append((q1, k, _keep_head(v, False, 0), state[3 * g + 1]))
    res = _sb_tiles(chains, mask)
    out = ()
    for g in range(len(qs)):
        (pv0, c0), (pv1, c1) = res[2 * g], res[2 * g + 1]
        out += (c0, c1, state[3 * g + 2] + pv0 + pv1)
    return out


def _sb_sweep(visit, lefts, state):
    groups = len(lefts)

    def live(s):
        alive = jnp.bool_(False)
        for g in range(groups):
            more = jnp.max(jnp.maximum(s[1 + 3 * g], s[2 + 3 * g])) > EXP_ZERO
            alive = jnp.logical_or(alive, jnp.logical_and(s[0] < lefts[g], more))
        return alive

    def step(s):
        i, old = s[0], s[1:]
        new = visit([jnp.maximum(lefts[g] - 1 - i, 0) for g in range(groups)], old, None)
        out = (i + 1,)
        for g in range(groups):
            inside = i < lefts[g]
            out += tuple(jnp.where(inside, n, o) for n, o in zip(new[3 * g:3 * g + 3], old[3 * g:3 * g + 3]))
        return out

    return lax.while_loop(live, step, (jnp.int32(0),) + tuple(state))[1:]


def _head_column(f, h):
    lane = lax.broadcasted_iota(jnp.int32, f.shape, 1)
    return jnp.sum(jnp.where(lane == h, f, 0.0), axis=1, keepdims=True)


def _sb_prompt_kernel(q_ref, k_ref, v_ref, o_ref, *, tile, blocks):
    first = pl.program_id(2) * blocks
    qs = []
    for j in range(blocks):
        q = q_ref[0, j * tile:(j + 1) * tile, :]
        qs.append((_keep_head(q, True), _keep_head(q, False)))

    def visit(kbs, state, mask):
        kvs = []
        for j in range(blocks):
            start = pl.multiple_of(kbs[j] * tile, tile)
            kvs.append((k_ref[0, :, pl.ds(start, tile)], v_ref[0, :, pl.ds(start, tile)]))
        return _sb_visit(qs, kvs, state, mask)

    lefts = [first + j for j in range(blocks)]
    state = visit(lefts, _sb_start(tile, blocks), _causal(tile, tile, True))
    state = _sb_sweep(visit, lefts, state)
    for j in range(blocks):
        o_ref[0, j * tile:(j + 1) * tile, :] = state[2 + 3 * j]


def _head_norms_sq(x16):
    x = x16.astype(F32)
    r = lax.broadcasted_iota(jnp.int32, (LANES, LANES), 0)
    c = lax.broadcasted_iota(jnp.int32, (LANES, LANES), 1)
    same_head = jnp.where((r < HEAD_DIM) == (c < HEAD_DIM), 1.0, 0.0).astype(BF16)
    return _dot_split(x * x, same_head, 2)


def _fox_prompt_kernel(q_ref, k_ref, v_ref, fq_ref, fk_ref, o_ref, kmax_ref, *, tile, blocks):
    hp = pl.program_id(1)
    first = pl.program_id(2) * blocks
    shape = (tile, LANES)
    half_tile = tile // 2
    causal = _causal(tile, tile, False)
    zero = jnp.zeros(shape, F32)
    every = range(blocks)

    @pl.when(first == 0)
    def _():
        width = min(NORM_CHUNK, k_ref.shape[2])

        def chunk(i, best):
            kc = k_ref[0, :, pl.ds(pl.multiple_of(i * width, width), width)].astype(F32)
            sq = kc * kc
            return (jnp.maximum(best[0], jnp.sum(sq[:HEAD_DIM], axis=0, keepdims=True)),
                    jnp.maximum(best[1], jnp.sum(sq[HEAD_DIM:], axis=0, keepdims=True)))

        none = jnp.zeros((1, width), F32)
        best = lax.fori_loop(0, k_ref.shape[2] // width, chunk, (none, none))
        kmax_ref[...] = _per_head(jnp.max(best[0], axis=1, keepdims=True),
                                  jnp.max(best[1], axis=1, keepdims=True), (1, LANES))

    kmax = jnp.sqrt(kmax_ref[...]) * NORM_SLACK
    qis = [first + j for j in every]
    qs, fqs, reach = [], [], []
    for j in every:
        rows = slice(j * tile, (j + 1) * tile)
        q = q_ref[0, rows, :]
        qs.append((_keep_head(q, True), _keep_head(q, False)))
        fq = fq_ref[0, rows, :]
        fqs.append((_head_column(fq, 2 * hp), _head_column(fq, 2 * hp + 1)))
        reach.append(jnp.sqrt(_head_norms_sq(q)) * kmax + _per_head(fqs[j][0], fqs[j][1], shape))

    def tile_left(j, i):
        return jnp.maximum(qis[j] - 1 - i, 0)

    def key_cols(kb):
        return pl.ds(pl.multiple_of(kb * tile, tile), tile)

    def last_fk(kb):
        fk = fk_ref[0, 0, kb]
        return _per_head(fk[0:1, tile - 1:tile], fk[1:2, tile - 1:tile], (1, LANES))

    def scores(j, kb):
        k = k_ref[0, :, key_cols(kb)]
        fk = fk_ref[0, 0, kb]
        return _dot(qs[j][0], k) - fk[0:1, :], _dot(qs[j][1], k) - fk[1:2, :]

    def diagonal_scores(j):
        k = k_ref[0, :, key_cols(qis[j])]
        fk = fk_ref[0, 0, qis[j]]
        raw = [(_dot(q, k[:, :half_tile]) - fk[r:r + 1, :half_tile],
                _dot(q[half_tile:], k[:, half_tile:]) - fk[r:r + 1, half_tile:]) for r, q in enumerate(qs[j])]
        unseen = jnp.full((half_tile, half_tile), NEG_BIG, F32)
        return [(jnp.where(causal[:, :half_tile], left, NEG_BIG),
                 jnp.concatenate([unseen, jnp.where(causal[half_tile:, half_tile:], right, NEG_BIG)], axis=0))
                for left, right in raw]

    def best_gap(j, kb, m0, m1):
        return jnp.max(reach[j] - _per_head(m0, m1, shape) - last_fk(kb))

    def max_live(s):
        alive = jnp.bool_(False)
        for j in every:
            alive = jnp.logical_or(alive, jnp.logical_and(s[0] < qis[j], s[1 + j] > 0.0))
        return alive

    def max_step(s, us=None):
        i, ms = s[0], s[1 + blocks:]
        if us is None:
            us = [scores(j, tile_left(j, i)) for j in every]
        gaps, new = (), ()
        for j in every:
            inside = i < qis[j]
            both = [jnp.where(inside, jnp.maximum(ms[2 * j + r], jnp.max(us[j][r], axis=1, keepdims=True) + fqs[j][r]),
                              ms[2 * j + r]) for r in range(2)]
            gaps += (best_gap(j, tile_left(j, i + 1), both[0], both[1]),)
            new += tuple(both)
        return (i + 1,) + gaps + new

    diagonal = [diagonal_scores(j) for j in every]
    ms = ()
    for j in every:
        ms += tuple(jnp.maximum(jnp.max(left, axis=1, keepdims=True), jnp.max(right, axis=1, keepdims=True)) + fq
                    for (left, right), fq in zip(diagonal[j], fqs[j]))
    gaps = tuple(best_gap(j, tile_left(j, 0), ms[2 * j], ms[2 * j + 1]) for j in every)
    nearest = [scores(j, tile_left(j, 0)) for j in every]
    ms = lax.while_loop(max_live, max_step, max_step((jnp.int32(0),) + gaps + ms, nearest))[1 + blocks:]

    cs = [(ms[2 * j] - fqs[j][0], ms[2 * j + 1] - fqs[j][1]) for j in every]
    gap_rows = [jnp.max(reach[j] - _per_head(ms[2 * j], ms[2 * j + 1], shape), axis=0, keepdims=True) for j in every]

    def head_gaps(j, kb):
        gap = gap_rows[j] - last_fk(kb)
        head0 = lax.broadcasted_iota(jnp.int32, gap.shape, 1) < HEAD_DIM
        return jnp.max(jnp.where(head0, gap, NEG_BIG)), jnp.max(jnp.where(head0, NEG_BIG, gap))

    def with_ones(v, head1):
        ones_rows = (lax.broadcasted_iota(jnp.int32, v.shape, 0) < HEAD_DIM) == head1
        return jnp.where(ones_rows, jnp.ones_like(v), v)

    blank = jnp.zeros((half_tile, LANES), F32)
    state = ()
    for j in every:
        v = v_ref[0, :, key_cols(qis[j])]
        weights = [(jnp.exp(left - c).astype(BF16), jnp.exp(right - c).astype(BF16))
                   for (left, right), c in zip(diagonal[j], cs[j])]
        accs = [_dot_nt(left, vh[:, :half_tile])
                + jnp.concatenate([blank, _dot_nt(right[half_tile:], vh[:, half_tile:])], axis=0)
                for (left, right), vh in zip(weights, (with_ones(v, False), with_ones(v, True)))]
        state += (jnp.int32(0),) + head_gaps(j, tile_left(j, 0)) + tuple(accs)

    def pair_active(s, j):
        done, gap0, gap1 = s[1 + 5 * j:4 + 5 * j]
        return jnp.logical_and(s[0] < qis[j], jnp.minimum(gap0, gap1) > EXP_ZERO)

    def pair_live(s):
        alive = jnp.bool_(False)
        for j in every:
            alive = jnp.logical_or(alive, pair_active(s, j))
        return alive

    def pair_step(s, us=None):
        i = s[0]
        kbs = [tile_left(j, i) for j in every]
        if us is None:
            us = [scores(j, kbs[j]) for j in every]
        ps =[[jnp.exp(u - c).astype(BF16) for u, c in zip(us[j], cs[j])] for j in every]
        out = (i + 1,)
        for j in every:
            done, gap0, gap1, acc0, acc1 = s[1 + 5 * j:6 + 5 * j]
            active = pair_active(s, j)
            v = v_ref[0, :, key_cols(kbs[j])]
            new0, new1 = head_gaps(j, tile_left(j, i + 1))
            out += (jnp.where(active, i + 1, done), jnp.where(active, new0, gap0), jnp.where(active, new1, gap1),
                    acc0 + jnp.where(active, _dot_nt(ps[j][0], with_ones(v, False)), 0.0),
                    acc1 + jnp.where(active, _dot_nt(ps[j][1], with_ones(v, True)), 0.0))
        return out

    state = lax.while_loop(pair_live, pair_step, pair_step((jnp.int32(0),) + state, nearest))[1:]

    seconds = [state[5 * j + 2] > state[5 * j + 1] for j in every]
    starts = [state[5 * j] for j in every]
    q_lone = [jnp.where(seconds[j], qs[j][1], qs[j][0]) for j in every]
    c_lone = [jnp.where(seconds[j], cs[j][1], cs[j][0]) for j in every]

    def lone_active(s, j):
        return jnp.logical_and(starts[j] + s[0] < qis[j], s[1 + 2 * j] > EXP_ZERO)

    def lone_live(s):
        alive = jnp.bool_(False)
        for j in every:
            alive = jnp.logical_or(alive, lone_active(s, j))
        return alive

    def lone_step(s):
        t = s[0]
        kbs = [tile_left(j, starts[j] + t) for j in every]
        us = []
        for j in every:
            k = k_ref[0, :, key_cols(kbs[j])]
            fk = fk_ref[0, 0, kbs[j]]
            fk_s = jnp.where(seconds[j], fk[1:2, :], fk[0:1, :])
            us.append([_dot(q_lone[j], k[:, lo:lo + half_tile]) - fk_s[:, lo:lo + half_tile] for lo in (0, half_tile)])
        ps = [[jnp.exp(u - c_lone[j]).astype(BF16) for u in us[j]] for j in every]
        out = (t + 1,)
        for j in every:
            gap, acc = s[1 + 2 * j:3 + 2 * j]
            active = lone_active(s, j)
            v_s = with_ones(v_ref[0, :, key_cols(kbs[j])], seconds[j])
            gaps = head_gaps(j, tile_left(j, starts[j] + t + 1))
            more = _dot_nt(ps[j][0], v_s[:, :half_tile]) + _dot_nt(ps[j][1], v_s[:, half_tile:])
            out += (jnp.where(active, jnp.where(seconds[j], gaps[1], gaps[0]), gap), acc + jnp.where(active, more, 0.0))
        return out

    lone = (jnp.int32(0),)
    for j in every:
        lone += (jnp.maximum(state[5 * j + 1], state[5 * j + 2]), zero)
    lone = lax.while_loop(lone_live, lone_step, lone)[1:]

    for j in every:
        acc0 = state[5 * j + 3] + jnp.where(seconds[j], 0.0, lone[2 * j + 1])
        acc1 = state[5 * j + 4] + jnp.where(seconds[j], lone[2 * j + 1], 0.0)
        o_ref[0, j * tile:(j + 1) * tile, :] = _softmax_out(acc0, acc1)


def _prompt_specs(t, tile):
    qspec = pl.BlockSpec((1, tile, LANES), lambda b, h, i: (b, i, h))
    kvspec = pl.BlockSpec((1, LANES, t), lambda b, h, i: (b, h, 0))
    return qspec, kvspec


def _sb_prompt(q, k, v):
    b, t, w = q.shape
    tile = ATT_BLOCK
    blocks = SB_BLOCKS_PER_STEP if t % (tile * SB_BLOCKS_PER_STEP) == 0 else 1
    qspec, kvspec = _prompt_specs(t, tile * blocks)
    return pl.pallas_call(
        functools.partial(_sb_prompt_kernel, tile=tile, blocks=blocks),
        grid=(b, w // LANES, t // (tile * blocks)),
        in_specs=[qspec, kvspec, kvspec],
        out_specs=qspec,
        out_shape=jax.ShapeDtypeStruct((b, t, w), F32),
        compiler_params=_params("arbitrary", "arbitrary", "arbitrary"),
        name="sb_prompt",
    )(q, k, v)


def _fox_prompt(q, k, v, fq, fk):
    b, t, w = q.shape
    tile = fk.shape[-1]
    blocks = FOX_BLOCKS_PER_STEP if t % (tile * FOX_BLOCKS_PER_STEP) == 0 else 1
    qspec, kvspec = _prompt_specs(t, tile * blocks)
    n_f = fq.shape[-1]
    return pl.pallas_call(
        functools.partial(_fox_prompt_kernel, tile=tile, blocks=blocks),
        grid=(b, w // LANES, t // (tile * blocks)),
        in_specs=[qspec, kvspec, kvspec,
                  pl.BlockSpec((1, tile * blocks, n_f), lambda b, h, i: (b, i, 0)),
                  pl.BlockSpec((1, 1, t // tile, 2, tile), lambda b, h, i: (b, h, 0, 0, 0))],
        out_specs=qspec,
        out_shape=jax.ShapeDtypeStruct((b, t, w), F32),
        scratch_shapes=[pltpu.VMEM((1, LANES), F32)],
        compiler_params=_params("arbitrary", "arbitrary", "arbitrary"),
        name="fox_prompt",
    )(q, k, v, fq, fk)


def _group_rows(g):
    return slice(g * LANES, (g + 1) * LANES)


def _sb_sample_kernel(q_ref, kn_ref, vn_ref, kc_ref, vc_ref, o_ref, *, tile):
    n, width = q_ref.shape[1:]
    groups = width // LANES
    past = kc_ref.shape[2]
    qs = []
    for g in range(groups):
        q = q_ref[0, :, _group_rows(g)]
        qs.append((_keep_head(q, True), _keep_head(q, False)))

    def visit(kbs, state, mask):
        kvs = []
        for g in range(groups):
            if mask is not None:
                kvs.append((kn_ref[0, _group_rows(g), :], vn_ref[0, _group_rows(g), :]))
            else:
                cols = pl.ds(pl.multiple_of(kbs[g] * tile, tile), tile)
                kvs.append((kc_ref[0, _group_rows(g), cols].astype(BF16),
                            vc_ref[0, _group_rows(g), cols].astype(BF16)))
        return _sb_visit(qs, kvs, state, mask)

    state = visit(None, _sb_start(n, groups), _causal(n, n, True))
    state = _sb_sweep(visit, [past // tile] * groups, state)
    for g in range(groups):
        o_ref[0, :, _group_rows(g)] = state[2 + 3 * g]


def _fox_sample_kernel(q_ref, kn_ref, vn_ref, kc_ref, vc_ref, fq_ref, fk_ref, o_ref):
    n, width = q_ref.shape[1:]
    groups = width // LANES
    past = kc_ref.shape[2]
    fq = fq_ref[0]
    causal = _causal(n, n, False)
    scored = []
    for g in range(groups):
        q = q_ref[0, :, _group_rows(g)]
        kn = kn_ref[0, _group_rows(g), :]
        kc = kc_ref[0, _group_rows(g), :].astype(BF16)
        for h in range(2):
            qh = _keep_head(q, h == 0)
            fk = fk_ref[0, g, h:h + 1, :]
            u_new = jnp.where(causal, _dot(qh, kn) - fk[:, past:past + n], NEG_BIG)
            u_old = _dot(qh, kc) - fk[:, 0:past]
            scored.append((u_new, u_old, fq[:, 2 * g + h:2 * g + h + 1]))
    weights = []
    for u_new, u_old, fq_col in scored:
        m = jnp.maximum(jnp.max(u_new, axis=1, keepdims=True), jnp.max(u_old, axis=1, keepdims=True)) + fq_col
        c = m - fq_col
        weights.append((jnp.exp(u_new - c).astype(BF16), jnp.exp(u_old - c).astype(BF16)))
    for g in range(groups):
        vn = vn_ref[0, _group_rows(g), :]
        vc = vc_ref[0, _group_rows(g), :].astype(BF16)
        accs = []
        for h in range(2):
            p_new, p_old = weights[2 * g + h]
            accs.append(_dot_nt(p_new, _keep_head(vn, h == 0, 0, 1.0))
                        + _dot_nt(p_old, _keep_head(vc, h == 0, 0, 1.0)))
        o_ref[0, :, _group_rows(g)] = _softmax_out(accs[0], accs[1])


def _sample_specs(n, width, past):
    qspec = pl.BlockSpec((1, n, width), lambda b: (b, 0, 0))
    new = pl.BlockSpec((1, width, n), lambda b: (b, 0, 0))
    cache = pl.BlockSpec((1, width, past), lambda b: (b, 0, 0))
    return qspec, new, cache


def _sample_kernel(sb_q, sb_kn, sb_vn, sb_kc, sb_vc, fx_q, fx_kn, fx_vn, fx_kc, fx_vc, fq_ref, fk_ref,
                   sb_o, fx_o, *, tile):
    _sb_sample_kernel(sb_q, sb_kn, sb_vn, sb_kc, sb_vc, sb_o, tile=tile)
    _fox_sample_kernel(fx_q, fx_kn, fx_vn, fx_kc, fx_vc, fq_ref, fk_ref, fx_o)


def _sample_attention(sb, fx, fq, fk):
    b, n, sb_w = sb[0].shape
    fx_w = fx[0].shape[2]
    sb_q, sb_new, sb_cache = _sample_specs(n, sb_w, sb[3].shape[2])
    fx_q, fx_new, fx_cache = _sample_specs(n, fx_w, fx[3].shape[2])
    return pl.pallas_call(
        functools.partial(_sample_kernel, tile=ATT_BLOCK),
        grid=(b,),
        in_specs=[sb_q, sb_new, sb_new, sb_cache, sb_cache, fx_q, fx_new, fx_new, fx_cache, fx_cache,
                  pl.BlockSpec((1, n, fq.shape[-1]), lambda b: (b, 0, 0)),
                  pl.BlockSpec((1,) + fk.shape[1:], lambda b: (b, 0, 0, 0))],
        out_specs=[sb_q, fx_q],
        out_shape=[jax.ShapeDtypeStruct((b, n, sb_w), F32), jax.ShapeDtypeStruct((b, n, fx_w), F32)],
        compiler_params=_params("arbitrary"),
        name="sample_attention",
    )(*sb, *fx, fq, fk)


def _post_kernel(x_ref, osb_ref, ofx_ref, ada_ref, gsb_ref, gfx_ref, wo_ref, gffn_ref,
                 wg_ref, wu_ref, wd_ref, gfin_ref, y_ref, *, final_norm):
    nb, tt, d = x_ref.shape
    rows = nb * tt
    ada = ada_ref[...]
    o_sb = _rms(osb_ref[...], gsb_ref[...])
    o_fx = _rms(ofx_ref[...], gfx_ref[...])
    sb_w, fx_w = o_sb.shape[-1], o_fx.shape[-1]
    proj = (_dot(o_sb.reshape(rows, sb_w).astype(BF16), wo_ref[0:sb_w, :])
            + _dot(o_fx.reshape(rows, fx_w).astype(BF16), wo_ref[sb_w:sb_w + fx_w, :]))
    x2 = x_ref[...] + (1.0 + ada[:, 2:3, :]) * proj.reshape(nb, tt, d)
    h = _rms(x2, gffn_ref[...]) * (1.0 + ada[:, 4:5, :]) + ada[:, 3:4, :]
    h = h.reshape(rows, d).astype(BF16)
    d_ff = wg_ref.shape[1]
    f = None
    for lo in range(0, d_ff, FF_CHUNK):
        hi = min(lo + FF_CHUNK, d_ff)
        g = _dot(h, wg_ref[:, lo:hi])
        u = _dot(h, wu_ref[:, lo:hi])
        act = (g * (1.0 / (1.0 + jnp.exp(-g))) * u).astype(BF16)
        part = _dot(act, wd_ref[lo:hi, :])
        f = part if f is None else f + part
    x3 = x2 + (1.0 + ada[:, 5:6, :]) * f.reshape(nb, tt, d)
    y_ref[...] = _rms(x3, gfin_ref[...]) if final_norm else x3


def _post(x, o_sb, o_fx, ada, g_sb, g_fx, wo16, g_ffn, wg16, wu16, wd16, g_fin, final_norm):
    b, t, d = x.shape
    tt = min(t, ROW_BLOCK)
    nb = max(1, ROW_BLOCK // tt)
    tok = lambda w: pl.BlockSpec((nb, tt, w), lambda i, j: (i, j, 0))
    const = lambda a: pl.BlockSpec(a.shape, lambda i, j: (0, 0), pipeline_mode=pl.Buffered(1))
    return pl.pallas_call(
        functools.partial(_post_kernel, final_norm=final_norm),
        grid=(b // nb, t // tt),
        in_specs=[tok(d), tok(o_sb.shape[-1]), tok(o_fx.shape[-1]),
                  pl.BlockSpec((nb, 6, d), lambda i, j: (i, 0, 0)),
                  const(g_sb), const(g_fx), const(wo16), const(g_ffn),
                  const(wg16), const(wu16), const(wd16), const(g_fin)],
        out_specs=tok(d),
        out_shape=jax.ShapeDtypeStruct((b, t, d), F32),
        compiler_params=_params("arbitrary", "arbitrary"),
        name="post_mixer_ffn",
    )(x, o_sb, o_fx, ada, g_sb, g_fx, wo16, g_ffn, wg16, wu16, wd16, g_fin)


def _pad_lanes(a, n):
    return jnp.pad(a, ((0, 0), (0, n - a.shape[1])))


def kernel(x_prompt, x_sample, c_prompt, c_sample, cache_sb_k, cache_sb_v, cache_fox_k, cache_fox_v, cache_fox_logf, w_ada, b_ada, g_mix, w_in, b_f, g_sb_out, g_fox_out, w_o, g_ffn, w_gate, w_up, w_down, g_final):
    depth = w_ada.shape[0]
    bp, tp, d = x_prompt.shape
    bs, ts, _ = x_sample.shape
    past, n_sb, hd = cache_sb_k.shape[2:]
    n_fx = cache_fox_k.shape[3]
    assert hd == HEAD_DIM and n_sb % 2 == 0 and n_fx % 2 == 0
    assert tp % ROW_BLOCK == 0 and tp % ATT_BLOCK == 0 and ROW_BLOCK % ts == 0 and bs % (ROW_BLOCK // ts) == 0
    assert past % ATT_BLOCK == 0
    sb_w, fx_w = n_sb * hd, n_fx * hd
    att = min(FOX_BLOCK, tp)
    past_pad = -(-(past + ts) // (3 * LANES)) * (3 * LANES)
    row = lambda a: a.reshape(1, -1)

    xp, xs = x_prompt, x_sample
    outs = [[] for _ in range(10)]
    for l in range(depth):
        in_cols = w_in.shape[2]
        w16t = jnp.pad(jnp.swapaxes(w_in[l], 0, 1), ((0, -in_cols % BF16_ROWS), (0, 0))).astype(BF16)
        bfr, bfc = row(b_f[l]), b_f[l].reshape(-1, 1)
        wo16, wg16 = w_o[l].astype(BF16), w_gate[l].astype(BF16)
        wu16, wd16 = w_up[l].astype(BF16), w_down[l].astype(BF16)

        c_all = jnp.concatenate([c_prompt, c_sample], axis=0)
        ada = _ada(c_all, w_ada[l], row(b_ada[l])).reshape(bp + bs, 6, d)
        ada_p, ada_s = ada[:bp], ada[bp:]

        (q_sb, k_sb, v_sb, k_sb16, v_sb16, q_fx, k_fx, v_fx, k_fx16, v_fx16, lf) = _pre(
            xp, ada_p, row(g_mix[l]), w16t, bfr, bfc, sb_w, fx_w, n_fx, True)
        o_sb = _sb_prompt(q_sb, k_sb16, v_sb16)
        f_cum = _cumsum_rows(lf.reshape(bp * n_fx, tp), ROW_BLOCK)
        fq = jnp.swapaxes(f_cum.reshape(bp, n_fx, tp), 1, 2)
        fk = jnp.swapaxes(f_cum.reshape(bp, n_fx // 2, 2, tp // att, att), 2, 3)
        o_fx = _fox_prompt(q_fx, k_fx16, v_fx16, fq, fk)
        xp = _post(xp, o_sb, o_fx, ada_p, row(g_sb_out[l]), row(g_fox_out[l]), wo16, row(g_ffn[l]),
                   wg16, wu16, wd16, row(g_final), l + 1 == depth)
        for dst, a in zip(outs[0:4], (k_sb, v_sb, k_fx, v_fx)):
            dst.append(jnp.swapaxes(a, 1, 2).reshape(bp, tp, -1, hd))
        outs[4].append(jnp.swapaxes(lf, 1, 2))

        (q_sb, k_sb, v_sb, k_sb16, v_sb16, q_fx, k_fx, v_fx, k_fx16, v_fx16, lf) = _pre(
            xs, ada_s, row(g_mix[l]), w16t, bfr, bfc, sb_w, fx_w, n_fx, False)
        time_minor = lambda a: jnp.swapaxes(a.reshape(a.shape[0], a.shape[1], -1), 1, 2)
        lf_all = jnp.concatenate([cache_fox_logf[l].astype(F32), lf], axis=1)
        lf_rows = _pad_lanes(jnp.swapaxes(lf_all, 1, 2).reshape(bs * n_fx, past + ts), past_pad)
        f_all = _cumsum_rows(lf_rows, past_pad // 3)
        fq = jnp.swapaxes(f_all.reshape(bs, n_fx, past_pad)[:, :, past:past + ts], 1, 2)
        fk = f_all.reshape(bs, n_fx // 2, 2, past_pad)
        o_sb, o_fx = _sample_attention(
            (q_sb, time_minor(k_sb16), time_minor(v_sb16), time_minor(cache_sb_k[l]), time_minor(cache_sb_v[l])),
            (q_fx, time_minor(k_fx16), time_minor(v_fx16), time_minor(cache_fox_k[l]), time_minor(cache_fox_v[l])),
            fq, fk)
        xs = _post(xs, o_sb, o_fx, ada_s, row(g_sb_out[l]), row(g_fox_out[l]), wo16, row(g_ffn[l]),
                   wg16, wu16, wd16, row(g_final), l + 1 == depth)
        for dst, a in zip(outs[5:9], (k_sb, v_sb, k_fx, v_fx)):
            dst.append(a.reshape(bs, ts, -1, hd))
        outs[9].append(lf)

    return (xp, xs) + tuple(jnp.stack(o) for o in outs)
```

```python
import functools

import jax
import jax.numpy as jnp
from jax import lax
from jax.experimental import pallas as pl
from jax.experimental.pallas import tpu as pltpu

F32 = jnp.float32
BF16 = jnp.bfloat16
EPS = 1e-6
HEAD_DIM = 64
LANES = 128
BF16_ROWS = 16
ROW_BLOCK = 512
FF_CHUNK = 512
ATT_BLOCK = 256
SB_BLOCKS_PER_STEP = 8
FOX_BLOCKS_PER_STEP = 2
SAMPLE_ELEMS_PER_STEP = 2
FOX_BLOCK = 512
NEG_BIG = -1e30
EXP_ZERO = -105.0
NORM_CHUNK = 2048
NORM_SLACK = 1.0 + 2.0 ** -8
VMEM_LIMIT = 56 * 1024 * 1024


def _params(*semantics):
    return pltpu.CompilerParams(dimension_semantics=semantics, vmem_limit_bytes=VMEM_LIMIT)


def _dot(a, b):
    return jnp.dot(a, b, preferred_element_type=F32)


def _dot_nt(a, b):
    return lax.dot_general(a, b, (((1,), (1,)), ((), ())), preferred_element_type=F32)


def _rms(x, g):
    return x * lax.rsqrt(jnp.mean(x * x, axis=-1, keepdims=True) + EPS) * g


def _split_bf16(x, parts):
    out = []
    for _ in range(parts - 1):
        p = x.astype(BF16)
        out.append(p)
        x = x - p.astype(F32)
    out.append(x.astype(BF16))
    return out


def _dot_split(x, m, parts):
    acc = None
    for p in _split_bf16(x, parts):
        d = _dot(p, m)
        acc = d if acc is None else acc + d
    return acc


def _keep_head(x, first, axis=1, fill=0.0):
    pos = lax.broadcasted_iota(jnp.int32, x.shape, axis)
    keep = (pos < HEAD_DIM) if first else (pos >= HEAD_DIM)
    return jnp.where(keep, x, jnp.full_like(x, fill))


def _per_head(col0, col1, shape):
    lane = lax.broadcasted_iota(jnp.int32, shape, 1)
    return jnp.where(lane < HEAD_DIM, col0, col1)


def _softmax_out(acc0, acc1):
    first = lax.broadcasted_iota(jnp.int32, acc0.shape, 1) < HEAD_DIM
    return jnp.where(first, acc0, acc1) / pltpu.roll(jnp.where(first, acc1, acc0), LANES // 2, 1)


def _lanes_to(x, n):
    if n <= LANES:
        return x[:, :n]
    return jnp.concatenate([x] * (n // LANES), axis=1)


def _ada_kernel(c_ref, w_ref, b_ref, o_ref):
    c = c_ref[...]
    s = (c * (1.0 / (1.0 + jnp.exp(-c)))).astype(BF16)
    o_ref[...] = _dot(s, w_ref[...].astype(BF16)) + b_ref[...]


def _ada(c, w, b):
    rows, d = c.shape
    n = w.shape[1]
    tn = 1024
    return pl.pallas_call(
        _ada_kernel,
        grid=(n // tn,),
        in_specs=[pl.BlockSpec((rows, d), lambda j: (0, 0)),
                  pl.BlockSpec((d, tn), lambda j: (0, j)),
                  pl.BlockSpec((1, tn), lambda j: (0, j))],
        out_specs=pl.BlockSpec((rows, tn), lambda j: (0, j)),
        out_shape=jax.ShapeDtypeStruct((rows, n), F32),
        compiler_params=_params("arbitrary"),
        name="ada",
    )(c, w, b)


def _log_sigmoid(u):
    return jnp.minimum(u, 0.0) - jnp.log(1.0 + jnp.exp(-jnp.abs(u)))


def _pre_kernel(x_ref, ada_ref, g_ref, wt_ref, bf_ref, bfc_ref,
                qsb_ref, ksb_ref, vsb_ref, ksb16_ref, vsb16_ref,
                qfx_ref, kfx_ref, vfx_ref, kfx16_ref, vfx16_ref, lf_ref, *, sb_w, fx_w, n_f, time_minor):
    x = x_ref[...]
    nb, tt, d = x.shape
    ada = ada_ref[...]
    h = _rms(x, g_ref[...]) * (1.0 + ada[:, 1:2, :]) + ada[:, 0:1, :]
    h = h.reshape(nb * tt, d).astype(BF16)
    q_scale = HEAD_DIM ** -0.5

    def proj(lo, width):
        return _dot_nt(h, wt_ref[lo:lo + width, :]).reshape(nb, tt, width)

    base = 3 * sb_w
    qsb_ref[...] = (proj(0, sb_w) * q_scale).astype(BF16)
    qfx_ref[...] = (proj(base, fx_w) * q_scale).astype(BF16)
    if time_minor:
        sb = _dot_nt(wt_ref[sb_w:base, :], h)
        fx = _dot_nt(wt_ref[base + fx_w:, :], h)
        parts = (sb[:sb_w], sb[sb_w:], fx[:fx_w], fx[fx_w:2 * fx_w])
        lf_ref[...] = _log_sigmoid(fx[2 * fx_w:2 * fx_w + n_f] + bfc_ref[...])[None]
    else:
        parts = (proj(sb_w, sb_w), proj(2 * sb_w, sb_w), proj(base + fx_w, fx_w), proj(base + 2 * fx_w, fx_w))
        forget = _dot_nt(h, wt_ref[base + 3 * fx_w:, :])
        lf_ref[...] = _log_sigmoid(forget[:, :n_f] + bf_ref[...]).reshape(nb, tt, n_f)
    for r, out, out16 in zip(parts, (ksb_ref, vsb_ref, kfx_ref, vfx_ref),
                             (ksb16_ref, vsb16_ref, kfx16_ref, vfx16_ref)):
        r = r[None] if time_minor else r
        out[...] = r
        out16[...] = r.astype(BF16)


def _pre(x, ada, g, w16t, bfr, bfc, sb_w, fx_w, n_f, time_minor):
    b, t, d = x.shape
    tt = min(t, ROW_BLOCK)
    nb = max(1, ROW_BLOCK // tt)
    assert nb == 1 or not time_minor
    grid = (b // nb, t // tt)
    tok = lambda w: pl.BlockSpec((nb, tt, w), lambda i, j: (i, j, 0))
    const = lambda a: pl.BlockSpec(a.shape, lambda i, j: (0, 0))
    shape = lambda w, dt: jax.ShapeDtypeStruct((b, t, w), dt)
    if time_minor:
        kv = lambda w: pl.BlockSpec((1, w, tt), lambda i, j: (i, 0, j))
        kv_shape = lambda w, dt: jax.ShapeDtypeStruct((b, w, t), dt)
    else:
        kv, kv_shape = tok, shape
    return pl.pallas_call(
        functools.partial(_pre_kernel, sb_w=sb_w, fx_w=fx_w, n_f=n_f, time_minor=time_minor),
        grid=grid,
        in_specs=[tok(d), pl.BlockSpec((nb, 6, d), lambda i, j: (i, 0, 0)),
                  const(g), const(w16t), const(bfr), const(bfc)],
        out_specs=[tok(sb_w)] + [kv(sb_w)] * 4 + [tok(fx_w)] + [kv(fx_w)] * 4 + [kv(n_f)],
        out_shape=[shape(sb_w, BF16), kv_shape(sb_w, F32), kv_shape(sb_w, F32), kv_shape(sb_w, BF16),
                   kv_shape(sb_w, BF16),
                   shape(fx_w, BF16), kv_shape(fx_w, F32), kv_shape(fx_w, F32), kv_shape(fx_w, BF16),
                   kv_shape(fx_w, BF16),
                   kv_shape(n_f, F32)],
        compiler_params=_params("parallel", "parallel"),
        name="pre_mixer",
    )(x, ada, g, w16t, bfr, bfc)


def _cumsum_kernel(x_ref, o_ref, *, tc):
    r = lax.broadcasted_iota(jnp.int32, (tc, tc), 0)
    c = lax.broadcasted_iota(jnp.int32, (tc, tc), 1)
    upper = jnp.where(r <= c, 1.0, 0.0).astype(BF16)
    ones = jnp.ones((tc, LANES), BF16)
    carry = jnp.zeros((x_ref.shape[0], LANES), F32)
    for lo in range(0, x_ref.shape[1], tc):
        x = x_ref[:, lo:lo + tc]
        o_ref[:, lo:lo + tc] = _dot_split(x, upper, 3) + _lanes_to(carry, tc)
        carry = carry + _dot_split(x, ones, 3)


def _cumsum_rows(x, tc):
    return pl.pallas_call(
        functools.partial(_cumsum_kernel, tc=tc),
        out_shape=jax.ShapeDtypeStruct(x.shape, F32),
        compiler_params=pltpu.CompilerParams(vmem_limit_bytes=VMEM_LIMIT),
        name="cumsum_time",
    )(x)


def _tri_strict(n):
    r = lax.broadcasted_iota(jnp.int32, (n, n), 0)
    c = lax.broadcasted_iota(jnp.int32, (n, n), 1)
    return jnp.where(r > c, 1.0, 0.0).astype(BF16)


def _causal(tq, tk, strict):
    r = lax.broadcasted_iota(jnp.int32, (tq, tk), 0)
    c = lax.broadcasted_iota(jnp.int32, (tq, tk), 1)
    return (c < r) if strict else (c <= r)


def _sb_tiles(chains, mask):
    tri = _tri_strict(chains[0][1].shape[1])
    zs = [_dot(q, k) for q, k, _, _ in chains]
    staged = []
    for z in zs:
        log_beta = jnp.minimum(z, 0.0) - jnp.log(1.0 + jnp.exp(-jnp.abs(z)))
        log_keep = log_beta - z
        if mask is not None:
            log_keep = jnp.where(mask, log_keep, 0.0)
        staged.append((log_beta, log_keep[:, 0:1], log_keep.astype(BF16)))
    betweens = [_dot(terms, tri) for _, _, terms in staged]
    weights = []
    for (log_beta, _, _), between, (_, _, _, carry) in zip(staged, betweens, chains):
        a = jnp.exp(log_beta + between + carry)
        if mask is not None:
            a = jnp.where(mask, a, 0.0)
        weights.append(a.astype(BF16))
    pvs = [_dot_nt(a, v) for a, (_, _, v, _) in zip(weights, chains)]
    return [(pv, carry + (between[:, 0:1] + first))
            for pv, between, (_, first, _), (_, _, _, carry) in zip(pvs, betweens, staged, chains)]


def _sb_start(rows, groups):
    col = jnp.zeros((rows, 1), F32)
    return (col, col, jnp.zeros((rows, LANES), F32)) * groups


def _sb_visit(qs, kvs, state, mask):
    chains = []
    for g, ((q0, q1), (k, v)) in enumerate(zip(qs, kvs)):
        chains.append((q0, k, _keep_head(v, True, 0), state[3 * g]))
        chains.append((q1, k, _keep_head(v, False, 0), state[3 * g + 1]))
    res = _sb_tiles(chains, mask)
    out = ()
    for g in range(len(qs)):
        (pv0, c0), (pv1, c1) = res[2 * g], res[2 * g + 1]
        out += (c0, c1, state[3 * g + 2] + pv0 + pv1)
    return out


def _sb_sweep(visit, lefts, state):
    groups = len(lefts)

    def live(s):
        alive = jnp.bool_(False)
        for g in range(groups):
            more = jnp.max(jnp.maximum(s[1 + 3 * g], s[2 + 3 * g])) > EXP_ZERO
            alive = jnp.logical_or(alive, jnp.logical_and(s[0] < lefts[g], more))
        return alive

    def step(s):
        i, old = s[0], s[1:]
        new = visit([jnp.maximum(lefts[g] - 1 - i, 0) for g in range(groups)], old, None)
        out = (i + 1,)
        for g in range(groups):
            inside = i < lefts[g]
            out += tuple(jnp.where(inside, n, o) for n, o in zip(new[3 * g:3 * g + 3], old[3 * g:3 * g + 3]))
        return out

    return lax.while_loop(live, step, (jnp.int32(0),) + tuple(state))[1:]


def _head_column(f, h):
    lane = lax.broadcasted_iota(jnp.int32, f.shape, 1)
    return jnp.sum(jnp.where(lane == h, f, 0.0), axis=1, keepdims=True)


def _sb_prompt_kernel(q_ref, k_ref, v_ref, o_ref, *, tile, blocks):
    first = pl.program_id(2) * blocks
    qs = []
    for j in range(blocks):
        q = q_ref[0, j * tile:(j + 1) * tile, :]
        qs.append((_keep_head(q, True), _keep_head(q, False)))

    def visit(kbs, state, mask):
        kvs = []
        for j in range(blocks):
            start = pl.multiple_of(kbs[j] * tile, tile)
            kvs.append((k_ref[0, :, pl.ds(start, tile)], v_ref[0, :, pl.ds(start, tile)]))
        return _sb_visit(qs, kvs, state, mask)

    lefts = [first + j for j in range(blocks)]
    state = visit(lefts, _sb_start(tile, blocks), _causal(tile, tile, True))
    state = _sb_sweep(visit, lefts, state)
    for j in range(blocks):
        o_ref[0, j * tile:(j + 1) * tile, :] = state[2 + 3 * j]


def _head_norms_sq(x16):
    x = x16.astype(F32)
    r = lax.broadcasted_iota(jnp.int32, (LANES, LANES), 0)
    c = lax.broadcasted_iota(jnp.int32, (LANES, LANES), 1)
    same_head = jnp.where((r < HEAD_DIM) == (c < HEAD_DIM), 1.0, 0.0).astype(BF16)
    return _dot_split(x * x, same_head, 2)


def _fox_prompt_kernel(q_ref, k_ref, v_ref, fq_ref, fk_ref, o_ref, kmax_ref, *, tile, blocks):
    hp = pl.program_id(1)
    first = pl.program_id(2) * blocks
    shape = (tile, LANES)
    half_tile = tile // 2
    causal = _causal(tile, tile, False)
    zero = jnp.zeros(shape, F32)
    every = range(blocks)

    @pl.when(first == 0)
    def _():
        width = min(NORM_CHUNK, k_ref.shape[2])

        def chunk(i, best):
            kc = k_ref[0, :, pl.ds(pl.multiple_of(i * width, width), width)].astype(F32)
            sq = kc * kc
            return (jnp.maximum(best[0], jnp.sum(sq[:HEAD_DIM], axis=0, keepdims=True)),
                    jnp.maximum(best[1], jnp.sum(sq[HEAD_DIM:], axis=0, keepdims=True)))

        none = jnp.zeros((1, width), F32)
        best = lax.fori_loop(0, k_ref.shape[2] // width, chunk, (none, none))
        kmax_ref[...] = _per_head(jnp.max(best[0], axis=1, keepdims=True),
                                  jnp.max(best[1], axis=1, keepdims=True), (1, LANES))

    kmax = jnp.sqrt(kmax_ref[...]) * NORM_SLACK
    qis = [first + j for j in every]
    qs, fqs, reach = [], [], []
    for j in every:
        rows = slice(j * tile, (j + 1) * tile)
        q = q_ref[0, rows, :]
        qs.append((_keep_head(q, True), _keep_head(q, False)))
        fq = fq_ref[0, rows, :]
        fqs.append((_head_column(fq, 2 * hp), _head_column(fq, 2 * hp + 1)))
        reach.append(jnp.sqrt(_head_norms_sq(q)) * kmax + _per_head(fqs[j][0], fqs[j][1], shape))

    def tile_left(j, i):
        return jnp.maximum(qis[j] - 1 - i, 0)

    def key_cols(kb):
        return pl.ds(pl.multiple_of(kb * tile, tile), tile)

    def last_fk(kb):
        fk = fk_ref[0, 0, kb]
        return _per_head(fk[0:1, tile - 1:tile], fk[1:2, tile - 1:tile], (1, LANES))

    def scores(j, kb):
        k = k_ref[0, :, key_cols(kb)]
        fk = fk_ref[0, 0, kb]
        return _dot(qs[j][0], k) - fk[0:1, :], _dot(qs[j][1], k) - fk[1:2, :]

    def diagonal_scores(j):
        k = k_ref[0, :, key_cols(qis[j])]
        fk = fk_ref[0, 0, qis[j]]
        raw = [(_dot(q, k[:, :half_tile]) - fk[r:r + 1, :half_tile],
                _dot(q[half_tile:], k[:, half_tile:]) - fk[r:r + 1, half_tile:]) for r, q in enumerate(qs[j])]
        unseen = jnp.full((half_tile, half_tile), NEG_BIG, F32)
        return [(jnp.where(causal[:, :half_tile], left, NEG_BIG),
                 jnp.concatenate([unseen, jnp.where(causal[half_tile:, half_tile:], right, NEG_BIG)], axis=0))
                for left, right in raw]

    def best_gap(j, kb, m0, m1):
        return jnp.max(reach[j] - _per_head(m0, m1, shape) - last_fk(kb))

    def max_live(s):
        alive = jnp.bool_(False)
        for j in every:
            alive = jnp.logical_or(alive, jnp.logical_and(s[0] < qis[j], s[1 + j] > 0.0))
        return alive

    def max_step(s, us=None):
        i, ms = s[0], s[1 + blocks:]
        if us is None:
            us = [scores(j, tile_left(j, i)) for j in every]
        gaps, new = (), ()
        for j in every:
            inside = i < qis[j]
            both = [jnp.where(inside, jnp.maximum(ms[2 * j + r], jnp.max(us[j][r], axis=1, keepdims=True) + fqs[j][r]),
                              ms[2 * j + r]) for r in range(2)]
            gaps += (best_gap(j, tile_left(j, i + 1), both[0], both[1]),)
            new += tuple(both)
        return (i + 1,) + gaps + new

    diagonal = [diagonal_scores(j) for j in every]
    ms = ()
    for j in every:
        ms += tuple(jnp.maximum(jnp.max(left, axis=1, keepdims=True), jnp.max(right, axis=1, keepdims=True)) + fq
                    for (left, right), fq in zip(diagonal[j], fqs[j]))
    gaps = tuple(best_gap(j, tile_left(j, 0), ms[2 * j], ms[2 * j + 1]) for j in every)
    nearest = [scores(j, tile_left(j, 0)) for j in every]
    ms = lax.while_loop(max_live, max_step, max_step((jnp.int32(0),) + gaps + ms, nearest))[1 + blocks:]

    cs = [(ms[2 * j] - fqs[j][0], ms[2 * j + 1] - fqs[j][1]) for j in every]
    gap_rows = [jnp.max(reach[j] - _per_head(ms[2 * j], ms[2 * j + 1], shape), axis=0, keepdims=True) for j in every]

    def head_gaps(j, kb):
        gap = gap_rows[j] - last_fk(kb)
        head0 = lax.broadcasted_iota(jnp.int32, gap.shape, 1) < HEAD_DIM
        return jnp.max(jnp.where(head0, gap, NEG_BIG)), jnp.max(jnp.where(head0, NEG_BIG, gap))

    def with_ones(v, head1):
        ones_rows = (lax.broadcasted_iota(jnp.int32, v.shape, 0) < HEAD_DIM) == head1
        return jnp.where(ones_rows, jnp.ones_like(v), v)

    blank = jnp.zeros((half_tile, LANES), F32)
    state = ()
    for j in every:
        v = v_ref[0, :, key_cols(qis[j])]
        weights = [(jnp.exp(left - c).astype(BF16), jnp.exp(right - c).astype(BF16))
                   for (left, right), c in zip(diagonal[j], cs[j])]
        accs = [_dot_nt(left, vh[:, :half_tile])
                + jnp.concatenate([blank, _dot_nt(right[half_tile:], vh[:, half_tile:])], axis=0)
                for (left, right), vh in zip(weights, (with_ones(v, False), with_ones(v, True)))]
        state += (jnp.int32(0),) + head_gaps(j, tile_left(j, 0)) + tuple(accs)

    def pair_active(s, j):
        done, gap0, gap1 = s[1 + 5 * j:4 + 5 * j]
        return jnp.logical_and(s[0] < qis[j], jnp.minimum(gap0, gap1) > EXP_ZERO)

    def pair_live(s):
        alive = jnp.bool_(False)
        for j in every:
            alive = jnp.logical_or(alive, pair_active(s, j))
        return alive

    def pair_step(s, us=None):
        i = s[0]
        kbs = [tile_left(j, i) for j in every]
        if us is None:
            us = [scores(j, kbs[j]) for j in every]
        ps =[[jnp.exp(u - c).astype(BF16) for u, c in zip(us[j], cs[j])] for j in every]
        out = (i + 1,)
        for j in every:
            done, gap0, gap1, acc0, acc1 = s[1 + 5 * j:6 + 5 * j]
            active = pair_active(s, j)
            v = v_ref[0, :, key_cols(kbs[j])]
            new0, new1 = head_gaps(j, tile_left(j, i + 1))
            out += (jnp.where(active, i + 1, done), jnp.where(active, new0, gap0), jnp.where(active, new1, gap1),
                    acc0 + jnp.where(active, _dot_nt(ps[j][0], with_ones(v, False)), 0.0),
                    acc1 + jnp.where(active, _dot_nt(ps[j][1], with_ones(v, True)), 0.0))
        return out

    state = lax.while_loop(pair_live, pair_step, pair_step((jnp.int32(0),) + state, nearest))[1:]

    seconds = [state[5 * j + 2] > state[5 * j + 1] for j in every]
    starts = [state[5 * j] for j in every]
    q_lone = [jnp.where(seconds[j], qs[j][1], qs[j][0]) for j in every]
    c_lone = [jnp.where(seconds[j], cs[j][1], cs[j][0]) for j in every]

    def lone_active(s, j):
        return jnp.logical_and(starts[j] + s[0] < qis[j], s[1 + 2 * j] > EXP_ZERO)

    def lone_live(s):
        alive = jnp.bool_(False)
        for j in every:
            alive = jnp.logical_or(alive, lone_active(s, j))
        return alive

    def lone_step(s):
        t = s[0]
        kbs = [tile_left(j, starts[j] + t) for j in every]
        us = []
        for j in every:
            k = k_ref[0, :, key_cols(kbs[j])]
            fk = fk_ref[0, 0, kbs[j]]
            fk_s = jnp.where(seconds[j], fk[1:2, :], fk[0:1, :])
            us.append([_dot(q_lone[j], k[:, lo:lo + half_tile]) - fk_s[:, lo:lo + half_tile] for lo in (0, half_tile)])
        ps = [[jnp.exp(u - c_lone[j]).astype(BF16) for u in us[j]] for j in every]
        out = (t + 1,)
        for j in every:
            gap, acc = s[1 + 2 * j:3 + 2 * j]
            active = lone_active(s, j)
            v_s = with_ones(v_ref[0, :, key_cols(kbs[j])], seconds[j])
            gaps = head_gaps(j, tile_left(j, starts[j] + t + 1))
            more = _dot_nt(ps[j][0], v_s[:, :half_tile]) + _dot_nt(ps[j][1], v_s[:, half_tile:])
            out += (jnp.where(active, jnp.where(seconds[j], gaps[1], gaps[0]), gap), acc + jnp.where(active, more, 0.0))
        return out

    lone = (jnp.int32(0),)
    for j in every:
        lone += (jnp.maximum(state[5 * j + 1], state[5 * j + 2]), zero)
    lone = lax.while_loop(lone_live, lone_step, lone)[1:]

    for j in every:
        acc0 = state[5 * j + 3] + jnp.where(seconds[j], 0.0, lone[2 * j + 1])
        acc1 = state[5 * j + 4] + jnp.where(seconds[j], lone[2 * j + 1], 0.0)
        o_ref[0, j * tile:(j + 1) * tile, :] = _softmax_out(acc0, acc1)


def _prompt_specs(t, tile):
    qspec = pl.BlockSpec((1, tile, LANES), lambda b, h, i: (b, i, h))
    kvspec = pl.BlockSpec((1, LANES, t), lambda b, h, i: (b, h, 0))
    return qspec, kvspec


def _sb_prompt(q, k, v):
    b, t, w = q.shape
    tile = ATT_BLOCK
    blocks = SB_BLOCKS_PER_STEP if t % (tile * SB_BLOCKS_PER_STEP) == 0 else 1
    qspec, kvspec = _prompt_specs(t, tile * blocks)
    return pl.pallas_call(
        functools.partial(_sb_prompt_kernel, tile=tile, blocks=blocks),
        grid=(b, w // LANES, t // (tile * blocks)),
        in_specs=[qspec, kvspec, kvspec],
        out_specs=qspec,
        out_shape=jax.ShapeDtypeStruct((b, t, w), F32),
        compiler_params=_params("parallel", "parallel", "arbitrary"),
        name="sb_prompt",
    )(q, k, v)


def _fox_prompt(q, k, v, fq, fk):
    b, t, w = q.shape
    tile = fk.shape[-1]
    blocks = FOX_BLOCKS_PER_STEP if t % (tile * FOX_BLOCKS_PER_STEP) == 0 else 1
    qspec, kvspec = _prompt_specs(t, tile * blocks)
    n_f = fq.shape[-1]
    return pl.pallas_call(
        functools.partial(_fox_prompt_kernel, tile=tile, blocks=blocks),
        grid=(b, w // LANES, t // (tile * blocks)),
        in_specs=[qspec, kvspec, kvspec,
                  pl.BlockSpec((1, tile * blocks, n_f), lambda b, h, i: (b, i, 0)),
                  pl.BlockSpec((1, 1, t // tile, 2, tile), lambda b, h, i: (b, h, 0, 0, 0))],
        out_specs=qspec,
        out_shape=jax.ShapeDtypeStruct((b, t, w), F32),
        scratch_shapes=[pltpu.VMEM((1, LANES), F32)],
        compiler_params=_params("parallel", "parallel", "arbitrary"),
        name="fox_prompt",
    )(q, k, v, fq, fk)


def _group_rows(g):
    return slice(g * LANES, (g + 1) * LANES)


def _sb_sample_kernel(q_ref, kn_ref, vn_ref, kc_ref, vc_ref, o_ref, *, tile):
    elems, n, width = q_ref.shape
    pairs = [(e, g) for e in range(elems) for g in range(width // LANES)]
    past = kc_ref.shape[2]
    qs = []
    for e, g in pairs:
        q = q_ref[e, :, _group_rows(g)]
        qs.append((_keep_head(q, True), _keep_head(q, False)))

    def visit(kbs, state, mask):
        kvs = []
        for i, (e, g) in enumerate(pairs):
            if mask is not None:
                kvs.append((kn_ref[e, _group_rows(g), :], vn_ref[e, _group_rows(g), :]))
            else:
                cols = pl.ds(pl.multiple_of(kbs[i] * tile, tile), tile)
                kvs.append((kc_ref[e, _group_rows(g), cols].astype(BF16),
                            vc_ref[e, _group_rows(g), cols].astype(BF16)))
        return _sb_visit(qs, kvs, state, mask)

    state = visit(None, _sb_start(n, len(pairs)), _causal(n, n, True))
    state = _sb_sweep(visit, [past // tile] * len(pairs), state)
    for i, (e, g) in enumerate(pairs):
        o_ref[e, :, _group_rows(g)] = state[2 + 3 * i]


def _fox_sample_kernel(q_ref, kn_ref, vn_ref, kc_ref, vc_ref, fq_ref, fk_ref, o_ref):
    elems, n, width = q_ref.shape
    pairs = [(e, g) for e in range(elems) for g in range(width // LANES)]
    past = kc_ref.shape[2]
    causal = _causal(n, n, False)
    scored = []
    for e, g in pairs:
        q = q_ref[e, :, _group_rows(g)]
        kn = kn_ref[e, _group_rows(g), :]
        kc = kc_ref[e, _group_rows(g), :].astype(BF16)
        for h in range(2):
            qh = _keep_head(q, h == 0)
            fk = fk_ref[e, g, h:h + 1, :]
            u_new = jnp.where(causal, _dot(qh, kn) - fk[:, past:past + n], NEG_BIG)
            u_old = _dot(qh, kc) - fk[:, 0:past]
            scored.append((u_new, u_old, fq_ref[e, :, 2 * g + h:2 * g + h + 1]))
    weights = []
    for u_new, u_old, fq_col in scored:
        m = jnp.maximum(jnp.max(u_new, axis=1, keepdims=True), jnp.max(u_old, axis=1, keepdims=True)) + fq_col
        c = m - fq_col
        weights.append((jnp.exp(u_new - c).astype(BF16), jnp.exp(u_old - c).astype(BF16)))
    for i, (e, g) in enumerate(pairs):
        vn = vn_ref[e, _group_rows(g), :]
        vc = vc_ref[e, _group_rows(g), :].astype(BF16)
        accs = []
        for h in range(2):
            p_new, p_old = weights[2 * i + h]
            accs.append(_dot_nt(p_new, _keep_head(vn, h == 0, 0, 1.0))
                        + _dot_nt(p_old, _keep_head(vc, h == 0, 0, 1.0)))
        o_ref[e, :, _group_rows(g)] = _softmax_out(accs[0], accs[1])


def _sample_specs(elems, n, width, past):
    qspec = pl.BlockSpec((elems, n, width), lambda b: (b, 0, 0))
    new = pl.BlockSpec((elems, width, n), lambda b: (b, 0, 0))
    cache = pl.BlockSpec((elems, width, past), lambda b: (b, 0, 0))
    return qspec, new, cache


def _sample_kernel(sb_q, sb_kn, sb_vn, sb_kc, sb_vc, fx_q, fx_kn, fx_vn, fx_kc, fx_vc, fq_ref, fk_ref,
                   sb_o, fx_o, *, tile):
    _sb_sample_kernel(sb_q, sb_kn, sb_vn, sb_kc, sb_vc, sb_o, tile=tile)
    _fox_sample_kernel(fx_q, fx_kn, fx_vn, fx_kc, fx_vc, fq_ref, fk_ref, fx_o)


def _sample_attention(sb, fx, fq, fk):
    b, n, sb_w = sb[0].shape
    fx_w = fx[0].shape[2]
    elems = SAMPLE_ELEMS_PER_STEP if b % SAMPLE_ELEMS_PER_STEP == 0 else 1
    sb_q, sb_new, sb_cache = _sample_specs(elems, n, sb_w, sb[3].shape[2])
    fx_q, fx_new, fx_cache = _sample_specs(elems, n, fx_w, fx[3].shape[2])
    return pl.pallas_call(
        functools.partial(_sample_kernel, tile=ATT_BLOCK),
        grid=(b // elems,),
        in_specs=[sb_q, sb_new, sb_new, sb_cache, sb_cache, fx_q, fx_new, fx_new, fx_cache, fx_cache,
                  pl.BlockSpec((elems, n, fq.shape[-1]), lambda b: (b, 0, 0)),
                  pl.BlockSpec((elems,) + fk.shape[1:], lambda b: (b, 0, 0, 0))],
        out_specs=[sb_q, fx_q],
        out_shape=[jax.ShapeDtypeStruct((b, n, sb_w), F32), jax.ShapeDtypeStruct((b, n, fx_w), F32)],
        compiler_params=_params("parallel"),
        name="sample_attention",
    )(*sb, *fx, fq, fk)


def _post_kernel(x_ref, osb_ref, ofx_ref, ada_ref, gsb_ref, gfx_ref, wo_ref, gffn_ref,
                 wg_ref, wu_ref, wd_ref, gfin_ref, y_ref, *, final_norm):
    nb, tt, d = x_ref.shape
    rows = nb * tt
    ada = ada_ref[...]
    o_sb = _rms(osb_ref[...], gsb_ref[...])
    o_fx = _rms(ofx_ref[...], gfx_ref[...])
    sb_w, fx_w = o_sb.shape[-1], o_fx.shape[-1]
    proj = (_dot(o_sb.reshape(rows, sb_w).astype(BF16), wo_ref[0:sb_w, :])
            + _dot(o_fx.reshape(rows, fx_w).astype(BF16), wo_ref[sb_w:sb_w + fx_w, :]))
    x2 = x_ref[...] + (1.0 + ada[:, 2:3, :]) * proj.reshape(nb, tt, d)
    h = _rms(x2, gffn_ref[...]) * (1.0 + ada[:, 4:5, :]) + ada[:, 3:4, :]
    h = h.reshape(rows, d).astype(BF16)
    d_ff = wg_ref.shape[1]
    f = None
    for lo in range(0, d_ff, FF_CHUNK):
        hi = min(lo + FF_CHUNK, d_ff)
        g = _dot(h, wg_ref[:, lo:hi])
        u = _dot(h, wu_ref[:, lo:hi])
        act = (g * (1.0 / (1.0 + jnp.exp(-g))) * u).astype(BF16)
        part = _dot(act, wd_ref[lo:hi, :])
        f = part if f is None else f + part
    x3 = x2 + (1.0 + ada[:, 5:6, :]) * f.reshape(nb, tt, d)
    y_ref[...] = _rms(x3, gfin_ref[...]) if final_norm else x3


def _post(x, o_sb, o_fx, ada, g_sb, g_fx, wo16, g_ffn, wg16, wu16, wd16, g_fin, final_norm):
    b, t, d = x.shape
    tt = min(t, ROW_BLOCK)
    nb = max(1, ROW_BLOCK // tt)
    tok = lambda w: pl.BlockSpec((nb, tt, w), lambda i, j: (i, j, 0))
    const = lambda a: pl.BlockSpec(a.shape, lambda i, j: (0, 0), pipeline_mode=pl.Buffered(1))
    return pl.pallas_call(
        functools.partial(_post_kernel, final_norm=final_norm),
        grid=(b // nb, t // tt),
        in_specs=[tok(d), tok(o_sb.shape[-1]), tok(o_fx.shape[-1]),
                  pl.BlockSpec((nb, 6, d), lambda i, j: (i, 0, 0)),
                  const(g_sb), const(g_fx), const(wo16), const(g_ffn),
                  const(wg16), const(wu16), const(wd16), const(g_fin)],
        out_specs=tok(d),
        out_shape=jax.ShapeDtypeStruct((b, t, d), F32),
        compiler_params=_params("parallel", "parallel"),
        name="post_mixer_ffn",
    )(x, o_sb, o_fx, ada, g_sb, g_fx, wo16, g_ffn, wg16, wu16, wd16, g_fin)


def _pad_lanes(a, n):
    return jnp.pad(a, ((0, 0), (0, n - a.shape[1])))


def kernel(x_prompt, x_sample, c_prompt, c_sample, cache_sb_k, cache_sb_v, cache_fox_k, cache_fox_v, cache_fox_logf, w_ada, b_ada, g_mix, w_in, b_f, g_sb_out, g_fox_out, w_o, g_ffn, w_gate, w_up, w_down, g_final):
    depth = w_ada.shape[0]
    bp, tp, d = x_prompt.shape
    bs, ts, _ = x_sample.shape
    past, n_sb, hd = cache_sb_k.shape[2:]
    n_fx = cache_fox_k.shape[3]
    assert hd == HEAD_DIM and n_sb % 2 == 0 and n_fx % 2 == 0
    assert tp % ROW_BLOCK == 0 and tp % ATT_BLOCK == 0 and ROW_BLOCK % ts == 0 and bs % (ROW_BLOCK // ts) == 0
    assert past % ATT_BLOCK == 0
    sb_w, fx_w = n_sb * hd, n_fx * hd
    att = min(FOX_BLOCK, tp)
    past_pad = -(-(past + ts) // (3 * LANES)) * (3 * LANES)
    row = lambda a: a.reshape(1, -1)

    xp, xs = x_prompt, x_sample
    outs = [[] for _ in range(10)]
    for l in range(depth):
        in_cols = w_in.shape[2]
        w16t = jnp.pad(jnp.swapaxes(w_in[l], 0, 1), ((0, -in_cols % BF16_ROWS), (0, 0))).astype(BF16)
        bfr, bfc = row(b_f[l]), b_f[l].reshape(-1, 1)
        wo16, wg16 = w_o[l].astype(BF16), w_gate[l].astype(BF16)
        wu16, wd16 = w_up[l].astype(BF16), w_down[l].astype(BF16)

        c_all = jnp.concatenate([c_prompt, c_sample], axis=0)
        ada = _ada(c_all, w_ada[l], row(b_ada[l])).reshape(bp + bs, 6, d)
        ada_p, ada_s = ada[:bp], ada[bp:]

        (q_sb, k_sb, v_sb, k_sb16, v_sb16, q_fx, k_fx, v_fx, k_fx16, v_fx16, lf) = _pre(
            xp, ada_p, row(g_mix[l]), w16t, bfr, bfc, sb_w, fx_w, n_fx, True)
        o_sb = _sb_prompt(q_sb, k_sb16, v_sb16)
        f_cum = _cumsum_rows(lf.reshape(bp * n_fx, tp), ROW_BLOCK)
        fq = jnp.swapaxes(f_cum.reshape(bp, n_fx, tp), 1, 2)
        fk = jnp.swapaxes(f_cum.reshape(bp, n_fx // 2, 2, tp // att, att), 2, 3)
        o_fx = _fox_prompt(q_fx, k_fx16, v_fx16, fq, fk)
        xp = _post(xp, o_sb, o_fx, ada_p, row(g_sb_out[l]), row(g_fox_out[l]), wo16, row(g_ffn[l]),
                   wg16, wu16, wd16, row(g_final), l + 1 == depth)
        for dst, a in zip(outs[0:4], (k_sb, v_sb, k_fx, v_fx)):
            dst.append(jnp.swapaxes(a, 1, 2).reshape(bp, tp, -1, hd))
        outs[4].append(jnp.swapaxes(lf, 1, 2))

        (q_sb, k_sb, v_sb, k_sb16, v_sb16, q_fx, k_fx, v_fx, k_fx16, v_fx16, lf) = _pre(
            xs, ada_s, row(g_mix[l]), w16t, bfr, bfc, sb_w, fx_w, n_fx, False)
        time_minor = lambda a: jnp.swapaxes(a.reshape(a.shape[0], a.shape[1], -1), 1, 2)
        lf_all = jnp.concatenate([cache_fox_logf[l].astype(F32), lf], axis=1)
        lf_rows = _pad_lanes(jnp.swapaxes(lf_all, 1, 2).reshape(bs * n_fx, past + ts), past_pad)
        f_all = _cumsum_rows(lf_rows, past_pad // 3)
        fq = jnp.swapaxes(f_all.reshape(bs, n_fx, past_pad)[:, :, past:past + ts], 1, 2)
        fk = f_all.reshape(bs, n_fx // 2, 2, past_pad)
        o_sb, o_fx = _sample_attention(
            (q_sb, time_minor(k_sb16), time_minor(v_sb16), time_minor(cache_sb_k[l]), time_minor(cache_sb_v[l])),
            (q_fx, time_minor(k_fx16), time_minor(v_fx16), time_minor(cache_fox_k[l]), time_minor(cache_fox_v[l])),
            fq, fk)
        xs = _post(xs, o_sb, o_fx, ada_s, row(g_sb_out[l]), row(g_fox_out[l]), wo16, row(g_ffn[l]),
                   wg16, wu16, wd16, row(g_final), l + 1 == depth)
        for dst, a in zip(outs[5:9], (k_sb, v_sb, k_fx, v_fx)):
            dst.append(a.reshape(bs, ts, -1, hd))
        outs[9].append(lf)

    return (xp, xs) + tuple(jnp.stack(o) for o in outs)
```
